```python
import math
import jax
import jax.numpy as jnp
from jax import lax
import numpy as np

D_MODEL = 2048
BATCH = 2
SEQ = 8192
DEPTH = 2

GRID_W = 64
CTX_LEN = 256
EPS = 1e-6

SSM_WIDTH = 512
SSM_GROUP = 16
SSM_GROUPS = SSM_WIDTH // SSM_GROUP
SSM_STATE = 64
DT_MIN = 1e-3
DT_MAX = 1e-1

ATTN_HEADS = 8
ATTN_KV_HEADS = 2
HEAD_DIM = 128
ATTN_WIDTH = ATTN_HEADS * HEAD_DIM
KV_WIDTH = ATTN_KV_HEADS * HEAD_DIM
Q_BLOCK = 128
ROPE_THETA = 10000.0

MLSTM_HEADS = 4
MLSTM_HEAD_DIM = 128
MLSTM_WIDTH = MLSTM_HEADS * MLSTM_HEAD_DIM
MLSTM_CHUNK = 128
CONV_WIDTH = 3
N_GATES = 4 * MLSTM_HEADS

SECTION_SIZES = (SSM_WIDTH, SSM_WIDTH,
                 ATTN_WIDTH, KV_WIDTH, KV_WIDTH, ATTN_WIDTH,
                 MLSTM_WIDTH, MLSTM_WIDTH, MLSTM_WIDTH, MLSTM_WIDTH, N_GATES, MLSTM_WIDTH)
IN_COLS = sum(SECTION_SIZES)
MIX_WIDTH = SSM_WIDTH + ATTN_WIDTH + MLSTM_WIDTH

kernel_name = 'hybrid_s5_gqa_mlstm_dit_block'


def rmsnorm(x, w):
    xf = x.astype(jnp.float32)
    y = xf * lax.rsqrt(jnp.mean(xf * xf, axis=-1, keepdims=True) + EPS)
    return (y * w.astype(jnp.float32)).astype(x.dtype)


def split_sections(p):
    bounds = []
    acc = 0
    for s in SECTION_SIZES[:-1]:
        acc += s
        bounds.append(acc)
    return jnp.split(p, bounds, axis=-1)


def maybe_flip(t, flip, axis):
    return jnp.flip(t, axis=axis) if flip else t


def axial_rope_tables(n_tokens):
    rows = n_tokens // GRID_W
    row = jnp.broadcast_to(jnp.arange(rows, dtype=jnp.float32)[:, None], (rows, GRID_W)).reshape(-1)
    col = jnp.broadcast_to(jnp.arange(GRID_W, dtype=jnp.float32)[None, :], (rows, GRID_W)).reshape(-1)
    n_freq = HEAD_DIM // 4
    inv = ROPE_THETA ** (-jnp.arange(n_freq, dtype=jnp.float32) / n_freq)
    ang = jnp.concatenate([row[:, None] * inv, col[:, None] * inv], axis=-1)
    return jnp.cos(ang), jnp.sin(ang)


def apply_rope(x, cos, sin):
    half = x.shape[-1] // 2
    x1, x2 = x[..., :half], x[..., half:]
    cs, sn = cos[None, :, None, :], sin[None, :, None, :]
    return jnp.concatenate([x1 * cs - x2 * sn, x1 * sn + x2 * cs], axis=-1).astype(x.dtype)


def s5_discretize(a_re, a_im, log_dt, b_re, b_im):
    a_re = a_re.astype(jnp.float32)
    a_im = a_im.astype(jnp.float32)
    dt = jnp.exp(log_dt.astype(jnp.float32))[:, None]
    mag = jnp.exp(a_re * dt)
    lam_re = mag * jnp.cos(a_im * dt)
    lam_im = mag * jnp.sin(a_im * dt)
    inv_abs2 = 1.0 / (a_re * a_re + a_im * a_im)
    num_re, num_im = lam_re - 1.0, lam_im
    f_re = (num_re * a_re + num_im * a_im) * inv_abs2
    f_im = (num_im * a_re - num_re * a_im) * inv_abs2
    b_re = b_re.astype(jnp.float32)
    b_im = b_im.astype(jnp.float32)
    bb_re = f_re[:, None, :] * b_re - f_im[:, None, :] * b_im
    bb_im = f_re[:, None, :] * b_im + f_im[:, None, :] * b_re
    return lam_re, lam_im, bb_re, bb_im


def complex_affine_combine(e1, e2):
    a1r, a1i, b1r, b1i = e1
    a2r, a2i, b2r, b2i = e2
    return (a1r * a2r - a1i * a2i,
            a1r * a2i + a1i * a2r,
            a2r * b1r - a2i * b1i + b2r,
            a2r * b1i + a2i * b1r + b2i)


def s5_scan(u, h_re, h_im, lam_re, lam_im, bb_re, bb_im, c_re, c_im):
    n = u.shape[1]
    x_re = jnp.einsum('blgp,gpn->blgn', u, bb_re)
    x_im = jnp.einsum('blgp,gpn->blgn', u, bb_im)
    x_re = x_re.at[:, 0].add(lam_re * h_re - lam_im * h_im)
    x_im = x_im.at[:, 0].add(lam_re * h_im + lam_im * h_re)
    a_re = jnp.broadcast_to(lam_re, (1, n) + lam_re.shape)
    a_im = jnp.broadcast_to(lam_im, (1, n) + lam_im.shape)
    _, _, s_re, s_im = lax.associative_scan(complex_affine_combine, (a_re, a_im, x_re, x_im), axis=1)
    y = (jnp.einsum('blgn,gpn->blgp', s_re, c_re.astype(jnp.float32))
         - jnp.einsum('blgn,gpn->blgp', s_im, c_im.astype(jnp.float32)))
    return y, s_re[:, -1], s_im[:, -1]


def s5_branch(u_lat, z_lat, u_ctx, z_ctx, a_re, a_im, log_dt, b_re, b_im, c_re, c_im, d, w_glu, need_ctx):
    def grp(t):
        return t.astype(jnp.float32).reshape(t.shape[0], t.shape[1], SSM_GROUPS, SSM_GROUP)
    ul, uc = grp(u_lat), grp(u_ctx)
    d_g = d.astype(jnp.float32).reshape(SSM_GROUPS, SSM_GROUP)
    y_lat = d_g * ul
    y_ctx = d_g * uc if need_ctx else None
    bsz = ul.shape[0]
    for direction in range(2):
        rev = direction == 1
        lam_re, lam_im, bb_re, bb_im = s5_discretize(a_re[direction], a_im[direction], log_dt[direction], b_re, b_im)
        zero = jnp.zeros((bsz, SSM_GROUPS, SSM_STATE), jnp.float32)
        yc, h_re, h_im = s5_scan(maybe_flip(uc, rev, 1), zero, zero, lam_re, lam_im, bb_re, bb_im,
                                 c_re[direction], c_im[direction])
        yl, _, _ = s5_scan(maybe_flip(ul, rev, 1), h_re, h_im, lam_re, lam_im, bb_re, bb_im,
                           c_re[direction], c_im[direction])
        y_lat = y_lat + maybe_flip(yl, rev, 1)
        if need_ctx:
            y_ctx = y_ctx + maybe_flip(yc, rev, 1)

    def glu_out(y, z):
        g = jax.nn.gelu(y.reshape(y.shape[0], y.shape[1], SSM_WIDTH)).astype(z.dtype)
        return g * jax.nn.sigmoid(g @ w_glu) * jax.nn.silu(z)

    out_ctx = glu_out(y_ctx, z_ctx) if need_ctx else None
    return glu_out(y_lat, z_lat), out_ctx


def gqa_attend(q, keys, vals):
    s = jnp.einsum('bqkgd,bskd->bkgqs', q, keys).astype(jnp.float32) * (HEAD_DIM ** -0.5)
    p = jax.nn.softmax(s, axis=-1).astype(vals.dtype)
    return jnp.einsum('bkgqs,bskd->bqkgd', p, vals)


def attention_branch(q, k, v, z, q_c, k_c, v_c, z_c, q_norm_w, k_norm_w, cos, sin, need_ctx):
    bsz, n, _ = q.shape
    grp = ATTN_HEADS // ATTN_KV_HEADS

    def heads(t, nh):
        return t.reshape(t.shape[0], t.shape[1], nh, HEAD_DIM)

    ql = apply_rope(rmsnorm(heads(q, ATTN_HEADS), q_norm_w), cos, sin)
    kl = apply_rope(rmsnorm(heads(k, ATTN_KV_HEADS), k_norm_w), cos, sin)
    kc = rmsnorm(heads(k_c, ATTN_KV_HEADS), k_norm_w)
    vc = heads(v_c, ATTN_KV_HEADS)
    keys = jnp.concatenate([kc, kl], axis=1)
    vals = jnp.concatenate([vc, heads(v, ATTN_KV_HEADS)], axis=1)
    nb = n // Q_BLOCK
    qb = ql.reshape(bsz, nb, Q_BLOCK, ATTN_KV_HEADS, grp, HEAD_DIM).swapaxes(0, 1)
    o = lax.map(lambda blk: gqa_attend(blk, keys, vals), qb)
    o = o.swapaxes(0, 1).reshape(bsz, n, ATTN_WIDTH)
    out_lat = o * jax.nn.silu(z)
    out_ctx = None
    if need_ctx:
        qc = rmsnorm(heads(q_c, ATTN_HEADS), q_norm_w)
        qc = qc.reshape(bsz, qc.shape[1], ATTN_KV_HEADS, grp, HEAD_DIM)
        oc = gqa_attend(qc, kc, vc).reshape(bsz, qc.shape[1], ATTN_WIDTH)
        out_ctx = oc * jax.nn.silu(z_c)
    return out_lat, out_ctx


def centred_dwconv(x, w, b):
    pad = CONV_WIDTH // 2
    y = lax.conv_general_dilated(x, w[:, None, :].astype(x.dtype), window_strides=(1,),
                                 padding=[(pad, pad)], dimension_numbers=('NWC', 'WIO', 'NWC'),
                                 feature_group_count=x.shape[-1])
    return y + b


def mlstm_inputs(q, k, v, g, conv_w, conv_b, gate_b):
    qk = jax.nn.silu(centred_dwconv(jnp.concatenate([q, k], axis=-1), conv_w, conv_b))
    q, k = jnp.split(qk, 2, axis=-1)
    bsz, n, _ = q.shape

    def heads(t):
        return t.astype(jnp.float32).reshape(bsz, n, MLSTM_HEADS, MLSTM_HEAD_DIM).transpose(0, 2, 1, 3)

    gates = (g + gate_b).astype(jnp.float32).reshape(bsz, n, 4, MLSTM_HEADS).transpose(2, 0, 3, 1)
    return heads(q), heads(k), heads(v), gates


def mlstm_scan(q, k, v, i_pre, f_pre, state):
    bsz, nh, n, dh = q.shape
    nc = n // MLSTM_CHUNK

    def to_chunks(t):
        return jnp.moveaxis(t.reshape(t.shape[:2] + (nc, MLSTM_CHUNK) + t.shape[3:]), 2, 0)

    qs, ks, vs = to_chunks(q), to_chunks(k * (dh ** -0.5)), to_chunks(v)
    log_f = to_chunks(jax.nn.log_sigmoid(f_pre))
    log_i = to_chunks(i_pre)
    lower = jnp.tril(jnp.ones((MLSTM_CHUNK, MLSTM_CHUNK), dtype=bool))

    def step(carry, inp):
        c_st, n_st, m_st = carry
        qc, kc, vc, lf, li = inp
        b = jnp.cumsum(lf, axis=-1)
        log_d = jnp.where(lower, b[..., :, None] - b[..., None, :] + li[..., None, :], -jnp.inf)
        inter = b + m_st[..., None]
        m_t = jnp.maximum(inter, jnp.max(log_d, axis=-1))
        s = jnp.einsum('bhtd,bhsd->bhts', qc, kc) * jnp.exp(log_d - m_t[..., None])
        w_inter = jnp.exp(inter - m_t)
        num = jnp.einsum('bhts,bhse->bhte', s, vc) + w_inter[..., None] * jnp.einsum('bhed,bhtd->bhte', c_st, qc)
        den = jnp.sum(s, axis=-1) + w_inter * jnp.einsum('bhtd,bhd->bht', qc, n_st)
        h = num / jnp.maximum(jnp.abs(den), jnp.exp(-m_t))[..., None]
        b_last = b[..., -1]
        log_w = b_last[..., None] - b + li
        m_new = jnp.maximum(b_last + m_st, jnp.max(log_w, axis=-1))
        w = jnp.exp(log_w - m_new[..., None])
        decay = jnp.exp(b_last + m_st - m_new)
        c_new = decay[..., None, None] * c_st + jnp.einsum('bhs,bhse,bhsd->bhed', w, vc, kc)
        n_new = decay[..., None] * n_st + jnp.einsum('bhs,bhsd->bhd', w, kc)
        return (c_new, n_new, m_new), h

    state, hs = lax.scan(step, state, (qs, ks, vs, log_f, log_i))
    return jnp.moveaxis(hs, 0, 2).reshape(bsz, nh, n, dh), state


def mlstm_branch(q, k, v, o, g, z, q_c, k_c, v_c, o_c, g_c, z_c, conv_w, conv_b, gate_b, norm_w, need_ctx):
    ql, kl, vl, gl = mlstm_inputs(q, k, v, g, conv_w, conv_b, gate_b)
    qc, kc, vc, gc = mlstm_inputs(q_c, k_c, v_c, g_c, conv_w, conv_b, gate_b)
    bsz = ql.shape[0]
    h_lat = None
    h_ctx = None
    for direction in range(2):
        rev = direction == 1
        state0 = (jnp.zeros((bsz, MLSTM_HEADS, MLSTM_HEAD_DIM, MLSTM_HEAD_DIM), jnp.float32),
                  jnp.zeros((bsz, MLSTM_HEADS, MLSTM_HEAD_DIM), jnp.float32),
                  jnp.zeros((bsz, MLSTM_HEADS), jnp.float32))
        hc, st = mlstm_scan(maybe_flip(qc, rev, 2), maybe_flip(kc, rev, 2), maybe_flip(vc, rev, 2),
                            maybe_flip(gc[2 * direction], rev, 2), maybe_flip(gc[2 * direction + 1], rev, 2), state0)
        hl, _ = mlstm_scan(maybe_flip(ql, rev, 2), maybe_flip(kl, rev, 2), maybe_flip(vl, rev, 2),
                           maybe_flip(gl[2 * direction], rev, 2), maybe_flip(gl[2 * direction + 1], rev, 2), st)
        hl = maybe_flip(hl, rev, 2)
        h_lat = hl if h_lat is None else h_lat + hl
        if need_ctx:
            hc = maybe_flip(hc, rev, 2)
            h_ctx = hc if h_ctx is None else h_ctx + hc

    def out(h, og, zg):
        bsz_, nh, n, dh = h.shape
        hn = rmsnorm(h.transpose(0, 2, 1, 3), norm_w.reshape(nh, dh)).reshape(bsz_, n, MLSTM_WIDTH)
        return (jax.nn.sigmoid(og.astype(jnp.float32)) * hn * jax.nn.silu(zg.astype(jnp.float32))).astype(zg.dtype)

    out_ctx = out(h_ctx, o_c, z_c) if need_ctx else None
    return out(h_lat, o, z), out_ctx


def setup_inputs(seed: int = 0) -> dict:
    key = jax.random.key(seed)
    ks = jax.random.split(key, 26)
    f32 = jnp.float32

    def nrm(k, shape, s):
        return s * jax.random.normal(k, shape, f32)

    D = D_MODEL
    f_bias = jnp.linspace(3.0, 6.0, MLSTM_HEADS, dtype=f32)
    zb = jnp.zeros_like(f_bias)
    gate_base = jnp.stack([zb, f_bias, zb, f_bias])
    a_im_base = jnp.pi * jnp.arange(SSM_STATE, dtype=f32)
    return {
        'x': nrm(ks[0], (BATCH, SEQ, D), 1.0),
        'c': nrm(ks[1], (BATCH, D), 1.0),
        'ctx': nrm(ks[2], (BATCH, CTX_LEN, D), 1.0),
        'c_ctx': nrm(ks[3], (D,), 1.0),
        'norm_w': 1.0 + nrm(ks[4], (DEPTH, D), 0.02),
        'ada_w': nrm(ks[5], (DEPTH, D, 3 * D), 0.5 * D ** -0.5),
        'ada_b': nrm(ks[6], (DEPTH, 3 * D), 0.02),
        'w_in': nrm(ks[7], (DEPTH, D, IN_COLS), D ** -0.5),
        'mlstm_gate_b': (gate_base[None] + nrm(ks[8], (DEPTH, 4, MLSTM_HEADS), 0.1)).reshape(DEPTH, N_GATES),
        'ssm_a_re': -0.5 + nrm(ks[9], (DEPTH, 2, SSM_GROUPS, SSM_STATE), 0.01),
        'ssm_a_im': a_im_base + nrm(ks[10], (DEPTH, 2, SSM_GROUPS, SSM_STATE), 0.01),
        'ssm_log_dt': jax.random.uniform(ks[11], (DEPTH, 2, SSM_GROUPS), f32,
                                         minval=math.log(DT_MIN), maxval=math.log(DT_MAX)),
        'ssm_b_re': nrm(ks[12], (DEPTH, SSM_GROUPS, SSM_GROUP, SSM_STATE), (2 * SSM_GROUP) ** -0.5),
        'ssm_b_im': nrm(ks[13], (DEPTH, SSM_GROUPS, SSM_GROUP, SSM_STATE), (2 * SSM_GROUP) ** -0.5),
        'ssm_c_re': nrm(ks[14], (DEPTH, 2, SSM_GROUPS, SSM_GROUP, SSM_STATE), (2 * SSM_STATE) ** -0.5),
        'ssm_c_im': nrm(ks[15], (DEPTH, 2, SSM_GROUPS, SSM_GROUP, SSM_STATE), (2 * SSM_STATE) ** -0.5),
        'ssm_d': nrm(ks[16], (DEPTH, SSM_WIDTH), 1.0),
        'ssm_w_glu': nrm(ks[17], (DEPTH, SSM_WIDTH, SSM_WIDTH), SSM_WIDTH ** -0.5),
        'attn_q_norm': 1.0 + nrm(ks[18], (DEPTH, HEAD_DIM), 0.02),
        'attn_k_norm': 1.0 + nrm(ks[19], (DEPTH, HEAD_DIM), 0.02),
        'mlstm_conv_w': nrm(ks[20], (DEPTH, CONV_WIDTH, 2 * MLSTM_WIDTH), CONV_WIDTH ** -0.5),
        'mlstm_conv_b': nrm(ks[21], (DEPTH, 2 * MLSTM_WIDTH), 0.02),
        'mlstm_norm_w': 1.0 + nrm(ks[22], (DEPTH, MLSTM_WIDTH), 0.02),
        'w_out': nrm(ks[23], (DEPTH, MIX_WIDTH, D), MIX_WIDTH ** -0.5),
        'final_norm_w': 1.0 + nrm(ks[24], (D,), 0.02),
    }


def reference(x, c, ctx, c_ctx, norm_w, ada_w, ada_b, w_in, mlstm_gate_b, ssm_a_re, ssm_a_im, ssm_log_dt,
              ssm_b_re, ssm_b_im, ssm_c_re, ssm_c_im, ssm_d, ssm_w_glu, attn_q_norm, attn_k_norm,
              mlstm_conv_w, mlstm_conv_b, mlstm_norm_w, w_out, final_norm_w):
    n_lat = x.shape[1]
    cos, sin = axial_rope_tables(n_lat)
    c_act = jax.nn.silu(c)
    cc_act = jax.nn.silu(c_ctx)
    h_lat, h_ctx = x, ctx
    for layer in range(DEPTH):
        need_ctx = layer < DEPTH - 1
        shift, scale, gate = jnp.split(c_act @ ada_w[layer] + ada_b[layer], 3, axis=-1)
        shift_c, scale_c, gate_c = jnp.split(cc_act @ ada_w[layer] + ada_b[layer], 3, axis=-1)
        xn = rmsnorm(h_lat, norm_w[layer]) * (1.0 + scale[:, None]) + shift[:, None]
        xc = rmsnorm(h_ctx, norm_w[layer]) * (1.0 + scale_c) + shift_c
        (s5_u, s5_z, at_q, at_k, at_v, at_z,
         ml_q, ml_k, ml_v, ml_o, ml_g, ml_z) = split_sections(xn @ w_in[layer])
        (s5_uc, s5_zc, at_qc, at_kc, at_vc, at_zc,
         ml_qc, ml_kc, ml_vc, ml_oc, ml_gc, ml_zc) = split_sections(xc @ w_in[layer])
        ya, ya_c = s5_branch(s5_u, s5_z, s5_uc, s5_zc, ssm_a_re[layer], ssm_a_im[layer], ssm_log_dt[layer],
                             ssm_b_re[layer], ssm_b_im[layer], ssm_c_re[layer], ssm_c_im[layer],
                             ssm_d[layer], ssm_w_glu[layer], need_ctx)
        yb, yb_c = attention_branch(at_q, at_k, at_v, at_z, at_qc, at_kc, at_vc, at_zc,
                                    attn_q_norm[layer], attn_k_norm[layer], cos, sin, need_ctx)
        yc, yc_c = mlstm_branch(ml_q, ml_k, ml_v, ml_o, ml_g, ml_z, ml_qc, ml_kc, ml_vc, ml_oc, ml_gc, ml_zc,
                                mlstm_conv_w[layer], mlstm_conv_b[layer], mlstm_gate_b[layer],
                                mlstm_norm_w[layer], need_ctx)
        h_lat = h_lat + gate[:, None] * (jnp.concatenate([ya, yb, yc], axis=-1) @ w_out[layer])
        if need_ctx:
            h_ctx = h_ctx + gate_c * (jnp.concatenate([ya_c, yb_c, yc_c], axis=-1) @ w_out[layer])
    return rmsnorm(h_lat, final_norm_w)
```

```python
import functools
import math

import jax
import jax.numpy as jnp
from jax import lax
from jax.experimental import pallas as pl
from jax.experimental.pallas import tpu as pltpu

F32 = jnp.float32
BF16 = jnp.bfloat16
EPS = 1e-6

SSM_W = 512
SSM_P = 16
SSM_G = 32
SSM_N = 64
S5_T = 16
S5_SLAB_G = 8
S5_SLABS = SSM_G // S5_SLAB_G
S5_ST = S5_SLAB_G * SSM_N
AH = 8
AKV = 2
AGRP = AH // AKV
HD = 128
AW = AH * HD
KVW = AKV * HD
GRID_W = 64
ROPE_THETA = 10000.0
MH = 4
MD = 128
MW = MH * MD
MCH = 128
C_U, C_Z, C_AQ, C_AK, C_AV, C_AZ, C_MQ, C_MK, C_MV, C_MO, C_MZ = (
    0, 512, 1024, 2048, 2304, 2560, 3584, 4096, 4608, 5120, 5632)
PROJ_W = 6144
MIX_W = SSM_W + AW + MW

ROW_TILE = 256
VMEM_LIMIT = 48 * 1024 * 1024


def _cparams(*sem):
    return pltpu.CompilerParams(dimension_semantics=sem, vmem_limit_bytes=VMEM_LIMIT)


def _dot(a, b):
    return jnp.dot(a, b, preferred_element_type=F32)


def _dot_nt(a, b):
    return lax.dot_general(a, b, (((1,), (1,)), ((), ())), preferred_element_type=F32)


def _dot_tn(a, b):
    return lax.dot_general(a, b, (((0,), (0,)), ((), ())), preferred_element_type=F32)


def _split3(x):
    hi = x.astype(BF16)
    r1 = x - hi.astype(F32)
    mid = r1.astype(BF16)
    lo = (r1 - mid.astype(F32)).astype(BF16)
    return hi, mid, lo


def _silu(x):
    return x * jax.nn.sigmoid(x)


def _log_sigmoid(x):
    return jnp.minimum(x, 0.0) - jnp.log1p(jnp.exp(-jnp.abs(x)))


def _rms(x, w):
    return x * lax.rsqrt(jnp.mean(x * x, axis=-1, keepdims=True) + EPS) * w


def _pick_tile(n, cap, mult):
    best = mult
    for t in range(mult, min(n, cap) + 1, mult):
        if n % t == 0:
            best = t
    return best


def _ada_kernel(c_ref, w_ref, b_ref, o_ref):
    a = _silu(c_ref[...])
    w = w_ref[0]
    a_hi = a.astype(BF16)
    a_lo = (a - a_hi.astype(F32)).astype(BF16)
    w_hi = w.astype(BF16)
    w_lo = (w - w_hi.astype(F32)).astype(BF16)
    o_ref[0] = _dot(a_hi, w_hi) + _dot(a_hi, w_lo) + _dot(a_lo, w_hi) + b_ref[0]


def _ada_call(c_rows, ada_w, ada_b):
    depth, d, n3 = ada_w.shape
    tn = 512
    return pl.pallas_call(
        _ada_kernel,
        grid=(depth, n3 // tn),
        in_specs=[
            pl.BlockSpec((8, d), lambda l, j: (0, 0)),
            pl.BlockSpec((1, d, tn), lambda l, j: (l, 0, j)),
            pl.BlockSpec((1, 1, tn), lambda l, j: (l, 0, j)),
        ],
        out_specs=pl.BlockSpec((1, 8, tn), lambda l, j: (l, 0, j)),
        out_shape=jax.ShapeDtypeStruct((depth, 8, n3), F32),
        compiler_params=_cparams("parallel", "parallel"),
        name="ada_mod",
    )(c_rows, ada_w, ada_b.reshape(depth, 1, n3))


def _prenorm_kernel(x_ref, ctx_ref, nw_ref, sh_ref, sc_ref, o_ref, *, nct):
    i = pl.program_id(1)

    def emit(h):
        y = _rms(h, nw_ref[0])
        o_ref[0] = (y * (1.0 + sc_ref[0]) + sh_ref[0]).astype(BF16)

    @pl.when(i < nct)
    def _():
        emit(ctx_ref[0])

    @pl.when(i >= nct)
    def _():
        emit(x_ref[0])


def _mod_spec(d, layer, part, nct, nb):
    def idx(b, i):
        return (layer * 8 + jnp.where(i < nct, nb, b), 0, part)
    return pl.BlockSpec((1, 1, d), idx)


def _prenorm_call(x, ctx, nw, mod, layer):
    bsz, n, d = x.shape
    nctx = ctx.shape[1]
    tr = ROW_TILE
    nct = nctx // tr
    s = nctx + n
    return pl.pallas_call(
        functools.partial(_prenorm_kernel, nct=nct),
        grid=(bsz, s // tr),
        in_specs=[
            pl.BlockSpec((1, tr, d), lambda b, i: (b, jnp.maximum(i - nct, 0), 0)),
            pl.BlockSpec((1, tr, d), lambda b, i: (b, jnp.minimum(i, nct - 1), 0)),
            pl.BlockSpec((1, 1, d), lambda b, i: (layer, 0, 0)),
            _mod_spec(d, layer, 0, nct, bsz),
            _mod_spec(d, layer, 1, nct, bsz),
        ],
        out_specs=pl.BlockSpec((1, tr, d), lambda b, i: (b, i, 0)),
        out_shape=jax.ShapeDtypeStruct((bsz, s, d), BF16),
        compiler_params=_cparams("parallel", "parallel"),
        name="prenorm",
    )(x, ctx, nw, mod, mod)


def _matmul_kernel(x_ref, w_ref, o_ref):
    o_ref[...] = _dot(x_ref[...], w_ref[...])


def _inproj_call(xn2d, w):
    m, d = xn2d.shape
    n = w.shape[1]
    tm = _pick_tile(m, 1056, 16)
    tn = 1024
    return pl.pallas_call(
        _matmul_kernel,
        grid=(n // tn, m // tm),
        in_specs=[
            pl.BlockSpec((tm, d), lambda j, i: (i, 0)),
            pl.BlockSpec((d, tn), lambda j, i: (0, j)),
        ],
        out_specs=pl.BlockSpec((tm, tn), lambda j, i: (i, j)),
        out_shape=jax.ShapeDtypeStruct((m, n), F32),
        compiler_params=_cparams("parallel", "parallel"),
        name="in_proj",
    )(xn2d, w)


def _gates_kernel(x_ref, wc_ref, wr_ref, bc_ref, br_ref, gc_ref, gr_ref):
    x = x_ref[0]
    for dirn in range(2):
        gc_ref[0, dirn] = _dot(x, wc_ref[dirn]) + bc_ref[dirn]
        gr_ref[0, dirn] = _dot_nt(wr_ref[dirn], x) + br_ref[dirn][:, :1]


def _gates_call(xn, w_gc, w_gr, b_gc, b_gr):
    bsz, s, d = xn.shape
    tr = ROW_TILE
    return pl.pallas_call(
        _gates_kernel,
        grid=(bsz, s // tr),
        in_specs=[
            pl.BlockSpec((1, tr, d), lambda b, i: (b, i, 0)),
            pl.BlockSpec((2, d, 128), lambda b, i: (0, 0, 0)),
            pl.BlockSpec((2, 8, d), lambda b, i: (0, 0, 0)),
            pl.BlockSpec((2, 1, 128), lambda b, i: (0, 0, 0)),
            pl.BlockSpec((2, 8, 128), lambda b, i: (0, 0, 0)),
        ],
        out_specs=[
            pl.BlockSpec((1, 2, tr, 128), lambda b, i: (b, 0, i, 0)),
            pl.BlockSpec((1, 2, 8, tr), lambda b, i: (b, 0, 0, i)),
        ],
        out_shape=[
            jax.ShapeDtypeStruct((bsz, 2, s, 128), F32),
            jax.ShapeDtypeStruct((bsz, 2, 8, s), F32),
        ],
        compiler_params=_cparams("parallel", "parallel"),
        name="mlstm_gates",
    )(xn, w_gc, w_gr, b_gc, b_gr)


def _s5_weights(a_re, a_im, log_dt, b_re, b_im, c_re, c_im):
    t_ = S5_T
    a_re = a_re.astype(F32)
    a_im = a_im.astype(F32)
    dt = jnp.exp(log_dt.astype(F32))[..., None]
    mag = jnp.exp(a_re * dt)
    lam_re = mag * jnp.cos(a_im * dt)
    lam_im = mag * jnp.sin(a_im * dt)
    inv_abs2 = 1.0 / (a_re * a_re + a_im * a_im)
    num_re, num_im = lam_re - 1.0, lam_im
    f_re = (num_re * a_re + num_im * a_im) * inv_abs2
    f_im = (num_im * a_re - num_re * a_im) * inv_abs2
    b_re = b_re.astype(F32)[None]
    b_im = b_im.astype(F32)[None]
    bb_re = f_re[:, :, None, :] * b_re - f_im[:, :, None, :] * b_im
    bb_im = f_re[:, :, None, :] * b_im + f_im[:, :, None, :] * b_re
    c_re = c_re.astype(F32)
    c_im = c_im.astype(F32)

    pr = [jnp.ones_like(lam_re)]
    pi = [jnp.zeros_like(lam_im)]
    for _ in range(t_):
        pr.append(pr[-1] * lam_re - pi[-1] * lam_im)
        pi.append(pr[-2] * lam_im + pi[-1] * lam_re)
    pw_re = jnp.stack(pr)
    pw_im = jnp.stack(pi)

    eye = jnp.eye(S5_SLAB_G, dtype=F32)
    tt = jnp.arange(t_)

    def per_dir(arr_f, arr_b):
        return jnp.stack([arr_f, arr_b])

    e_in = per_dir(t_ - 1 - tt, tt)
    dsel = jnp.arange(2)[:, None]
    pin_re = pw_re[e_in, dsel]
    pin_im = pw_im[e_in, dsel]
    wr = bb_re[:, None] * pin_re[:, :, :, None, :] - bb_im[:, None] * pin_im[:, :, :, None, :]
    wi = bb_re[:, None] * pin_im[:, :, :, None, :] + bb_im[:, None] * pin_re[:, :, :, None, :]

    def in_blk(w):
        w = w.reshape(2, t_, S5_SLABS, S5_SLAB_G, SSM_P, SSM_N).transpose(0, 2, 1, 3, 4, 5)
        w = w[:, :, :, :, :, None, :] * eye[None, None, None, :, None, :, None]
        return w.reshape(2, S5_SLABS, t_ * 128, S5_ST)

    win = jnp.concatenate([in_blk(wr), in_blk(wi)], axis=-1).astype(BF16)

    e_out = per_dir(tt + 1, t_ - tt)
    pout_re = pw_re[e_out, dsel]
    pout_im = pw_im[e_out, dsel]
    clr = c_re[:, None] * pout_re[:, :, :, None, :] - c_im[:, None] * pout_im[:, :, :, None, :]
    cli = c_re[:, None] * pout_im[:, :, :, None, :] + c_im[:, None] * pout_re[:, :, :, None, :]

    def out_blk(w):
        w = w.reshape(2, t_, S5_SLABS, S5_SLAB_G, SSM_P, SSM_N).transpose(0, 2, 3, 5, 1, 4)
        w = w[:, :, :, :, :, None, :] * eye[None, None, :, None, None, :, None]
        return w.reshape(2, S5_SLABS, S5_ST, t_ * 128)

    wout = jnp.concatenate([out_blk(clr), out_blk(-cli)], axis=-2).astype(BF16)

    kr = c_re[:, None] * pw_re[:t_].transpose(1, 0, 2, 3)[:, :, :, None, :] \
        - c_im[:, None] * pw_im[:t_].transpose(1, 0, 2, 3)[:, :, :, None, :]
    ki = c_re[:, None] * pw_im[:t_].transpose(1, 0, 2, 3)[:, :, :, None, :] \
        + c_im[:, None] * pw_re[:t_].transpose(1, 0, 2, 3)[:, :, :, None, :]
    kk = (jnp.einsum('dtgpn,dgqn->dtgpq', kr, bb_re, precision=lax.Precision.HIGHEST)
          - jnp.einsum('dtgpn,dgqn->dtgpq', ki, bb_im, precision=lax.Precision.HIGHEST))
    lag = tt[None, :] - tt[:, None]
    kf = kk[0][jnp.clip(lag, 0, t_ - 1)] * (lag >= 0)[:, :, None, None, None]
    kb = kk[1][jnp.clip(-lag, 0, t_ - 1)] * (lag <= 0)[:, :, None, None, None]
    mf = (kf + kb).reshape(t_, t_, S5_SLABS, S5_SLAB_G, SSM_P, SSM_P).transpose(2, 0, 3, 5, 1, 4)
    mf = mf[:, :, :, :, :, None, :] * eye[None, None, :, None, None, :, None]
    mtoe = mf.reshape(S5_SLABS, t_ * 128, t_ * 128).astype(BF16)

    def lam_row(v):
        return v.reshape(2, S5_SLABS, 1, S5_ST)

    lam_t = jnp.concatenate([lam_row(pw_re[t_]), lam_row(pw_im[t_])], axis=-1)
    return win, wout, mtoe, lam_t


def _s5_statein_kernel(u_ref, w_ref, o_ref):
    o_ref[0, 0, 0] = _dot(u_ref[0, 0], w_ref[0, 0])


def _s5_statein_call(uf, win):
    bsz, slabs, nch, kdim = uf.shape
    st2 = win.shape[-1]
    return pl.pallas_call(
        _s5_statein_kernel,
        grid=(2, slabs, bsz),
        in_specs=[
            pl.BlockSpec((1, 1, nch, kdim), lambda d, s, b: (b, s, 0, 0)),
            pl.BlockSpec((1, 1, kdim, st2), lambda d, s, b: (d, s, 0, 0)),
        ],
        out_specs=pl.BlockSpec((1, 1, 1, nch, st2), lambda d, s, b: (d, b, s, 0, 0)),
        out_shape=jax.ShapeDtypeStruct((2, bsz, slabs, nch, st2), F32),
        compiler_params=_cparams("parallel", "parallel", "parallel"),
        name="s5_state_in",
    )(uf, win)


def _s5_scan_kernel(s_ref, lam_ref, h_ref, *, nch, nctx_ch):
    dirn = pl.program_id(0)
    lam = lam_ref[0, 0]
    lr = lam[:, :S5_ST]
    li = lam[:, S5_ST:]

    def step(k, carry):
        hr, hi = carry
        c_b = jnp.where(k < nctx_ch, nctx_ch - 1 - k, nch - 1 - (k - nctx_ch))
        c = jnp.where(dirn == 0, k, c_b)
        h_ref[0, 0, 0, pl.ds(c, 1), :] = jnp.concatenate([hr, hi], axis=-1).astype(h_ref.dtype)
        s = s_ref[0, 0, 0, pl.ds(c, 1), :]
        sr = s[:, :S5_ST]
        si = s[:, S5_ST:]
        return lr * hr - li * hi + sr, lr * hi + li * hr + si

    zero = jnp.zeros((1, S5_ST), F32)
    lax.fori_loop(0, nch, step, (zero, zero))


def _s5_scan_call(sin, lam_t, nctx_ch):
    _, bsz, slabs, nch, st2 = sin.shape
    return pl.pallas_call(
        functools.partial(_s5_scan_kernel, nch=nch, nctx_ch=nctx_ch),
        grid=(2, bsz, slabs),
        in_specs=[
            pl.BlockSpec((1, 1, 1, nch, st2), lambda d, b, s: (d, b, s, 0, 0)),
            pl.BlockSpec((1, 1, 1, st2), lambda d, b, s: (d, s, 0, 0)),
        ],
        out_specs=pl.BlockSpec((1, 1, 1, nch, st2), lambda d, b, s: (d, b, s, 0, 0)),
        out_shape=jax.ShapeDtypeStruct((2, bsz, slabs, nch, st2), F32),
        compiler_params=_cparams("parallel", "parallel", "parallel"),
        name="s5_scan",
    )(sin, lam_t)


def _s5_out_kernel(u_ref, h_ref, m_ref, w_ref, o_ref):
    acc = _dot(u_ref[0, 0], m_ref[0])
    acc += _dot(h_ref[0, 0, 0].astype(BF16), w_ref[0, 0])
    acc += _dot(h_ref[1, 0, 0].astype(BF16), w_ref[1, 0])
    o_ref[0, 0] = acc


def _s5_out_call(uf, hst, mtoe, wout):
    bsz, slabs, nch, kdim = uf.shape
    st2 = hst.shape[-1]
    tn = 512
    return pl.pallas_call(
        _s5_out_kernel,
        grid=(slabs, kdim // tn, bsz),
        in_specs=[
            pl.BlockSpec((1, 1, nch, kdim), lambda s, j, b: (b, s, 0, 0)),
            pl.BlockSpec((2, 1, 1, nch, st2), lambda s, j, b: (0, b, s, 0, 0)),
            pl.BlockSpec((1, kdim, tn), lambda s, j, b: (s, 0, j)),
            pl.BlockSpec((2, 1, st2, tn), lambda s, j, b: (0, s, 0, j)),
        ],
        out_specs=pl.BlockSpec((1, 1, nch, tn), lambda s, j, b: (b, s, 0, j)),
        out_shape=jax.ShapeDtypeStruct((bsz, slabs, nch, kdim), F32),
        compiler_params=_cparams("parallel", "parallel", "parallel"),
        name="s5_out",
    )(uf, hst, mtoe, wout)


def _s5_glu_kernel(y_ref, u_ref, z_ref, d_ref, w_ref, o_ref):
    y = y_ref[0] + d_ref[...] * u_ref[0]
    g = jax.nn.gelu(y, approximate=True)
    gate = jax.nn.sigmoid(_dot(g.astype(BF16), w_ref[...]))
    o_ref[0] = (g * gate * _silu(z_ref[0])).astype(BF16)


def _s5_glu_call(yssm, proj, d_row, w_glu):
    bsz, s, _ = yssm.shape
    tr = ROW_TILE
    return pl.pallas_call(
        _s5_glu_kernel,
        grid=(bsz, s // tr),
        in_specs=[
            pl.BlockSpec((1, tr, SSM_W), lambda b, i: (b, i, 0)),
            pl.BlockSpec((1, tr, SSM_W), lambda b, i: (b, i, C_U // SSM_W)),
            pl.BlockSpec((1, tr, SSM_W), lambda b, i: (b, i, C_Z // SSM_W)),
            pl.BlockSpec((1, SSM_W), lambda b, i: (0, 0)),
            pl.BlockSpec((SSM_W, SSM_W), lambda b, i: (0, 0)),
        ],
        out_specs=pl.BlockSpec((1, tr, SSM_W), lambda b, i: (b, i, 0)),
        out_shape=jax.ShapeDtypeStruct((bsz, s, SSM_W), BF16),
        compiler_params=_cparams("parallel", "parallel"),
        name="s5_glu",
    )(yssm, proj, proj, d_row, w_glu)


def _s5_branch(proj, p, nctx):
    bsz, s, _ = proj.shape
    nch = s // S5_T
    win, wout, mtoe, lam_t = _s5_weights(p['a_re'], p['a_im'], p['log_dt'], p['b_re'], p['b_im'],
                                         p['c_re'], p['c_im'])
    u = proj[:, :, C_U:C_U + SSM_W].astype(BF16)
    uf = u.reshape(bsz, nch, S5_T, S5_SLABS, 128).transpose(0, 3, 1, 2, 4).reshape(bsz, S5_SLABS, nch, S5_T * 128)
    sin = _s5_statein_call(uf, win)
    hst = _s5_scan_call(sin, lam_t, nctx // S5_T)
    yf = _s5_out_call(uf, hst, mtoe, wout)
    yssm = yf.reshape(bsz, S5_SLABS, nch, S5_T, 128).transpose(0, 2, 3, 1, 4).reshape(bsz, s, SSM_W)
    return _s5_glu_call(yssm, proj, p['d'].astype(F32).reshape(1, SSM_W), p['w_glu'].astype(BF16))


def _rope_tables(n_lat, nctx):
    rows = n_lat // GRID_W
    row = jnp.broadcast_to(jnp.arange(rows, dtype=F32)[:, None], (rows, GRID_W)).reshape(-1)
    col = jnp.broadcast_to(jnp.arange(GRID_W, dtype=F32)[None, :], (rows, GRID_W)).reshape(-1)
    n_freq = HD // 4
    inv = ROPE_THETA ** (-jnp.arange(n_freq, dtype=F32) / n_freq)
    ang = jnp.concatenate([row[:, None] * inv, col[:, None] * inv], axis=-1)
    cs, sn = jnp.cos(ang), jnp.sin(ang)
    cos2 = jnp.concatenate([cs, cs], axis=-1)
    sin2 = jnp.concatenate([-sn, sn], axis=-1)
    cos2 = jnp.concatenate([jnp.ones((nctx, HD), F32), cos2], axis=0)
    sin2 = jnp.concatenate([jnp.zeros((nctx, HD), F32), sin2], axis=0)
    return cos2, sin2


def _attn_prep_kernel(q_ref, k_ref, v_ref, cos_ref, sin_ref, qw_ref, kw_ref, qo_ref, ko_ref, vo_ref):
    cs = cos_ref[...]
    sn = sin_ref[...]

    def norm_rope(x, w):
        y = _rms(x, w)
        return y * cs + pltpu.roll(y, HD // 2, 1) * sn

    q = q_ref[0]
    for h in range(AH):
        r = norm_rope(q[:, h * HD:(h + 1) * HD], qw_ref[...])
        qo_ref[0, :, h * HD:(h + 1) * HD] = (r * (HD ** -0.5)).astype(BF16)
    k = k_ref[0]
    for h in range(AKV):
        ko_ref[0, :, h * HD:(h + 1) * HD] = norm_rope(k[:, h * HD:(h + 1) * HD], kw_ref[...]).astype(BF16)
    vo_ref[0] = v_ref[0].astype(BF16)


def _attn_prep_call(proj, cos2, sin2, qw, kw):
    bsz, s, _ = proj.shape
    tr = ROW_TILE
    return pl.pallas_call(
        _attn_prep_kernel,
        grid=(bsz, s // tr),
        in_specs=[
            pl.BlockSpec((1, tr, AW), lambda b, i: (b, i, C_AQ // AW)),
            pl.BlockSpec((1, tr, KVW), lambda b, i: (b, i, C_AK // KVW)),
            pl.BlockSpec((1, tr, KVW), lambda b, i: (b, i, C_AV // KVW)),
            pl.BlockSpec((tr, HD), lambda b, i: (i, 0)),
            pl.BlockSpec((tr, HD), lambda b, i: (i, 0)),
            pl.BlockSpec((1, HD), lambda b, i: (0, 0)),
            pl.BlockSpec((1, HD), lambda b, i: (0, 0)),
        ],
        out_specs=[
            pl.BlockSpec((1, tr, AW), lambda b, i: (b, i, 0)),
            pl.BlockSpec((1, tr, KVW), lambda b, i: (b, i, 0)),
            pl.BlockSpec((1, tr, KVW), lambda b, i: (b, i, 0)),
        ],
        out_shape=[
            jax.ShapeDtypeStruct((bsz, s, AW), BF16),
            jax.ShapeDtypeStruct((bsz, s, KVW), BF16),
            jax.ShapeDtypeStruct((bsz, s, KVW), BF16),
        ],
        compiler_params=_cparams("parallel", "parallel"),
        name="attn_prep",
    )(proj, proj, proj, cos2, sin2, qw, kw)


def _attn_kernel(q_ref, k_ref, v_ref, z_ref, o_ref, *, tq, nct, nctx, ck, nck):
    qi = pl.program_id(2)
    q = q_ref[0]
    q4 = jnp.concatenate([q[:, h * HD:(h + 1) * HD] for h in range(AGRP)], axis=0)

    def finish(acc, l):
        o = acc / l
        z = z_ref[0]
        for h in range(AGRP):
            o_ref[0, :, h * HD:(h + 1) * HD] = (
                o[h * tq:(h + 1) * tq] * _silu(z[:, h * HD:(h + 1) * HD])).astype(BF16)

    @pl.when(qi < nct)
    def _():
        s = _dot_nt(q4, k_ref[0, :nctx, :])
        m = jnp.max(s, axis=-1, keepdims=True)
        p = jnp.exp(s - m)
        l = jnp.sum(p, axis=-1, keepdims=True)
        finish(_dot(p.astype(BF16), v_ref[0, :nctx, :]), l)

    @pl.when(qi >= nct)
    def _():
        def body(j, carry):
            m, l, acc = carry
            off = pl.multiple_of(j * ck, ck)
            s = _dot_nt(q4, k_ref[0, pl.ds(off, ck), :])
            m_new = jnp.maximum(m, jnp.max(s, axis=-1, keepdims=True))
            alpha = jnp.exp(m - m_new)
            p = jnp.exp(s - m_new)
            l = alpha * l + jnp.sum(p, axis=-1, keepdims=True)
            acc = alpha * acc + _dot(p.astype(BF16), v_ref[0, pl.ds(off, ck), :])
            return m_new, l, acc

        rows = AGRP * tq
        m0 = jnp.full((rows, 1), -jnp.inf, F32)
        l0 = jnp.zeros((rows, 1), F32)
        a0 = jnp.zeros((rows, HD), F32)
        _, l, acc = lax.fori_loop(0, nck, body, (m0, l0, a0))
        finish(acc, l)


def _attn_call(qs, ks, vs, proj, nctx):
    bsz, s, _ = qs.shape
    tq = ROW_TILE
    nct = nctx // tq
    ck = _pick_tile(s, 1536, 128)
    nck = s // ck
    zw = AGRP * HD
    return pl.pallas_call(
        functools.partial(_attn_kernel, tq=tq, nct=nct, nctx=nctx, ck=ck, nck=nck),
        grid=(bsz, AKV, s // tq),
        in_specs=[
            pl.BlockSpec((1, tq, zw), lambda b, g, i: (b, i, g)),
            pl.BlockSpec((1, s, HD), lambda b, g, i: (b, 0, g)),
            pl.BlockSpec((1, s, HD), lambda b, g, i: (b, 0, g)),
            pl.BlockSpec((1, tq, zw), lambda b, g, i: (b, i, C_AZ // zw + g)),
        ],
        out_specs=pl.BlockSpec((1, tq, zw), lambda b, g, i: (b, i, g)),
        out_shape=jax.ShapeDtypeStruct((bsz, s, AW), BF16),
        compiler_params=_cparams("parallel", "parallel", "parallel"),
        name="attention",
    )(qs, ks, vs, proj)


def _mlstm_prep_kernel(q_ref, k_ref, qp_ref, kp_ref, qn_ref, kn_ref, w_ref, b_ref, qo_ref, ko_ref,
                       *, tr, nct, nt):
    i = pl.program_id(1)
    first = jnp.logical_or(i == 0, i == nct)
    last = jnp.logical_or(i == nct - 1, i == nt - 1)
    rid = lax.broadcasted_iota(jnp.int32, (tr, MW), 0)
    w = w_ref[...]
    bias = b_ref[...]

    def conv(x, prev_row, next_row, off):
        prev_row = jnp.where(first, 0.0, prev_row)
        next_row = jnp.where(last, 0.0, next_row)
        xp = jnp.where(rid == 0, prev_row, pltpu.roll(x, 1, 0))
        xn = jnp.where(rid == tr - 1, next_row, pltpu.roll(x, tr - 1, 0))
        y = (w[0:1, off:off + MW] * xp + w[1:2, off:off + MW] * x + w[2:3, off:off + MW] * xn
             + bias[:, off:off + MW])
        return _silu(y)

    qo_ref[0] = conv(q_ref[0], qp_ref[0, 7:8, :], qn_ref[0, 0:1, :], 0).astype(BF16)
    ko_ref[0] = (conv(k_ref[0], kp_ref[0, 7:8, :], kn_ref[0, 0:1, :], MW) * (MD ** -0.5)).astype(BF16)


def _mlstm_prep_call(proj, conv_w, conv_b, nctx):
    bsz, s, _ = proj.shape
    tr = ROW_TILE
    nt = s // tr
    nct = nctx // tr
    r8 = tr // 8
    n8 = s // 8
    cq = C_MQ // MW
    ck = C_MK // MW

    def prev_map(c):
        return lambda b, i: (b, jnp.maximum(i * r8 - 1, 0), c)

    def next_map(c):
        return lambda b, i: (b, jnp.minimum((i + 1) * r8, n8 - 1), c)

    return pl.pallas_call(
        functools.partial(_mlstm_prep_kernel, tr=tr, nct=nct, nt=nt),
        grid=(bsz, nt),
        in_specs=[
            pl.BlockSpec((1, tr, MW), lambda b, i: (b, i, cq)),
            pl.BlockSpec((1, tr, MW), lambda b, i: (b, i, ck)),
            pl.BlockSpec((1, 8, MW), prev_map(cq)),
            pl.BlockSpec((1, 8, MW), prev_map(ck)),
            pl.BlockSpec((1, 8, MW), next_map(cq)),
            pl.BlockSpec((1, 8, MW), next_map(ck)),
            pl.BlockSpec((8, 2 * MW), lambda b, i: (0, 0)),
            pl.BlockSpec((1, 2 * MW), lambda b, i: (0, 0)),
        ],
        out_specs=[
            pl.BlockSpec((1, tr, MW), lambda b, i: (b, i, 0)),
            pl.BlockSpec((1, tr, MW), lambda b, i: (b, i, 0)),
        ],
        out_shape=[
            jax.ShapeDtypeStruct((bsz, s, MW), BF16),
            jax.ShapeDtypeStruct((bsz, s, MW), BF16),
        ],
        compiler_params=_cparams("parallel", "parallel"),
        name="mlstm_prep",
    )(proj, proj, proj, proj, proj, proj, conv_w, conv_b)


def _mlstm_chunk_index(dirn, step, nctx_ch, nch):
    c_b = jnp.where(step < nctx_ch, nctx_ch - 1 - step, nch - 1 - (step - nctx_ch))
    return jnp.where(dirn == 0, step, c_b)


def _mlstm_kernel(q_ref, k_ref, v_ref, gc_ref, gr_ref, o_ref, ct_sc, n_sc, m_sc):
    dirn = pl.program_id(0)
    step = pl.program_id(2)

    @pl.when(step == 0)
    def _():
        ct_sc[...] = jnp.zeros_like(ct_sc)
        n_sc[...] = jnp.zeros_like(n_sc)
        m_sc[...] = jnp.zeros_like(m_sc)

    row = lax.broadcasted_iota(jnp.int32, (MCH, MCH), 0)
    col = lax.broadcasted_iota(jnp.int32, (MCH, MCH), 1)
    sgn = 1 - 2 * dirn
    allowed = (col - row) * sgn <= 0
    tri = jnp.where(allowed, 1.0, 0.0).astype(BF16)

    gc = gc_ref[0, 0]
    gr = gr_ref[0, 0]
    lf_c = _log_sigmoid(gc)
    lf_r = _log_sigmoid(gr)
    c3 = _split3(lf_c)
    r3 = _split3(lf_r)
    b_c = _dot(tri, c3[0]) + _dot(tri, c3[1]) + _dot(tri, c3[2])
    b_r = _dot_nt(r3[0], tri) + _dot_nt(r3[1], tri) + _dot_nt(r3[2], tri)
    bl_r = jnp.where(dirn == 0, b_r[:, MCH - 1:MCH], b_r[:, 0:1])

    q = q_ref[0]
    k = k_ref[0]
    v = v_ref[0].astype(BF16)
    for h in range(MH):
        bc = b_c[:, MH + h:MH + h + 1]
        br = b_r[MH + h:MH + h + 1, :]
        li_r = gr[h:h + 1, :]
        li_c = gc[:, h:h + 1]
        m_st = m_sc[h][:1, :1]
        qh = q[:, h * MD:(h + 1) * MD]
        kh = k[:, h * MD:(h + 1) * MD]
        vh = v[:, h * MD:(h + 1) * MD]

        log_d = jnp.where(allowed, bc - br + li_r, -jnp.inf)
        inter = bc + m_st
        m_t = jnp.maximum(inter, jnp.max(log_d, axis=-1, keepdims=True))
        s = _dot_nt(qh, kh) * jnp.exp(log_d - m_t)
        w_inter = jnp.exp(inter - m_t)
        num = _dot(s.astype(BF16), vh) + w_inter * _dot(qh, ct_sc[h].astype(BF16))
        den = (jnp.sum(s, axis=-1, keepdims=True)
               + w_inter * jnp.sum(qh.astype(F32) * n_sc[h][:1, :], axis=-1, keepdims=True))
        o_ref[0, 0, :, h * MD:(h + 1) * MD] = num / jnp.maximum(jnp.abs(den), jnp.exp(-m_t))

        b_last = bl_r[MH + h:MH + h + 1, :]
        log_w = b_last - bc + li_c
        m_new = jnp.maximum(b_last + m_st, jnp.max(log_w, axis=0, keepdims=True))
        wgt = jnp.exp(log_w - m_new)
        decay = jnp.exp(b_last + m_st - m_new)
        kw = kh.astype(F32) * wgt
        ct_sc[h] = decay * ct_sc[h] + _dot_tn(kw.astype(BF16), vh)
        n_sc[h] = jnp.broadcast_to(decay * n_sc[h][:1, :] + jnp.sum(kw, axis=0, keepdims=True), (8, MD))
        m_sc[h] = jnp.broadcast_to(m_new, (8, 128))


def _mlstm_call(qc, kc, proj, g_col, g_row, nctx):
    bsz, s, _ = qc.shape
    nch = s // MCH
    nctx_ch = nctx // MCH
    cv = C_MV // MW

    def cidx(d, st):
        return _mlstm_chunk_index(d, st, nctx_ch, nch)

    return pl.pallas_call(
        _mlstm_kernel,
        grid=(2, bsz, nch),
        in_specs=[
            pl.BlockSpec((1, MCH, MW), lambda d, b, st: (b, cidx(d, st), 0)),
            pl.BlockSpec((1, MCH, MW), lambda d, b, st: (b, cidx(d, st), 0)),
            pl.BlockSpec((1, MCH, MW), lambda d, b, st: (b, cidx(d, st), cv)),
            pl.BlockSpec((1, 1, MCH, 128), lambda d, b, st: (b, d, cidx(d, st), 0)),
            pl.BlockSpec((1, 1, 8, MCH), lambda d, b, st: (b, d, 0, cidx(d, st))),
        ],
        out_specs=pl.BlockSpec((1, 1, MCH, MW), lambda d, b, st: (d, b, cidx(d, st), 0)),
        out_shape=jax.ShapeDtypeStruct((2, bsz, s, MW), F32),
        scratch_shapes=[
            pltpu.VMEM((MH, MD, MD), F32),
            pltpu.VMEM((MH, 8, MD), F32),
            pltpu.VMEM((MH, 8, 128), F32),
        ],
        compiler_params=_cparams("parallel", "parallel", "arbitrary"),
        name="mlstm_scan",
    )(qc, kc, proj, g_col, g_row)


def _mlstm_out_kernel(h_ref, o_ref, z_ref, nw_ref, y_ref):
    hs = h_ref[0, 0] + h_ref[1, 0]
    og = o_ref[0]
    zg = z_ref[0]
    nw = nw_ref[...]
    for h in range(MH):
        sl = slice(h * MD, (h + 1) * MD)
        hn = _rms(hs[:, sl], nw[:, sl])
        y_ref[0, :, sl] = (jax.nn.sigmoid(og[:, sl]) * hn * _silu(zg[:, sl])).astype(BF16)


def _mlstm_out_call(hdir, proj, norm_w):
    _, bsz, s, _ = hdir.shape
    tr = ROW_TILE
    return pl.pallas_call(
        _mlstm_out_kernel,
        grid=(bsz, s // tr),
        in_specs=[
            pl.BlockSpec((2, 1, tr, MW), lambda b, i: (0, b, i, 0)),
            pl.BlockSpec((1, tr, MW), lambda b, i: (b, i, C_MO // MW)),
            pl.BlockSpec((1, tr, MW), lambda b, i: (b, i, C_MZ // MW)),
            pl.BlockSpec((1, MW), lambda b, i: (0, 0)),
        ],
        out_specs=pl.BlockSpec((1, tr, MW), lambda b, i: (b, i, 0)),
        out_shape=jax.ShapeDtypeStruct((bsz, s, MW), BF16),
        compiler_params=_cparams("parallel", "parallel"),
        name="mlstm_out",
    )(hdir, proj, proj, norm_w)


def _mix(ya_ref, yb_ref, yc_ref, wa_ref, wb_ref, wc_ref):
    return _dot(ya_ref[0], wa_ref[...]) + _dot(yb_ref[0], wb_ref[...]) + _dot(yc_ref[0], wc_ref[...])


def _outproj_mid_kernel(ya_ref, yb_ref, yc_ref, wa_ref, wb_ref, wc_ref, x_ref, ctx_ref, g_ref,
                        nw_ref, sh_ref, sc_ref, h_ref, xn_ref, *, nct):
    i = pl.program_id(1)
    upd = g_ref[0] * _mix(ya_ref, yb_ref, yc_ref, wa_ref, wb_ref, wc_ref)

    def emit(res):
        h = res + upd
        h_ref[0] = h
        xn_ref[0] = (_rms(h, nw_ref[0]) * (1.0 + sc_ref[0]) + sh_ref[0]).astype(BF16)

    @pl.when(i < nct)
    def _():
        emit(ctx_ref[0])

    @pl.when(i >= nct)
    def _():
        emit(x_ref[0])


def _outproj_mid_call(ya, yb, yc, w_out, x, ctx, mod, nw, layer):
    bsz, s, _ = ya.shape
    d = x.shape[-1]
    nctx = ctx.shape[1]
    tr = ROW_TILE
    nct = nctx // tr
    wa, wb, wc = w_out[:SSM_W], w_out[SSM_W:SSM_W + AW], w_out[SSM_W + AW:]
    nxt = layer + 1
    return pl.pallas_call(
        functools.partial(_outproj_mid_kernel, nct=nct),
        grid=(bsz, s // tr),
        in_specs=[
            pl.BlockSpec((1, tr, SSM_W), lambda b, i: (b, i, 0)),
            pl.BlockSpec((1, tr, AW), lambda b, i: (b, i, 0)),
            pl.BlockSpec((1, tr, MW), lambda b, i: (b, i, 0)),
            pl.BlockSpec((SSM_W, d), lambda b, i: (0, 0)),
            pl.BlockSpec((AW, d), lambda b, i: (0, 0)),
            pl.BlockSpec((MW, d), lambda b, i: (0, 0)),
            pl.BlockSpec((1, tr, d), lambda b, i: (b, jnp.maximum(i - nct, 0), 0)),
            pl.BlockSpec((1, tr, d), lambda b, i: (b, jnp.minimum(i, nct - 1), 0)),
            _mod_spec(d, layer, 2, nct, bsz),
            pl.BlockSpec((1, 1, d), lambda b, i: (nxt, 0, 0)),
            _mod_spec(d, nxt, 0, nct, bsz),
            _mod_spec(d, nxt, 1, nct, bsz),
        ],
        out_specs=[
            pl.BlockSpec((1, tr, d), lambda b, i: (b, i, 0)),
            pl.BlockSpec((1, tr, d), lambda b, i: (b, i, 0)),
        ],
        out_shape=[
            jax.ShapeDtypeStruct((bsz, s, d), F32),
            jax.ShapeDtypeStruct((bsz, s, d), BF16),
        ],
        compiler_params=_cparams("parallel", "parallel"),
        name="out_proj_mid",
    )(ya, yb, yc, wa, wb, wc, x, ctx, mod, nw, mod, mod)


def _outproj_last_kernel(ya_ref, yb_ref, yc_ref, wa_ref, wb_ref, wc_ref, h_ref, g_ref, nw_ref, o_ref):
    h = h_ref[0] + g_ref[0] * _mix(ya_ref, yb_ref, yc_ref, wa_ref, wb_ref, wc_ref)
    o_ref[0] = _rms(h, nw_ref[...])


def _outproj_last_call(ya, yb, yc, w_out, h_prev, mod, final_w, layer, nctx):
    bsz, s, d = h_prev.shape
    tr = ROW_TILE
    nct = nctx // tr
    n_lat = s - nctx
    wa, wb, wc = w_out[:SSM_W], w_out[SSM_W:SSM_W + AW], w_out[SSM_W + AW:]
    return pl.pallas_call(
        _outproj_last_kernel,
        grid=(bsz, n_lat // tr),
        in_specs=[
            pl.BlockSpec((1, tr, SSM_W), lambda b, i: (b, i + nct, 0)),
            pl.BlockSpec((1, tr, AW), lambda b, i: (b, i + nct, 0)),
            pl.BlockSpec((1, tr, MW), lambda b, i: (b, i + nct, 0)),
            pl.BlockSpec((SSM_W, d), lambda b, i: (0, 0)),
            pl.BlockSpec((AW, d), lambda b, i: (0, 0)),
            pl.BlockSpec((MW, d), lambda b, i: (0, 0)),
            pl.BlockSpec((1, tr, d), lambda b, i: (b, i + nct, 0)),
            pl.BlockSpec((1, 1, d), lambda b, i: (layer * 8 + b, 0, 2)),
            pl.BlockSpec((1, d), lambda b, i: (0, 0)),
        ],
        out_specs=pl.BlockSpec((1, tr, d), lambda b, i: (b, i, 0)),
        out_shape=jax.ShapeDtypeStruct((bsz, n_lat, d), F32),
        compiler_params=_cparams("parallel", "parallel"),
        name="out_proj_last",
    )(ya, yb, yc, wa, wb, wc, h_prev, mod, final_w)


def _reorder_w_in(w):
    g0 = C_MO + MW
    return jnp.concatenate([w[:, :g0], w[:, g0 + 4 * MH:]], axis=1), w[:, g0:g0 + 4 * MH]


def kernel(x, c, ctx, c_ctx, norm_w, ada_w, ada_b, w_in, mlstm_gate_b, ssm_a_re, ssm_a_im, ssm_log_dt,
           ssm_b_re, ssm_b_im, ssm_c_re, ssm_c_im, ssm_d, ssm_w_glu, attn_q_norm, attn_k_norm,
           mlstm_conv_w, mlstm_conv_b, mlstm_norm_w, w_out, final_norm_w):
    bsz, n_lat, d = x.shape
    nctx = ctx.shape[1]
    s = nctx + n_lat
    depth = norm_w.shape[0]
    assert bsz < 8 and nctx % ROW_TILE == 0 and n_lat % ROW_TILE == 0 and depth == 2

    c_rows = jnp.zeros((8, d), F32).at[:bsz].set(c).at[bsz].set(c_ctx)
    mod = _ada_call(c_rows, ada_w, ada_b).reshape(depth * 8, 1, 3 * d)
    cos2, sin2 = _rope_tables(n_lat, nctx)

    norm_w3 = norm_w.astype(F32).reshape(depth, 1, d)
    xn = _prenorm_call(x, ctx, norm_w3, mod, 0)
    h_prev = None
    out = None
    for layer in range(depth):
        w_main, w_g = _reorder_w_in(w_in[layer])
        proj = _inproj_call(xn.reshape(bsz * s, d), w_main.astype(BF16)).reshape(bsz, s, PROJ_W)

        wg = w_g.reshape(d, 2, 2 * MH)
        w_gc = jnp.zeros((2, d, 128), F32).at[:, :, :2 * MH].set(wg.transpose(1, 0, 2)).astype(BF16)
        w_gr = wg.transpose(1, 2, 0).astype(BF16)
        gb = mlstm_gate_b[layer].astype(F32).reshape(2, 2 * MH)
        b_gc = jnp.zeros((2, 1, 128), F32).at[:, 0, :2 * MH].set(gb)
        b_gr = jnp.broadcast_to(gb[:, :, None], (2, 2 * MH, 128))
        g_col, g_row = _gates_call(xn, w_gc, w_gr, b_gc, b_gr)

        ya = _s5_branch(proj, dict(a_re=ssm_a_re[layer], a_im=ssm_a_im[layer], log_dt=ssm_log_dt[layer],
                                   b_re=ssm_b_re[layer], b_im=ssm_b_im[layer], c_re=ssm_c_re[layer],
                                   c_im=ssm_c_im[layer], d=ssm_d[layer], w_glu=ssm_w_glu[layer]), nctx)

        qs, ks, vs = _attn_prep_call(proj, cos2, sin2, attn_q_norm[layer].reshape(1, HD).astype(F32),
                                     attn_k_norm[layer].reshape(1, HD).astype(F32))
        yb = _attn_call(qs, ks, vs, proj, nctx)

        conv_w = jnp.zeros((8, 2 * MW), F32).at[:3].set(mlstm_conv_w[layer].astype(F32))
        qc, kc = _mlstm_prep_call(proj, conv_w, mlstm_conv_b[layer].astype(F32).reshape(1, 2 * MW), nctx)
        hdir = _mlstm_call(qc, kc, proj, g_col, g_row, nctx)
        yc = _mlstm_out_call(hdir, proj, mlstm_norm_w[layer].astype(F32).reshape(1, MW))

        w_o = w_out[layer].astype(BF16)
        if layer < depth - 1:
            h_prev, xn = _outproj_mid_call(ya, yb, yc, w_o, x, ctx, mod, norm_w3, layer)
        else:
            out = _outproj_last_call(ya, yb, yc, w_o, h_prev, mod, final_norm_w.reshape(1, d), layer, nctx)
    return out
```

```python
import functools
import math

import jax
import jax.numpy as jnp
from jax import lax
from jax.experimental import pallas as pl
from jax.experimental.pallas import tpu as pltpu

F32 = jnp.float32
BF16 = jnp.bfloat16
EPS = 1e-6

SSM_W = 512
SSM_P = 16
SSM_G = 32
SSM_N = 64
S5_T = 16
S5_SLAB_G = 8
S5_SLABS = SSM_G // S5_SLAB_G
S5_ST = S5_SLAB_G * SSM_N
AH = 8
AKV = 2
AGRP = AH // AKV
HD = 128
AW = AH * HD
KVW = AKV * HD
GRID_W = 64
ROPE_THETA = 10000.0
MH = 4
MD = 128
MW = MH * MD
MCH = 128
C_U, C_Z, C_AQ, C_AK, C_AV, C_AZ, C_MQ, C_MK, C_MV, C_MO, C_MZ = (
    0, 512, 1024, 2048, 2304, 2560, 3584, 4096, 4608, 5120, 5632)
PROJ_W = 6144
MIX_W = SSM_W + AW + MW

ROW_TILE = 256
VMEM_LIMIT = 48 * 1024 * 1024


def _cparams(*sem):
    return pltpu.CompilerParams(dimension_semantics=sem, vmem_limit_bytes=VMEM_LIMIT)


def _dot(a, b):
    return jnp.dot(a, b, preferred_element_type=F32)


def _dot_nt(a, b):
    return lax.dot_general(a, b, (((1,), (1,)), ((), ())), preferred_element_type=F32)


def _dot_tn(a, b):
    return lax.dot_general(a, b, (((0,), (0,)), ((), ())), preferred_element_type=F32)


def _split3(x):
    hi = x.astype(BF16)
    r1 = x - hi.astype(F32)
    mid = r1.astype(BF16)
    lo = (r1 - mid.astype(F32)).astype(BF16)
    return hi, mid, lo


def _silu(x):
    return x * jax.nn.sigmoid(x)


def _log_sigmoid(x):
    return jnp.minimum(x, 0.0) - jnp.log1p(jnp.exp(-jnp.abs(x)))


def _rms(x, w):
    return x * lax.rsqrt(jnp.mean(x * x, axis=-1, keepdims=True) + EPS) * w


def _pick_tile(n, cap, mult):
    best = mult
    for t in range(mult, min(n, cap) + 1, mult):
        if n % t == 0:
            best = t
    return best


def _ada_kernel(c_ref, w_ref, b_ref, o_ref):
    a = _silu(c_ref[...])
    w = w_ref[0]
    a_hi = a.astype(BF16)
    a_lo = (a - a_hi.astype(F32)).astype(BF16)
    w_hi = w.astype(BF16)
    w_lo = (w - w_hi.astype(F32)).astype(BF16)
    o_ref[0] = _dot(a_hi, w_hi) + _dot(a_hi, w_lo) + _dot(a_lo, w_hi) + b_ref[0]


def _ada_call(c_rows, ada_w, ada_b):
    depth, d, n3 = ada_w.shape
    tn = 512
    return pl.pallas_call(
        _ada_kernel,
        grid=(depth, n3 // tn),
        in_specs=[
            pl.BlockSpec((8, d), lambda l, j: (0, 0)),
            pl.BlockSpec((1, d, tn), lambda l, j: (l, 0, j)),
            pl.BlockSpec((1, 1, tn), lambda l, j: (l, 0, j)),
        ],
        out_specs=pl.BlockSpec((1, 8, tn), lambda l, j: (l, 0, j)),
        out_shape=jax.ShapeDtypeStruct((depth, 8, n3), F32),
        compiler_params=_cparams("parallel", "parallel"),
        name="ada_mod",
    )(c_rows, ada_w, ada_b.reshape(depth, 1, n3))


def _prenorm_kernel(x_ref, ctx_ref, nw_ref, sh_ref, sc_ref, o_ref, *, nct):
    i = pl.program_id(1)

    def emit(h):
        y = _rms(h, nw_ref[0])
        o_ref[0] = (y * (1.0 + sc_ref[0]) + sh_ref[0]).astype(BF16)

    @pl.when(i < nct)
    def _():
        emit(ctx_ref[0])

    @pl.when(i >= nct)
    def _():
        emit(x_ref[0])


def _mod_spec(d, layer, part, nct, nb):
    def idx(b, i):
        return (layer * 8 + jnp.where(i < nct, nb, b), 0, part)
    return pl.BlockSpec((1, 1, d), idx)


def _prenorm_call(x, ctx, nw, mod, layer):
    bsz, n, d = x.shape
    nctx = ctx.shape[1]
    tr = ROW_TILE
    nct = nctx // tr
    s = nctx + n
    return pl.pallas_call(
        functools.partial(_prenorm_kernel, nct=nct),
        grid=(bsz, s // tr),
        in_specs=[
            pl.BlockSpec((1, tr, d), lambda b, i: (b, jnp.maximum(i - nct, 0), 0)),
            pl.BlockSpec((1, tr, d), lambda b, i: (b, jnp.minimum(i, nct - 1), 0)),
            pl.BlockSpec((1, 1, d), lambda b, i: (layer, 0, 0)),
            _mod_spec(d, layer, 0, nct, bsz),
            _mod_spec(d, layer, 1, nct, bsz),
        ],
        out_specs=pl.BlockSpec((1, tr, d), lambda b, i: (b, i, 0)),
        out_shape=jax.ShapeDtypeStruct((bsz, s, d), BF16),
        compiler_params=_cparams("parallel", "parallel"),
        name="prenorm",
    )(x, ctx, nw, mod, mod)


def _matmul_kernel(x_ref, w_ref, o_ref):
    o_ref[...] = _dot(x_ref[...], w_ref[...])


def _inproj_call(xn2d, w):
    m, d = xn2d.shape
    n = w.shape[1]
    tm = _pick_tile(m, 1056, 16)
    tn = 1024
    return pl.pallas_call(
        _matmul_kernel,
        grid=(n // tn, m // tm),
        in_specs=[
            pl.BlockSpec((tm, d), lambda j, i: (i, 0)),
            pl.BlockSpec((d, tn), lambda j, i: (0, j)),
        ],
        out_specs=pl.BlockSpec((tm, tn), lambda j, i: (i, j)),
        out_shape=jax.ShapeDtypeStruct((m, n), F32),
        compiler_params=_cparams("parallel", "parallel"),
        name="in_proj",
    )(xn2d, w)


def _gates_kernel(x_ref, wc_ref, wr_ref, bc_ref, br_ref, gc_ref, gr_ref):
    x = x_ref[0]
    for dirn in range(2):
        gc_ref[0, dirn] = _dot(x, wc_ref[dirn]) + bc_ref[dirn]
        gr_ref[0, dirn] = _dot_nt(wr_ref[dirn], x) + br_ref[dirn][:, :1]


def _gates_call(xn, w_gc, w_gr, b_gc, b_gr):
    bsz, s, d = xn.shape
    tr = ROW_TILE
    return pl.pallas_call(
        _gates_kernel,
        grid=(bsz, s // tr),
        in_specs=[
            pl.BlockSpec((1, tr, d), lambda b, i: (b, i, 0)),
            pl.BlockSpec((2, d, 128), lambda b, i: (0, 0, 0)),
            pl.BlockSpec((2, 8, d), lambda b, i: (0, 0, 0)),
            pl.BlockSpec((2, 1, 128), lambda b, i: (0, 0, 0)),
            pl.BlockSpec((2, 8, 128), lambda b, i: (0, 0, 0)),
        ],
        out_specs=[
            pl.BlockSpec((1, 2, tr, 128), lambda b, i: (b, 0, i, 0)),
            pl.BlockSpec((1, 2, 8, tr), lambda b, i: (b, 0, 0, i)),
        ],
        out_shape=[
            jax.ShapeDtypeStruct((bsz, 2, s, 128), F32),
            jax.ShapeDtypeStruct((bsz, 2, 8, s), F32),
        ],
        compiler_params=_cparams("parallel", "parallel"),
        name="mlstm_gates",
    )(xn, w_gc, w_gr, b_gc, b_gr)


def _s5_weights(a_re, a_im, log_dt, b_re, b_im, c_re, c_im):
    t_ = S5_T
    a_re = a_re.astype(F32)
    a_im = a_im.astype(F32)
    dt = jnp.exp(log_dt.astype(F32))[..., None]
    mag = jnp.exp(a_re * dt)
    lam_re = mag * jnp.cos(a_im * dt)
    lam_im = mag * jnp.sin(a_im * dt)
    inv_abs2 = 1.0 / (a_re * a_re + a_im * a_im)
    num_re, num_im = lam_re - 1.0, lam_im
    f_re = (num_re * a_re + num_im * a_im) * inv_abs2
    f_im = (num_im * a_re - num_re * a_im) * inv_abs2
    b_re = b_re.astype(F32)[None]
    b_im = b_im.astype(F32)[None]
    bb_re = f_re[:, :, None, :] * b_re - f_im[:, :, None, :] * b_im
    bb_im = f_re[:, :, None, :] * b_im + f_im[:, :, None, :] * b_re
    c_re = c_re.astype(F32)
    c_im = c_im.astype(F32)

    pr = [jnp.ones_like(lam_re)]
    pi = [jnp.zeros_like(lam_im)]
    for _ in range(t_):
        pr.append(pr[-1] * lam_re - pi[-1] * lam_im)
        pi.append(pr[-2] * lam_im + pi[-1] * lam_re)
    pw_re = jnp.stack(pr)
    pw_im = jnp.stack(pi)

    tt = jnp.arange(t_)

    def per_dir(arr_f, arr_b):
        return jnp.stack([arr_f, arr_b])

    e_in = per_dir(t_ - 1 - tt, tt)
    dsel = jnp.arange(2)[:, None]
    pin_re = pw_re[e_in, dsel]
    pin_im = pw_im[e_in, dsel]
    wr = bb_re[:, None] * pin_re[:, :, :, None, :] - bb_im[:, None] * pin_im[:, :, :, None, :]
    wi = bb_re[:, None] * pin_im[:, :, :, None, :] + bb_im[:, None] * pin_re[:, :, :, None, :]
    win = jnp.concatenate([wr, wi], axis=-1)
    win = win.reshape(2, t_, S5_SLABS, S5_SLAB_G, SSM_P, 2 * SSM_N).transpose(0, 2, 1, 3, 4, 5)
    win = win.reshape(2, S5_SLABS, t_ * 128, 2 * SSM_N).astype(BF16)

    e_out = per_dir(tt + 1, t_ - tt)
    pout_re = pw_re[e_out, dsel]
    pout_im = pw_im[e_out, dsel]
    clr = c_re[:, None] * pout_re[:, :, :, None, :] - c_im[:, None] * pout_im[:, :, :, None, :]
    cli = c_re[:, None] * pout_im[:, :, :, None, :] + c_im[:, None] * pout_re[:, :, :, None, :]
    wout = jnp.stack([clr, -cli], axis=-2)
    wout = wout.reshape(2, t_, S5_SLABS, S5_SLAB_G, SSM_P, 2, SSM_N).transpose(0, 2, 5, 6, 1, 3, 4)
    wout = wout.reshape(2, S5_SLABS, 2 * SSM_N, t_ * 128).astype(BF16)

    kr = c_re[:, None] * pw_re[:t_].transpose(1, 0, 2, 3)[:, :, :, None, :] \
        - c_im[:, None] * pw_im[:t_].transpose(1, 0, 2, 3)[:, :, :, None, :]
    ki = c_re[:, None] * pw_im[:t_].transpose(1, 0, 2, 3)[:, :, :, None, :] \
        + c_im[:, None] * pw_re[:t_].transpose(1, 0, 2, 3)[:, :, :, None, :]
    kk = (jnp.einsum('dtgpn,dgqn->dtgpq', kr, bb_re, precision=lax.Precision.HIGHEST)
          - jnp.einsum('dtgpn,dgqn->dtgpq', ki, bb_im, precision=lax.Precision.HIGHEST))
    lag = tt[None, :] - tt[:, None]
    kf = kk[0][jnp.clip(lag, 0, t_ - 1)] * (lag >= 0)[:, :, None, None, None]
    kb = kk[1][jnp.clip(-lag, 0, t_ - 1)] * (lag <= 0)[:, :, None, None, None]
    mf = (kf + kb).reshape(t_, t_, S5_SLABS, S5_SLAB_G, SSM_P, SSM_P).transpose(2, 0, 3, 5, 1, 4)
    mtoe = mf.reshape(S5_SLABS, t_ * 128, t_ * SSM_P).astype(BF16)

    def lam_row(v):
        return v.reshape(2, S5_SLABS, 1, S5_ST)

    lam_t = jnp.concatenate([lam_row(pw_re[t_]), lam_row(pw_im[t_])], axis=-1)
    return win, wout, mtoe, lam_t


def _iota(shape, dim):
    return lax.broadcasted_iota(jnp.int32, shape, dim)


def _s5_core_kernel(u_ref, win_ref, wout_ref, mc_ref, lam_ref, y_ref, lhs_sc, w_sc, s_sc, acc_sc,
                    *, nch, nctx_ch):
    t_ = S5_T
    n2 = 2 * SSM_N
    st2 = 2 * S5_ST
    lg_p = SSM_P.bit_length() - 1
    lg_n = SSM_N.bit_length() - 1
    lg_st = S5_ST.bit_length() - 1

    for t in range(t_):
        lhs_sc[:, t * 128:(t + 1) * 128] = u_ref[0, pl.ds(t, nch, stride=t_), :].astype(BF16)

    k1, c1 = _iota((n2, st2), 0), _iota((n2, st2), 1)
    e_in = jnp.where(((k1 >> lg_n) == (c1 >> lg_st)) & ((k1 & (SSM_N - 1)) == (c1 & (SSM_N - 1))),
                     1.0, 0.0).astype(BF16)
    r1, c1b = _iota((128, st2), 0), _iota((128, st2), 1)
    m_in = (r1 >> lg_p) == ((c1b & (S5_ST - 1)) >> lg_n)

    for d in range(2):
        for t in range(t_):
            rows = slice(t * 128, (t + 1) * 128)
            w_sc[rows, :st2] = jnp.where(m_in, _dot(win_ref[d, 0, rows, :], e_in), 0.0).astype(BF16)
        s_sc[d] = _dot(lhs_sc[...], w_sc[:, :st2])

    lam_f = lam_ref[0, 0]
    lam_b = lam_ref[1, 0]
    lrf, lif = lam_f[:, :S5_ST], lam_f[:, S5_ST:]
    lrb, lib = lam_b[:, :S5_ST], lam_b[:, S5_ST:]

    def step(k, carry):
        hrf, hif, hrb, hib = carry
        cb = jnp.where(k < nctx_ch, nctx_ch - 1 - k, nch - 1 - (k - nctx_ch))
        sf = s_sc[0, pl.ds(k, 1), :]
        sb = s_sc[1, pl.ds(cb, 1), :]
        s_sc[0, pl.ds(k, 1), :] = jnp.concatenate([hrf, hif], axis=-1)
        s_sc[1, pl.ds(cb, 1), :] = jnp.concatenate([hrb, hib], axis=-1)
        return (lrf * hrf - lif * hif + sf[:, :S5_ST], lrf * hif + lif * hrf + sf[:, S5_ST:],
                lrb * hrb - lib * hib + sb[:, :S5_ST], lrb * hib + lib * hrb + sb[:, S5_ST:])

    zero = jnp.zeros((1, S5_ST), F32)
    lax.fori_loop(0, nch, step, (zero, zero, zero, zero))

    k2, c2 = _iota((t_ * SSM_P, t_ * 128), 0), _iota((t_ * SSM_P, t_ * 128), 1)
    e_m = jnp.where(((k2 >> lg_p) == (c2 >> 7)) & ((k2 & (SSM_P - 1)) == (c2 & (SSM_P - 1))),
                    1.0, 0.0).astype(BF16)
    r2, c2b = _iota((128, t_ * 128), 0), _iota((128, t_ * 128), 1)
    m_m = (r2 >> lg_p) == ((c2b & 127) >> lg_p)
    for t in range(t_):
        rows = slice(t * 128, (t + 1) * 128)
        w_sc[rows, :] = jnp.where(m_m, _dot(mc_ref[0, rows, :], e_m), 0.0).astype(BF16)
    acc_sc[...] = _dot(lhs_sc[...], w_sc[...])

    r3, k3 = _iota((st2, n2), 0), _iota((st2, n2), 1)
    e_out = jnp.where(((r3 >> lg_st) == (k3 >> lg_n)) & ((r3 & (SSM_N - 1)) == (k3 & (SSM_N - 1))),
                      1.0, 0.0).astype(BF16)
    r4, c4 = _iota((st2, 128), 0), _iota((st2, 128), 1)
    m_out = ((r4 & (S5_ST - 1)) >> lg_n) == (c4 >> lg_p)
    for d in range(2):
        for t in range(t_):
            cols = slice(t * 128, (t + 1) * 128)
            w_sc[:st2, cols] = jnp.where(m_out, _dot(e_out, wout_ref[d, 0, :, cols]), 0.0).astype(BF16)
        acc_sc[...] += _dot(s_sc[d].astype(BF16), w_sc[:st2, :])

    for t in range(t_):
        y_ref[0, pl.ds(t, nch, stride=t_), :] = acc_sc[:, t * 128:(t + 1) * 128]


def _s5_core_call(proj, win, wout, mtoe, lam_t, nctx):
    bsz, s, _ = proj.shape
    t_ = S5_T
    nch = s // t_
    kdim = t_ * 128
    n2 = 2 * SSM_N
    st2 = 2 * S5_ST
    cu = C_U // 128
    return pl.pallas_call(
        functools.partial(_s5_core_kernel, nch=nch, nctx_ch=nctx // t_),
        grid=(S5_SLABS, bsz),
        in_specs=[
            pl.BlockSpec((1, s, 128), lambda sl, b: (b, 0, cu + sl)),
            pl.BlockSpec((2, 1, kdim, n2), lambda sl, b: (0, sl, 0, 0)),
            pl.BlockSpec((2, 1, n2, kdim), lambda sl, b: (0, sl, 0, 0)),
            pl.BlockSpec((1, kdim, t_ * SSM_P), lambda sl, b: (sl, 0, 0)),
            pl.BlockSpec((2, 1, 1, st2), lambda sl, b: (0, sl, 0, 0)),
        ],
        out_specs=pl.BlockSpec((1, s, 128), lambda sl, b: (b, 0, sl)),
        out_shape=jax.ShapeDtypeStruct((bsz, s, SSM_W), F32),
        scratch_shapes=[
            pltpu.VMEM((nch, kdim), BF16),
            pltpu.VMEM((kdim, kdim), BF16),
            pltpu.VMEM((2, nch, st2), F32),
            pltpu.VMEM((nch, kdim), F32),
        ],
        compiler_params=pltpu.CompilerParams(dimension_semantics=("parallel", "parallel"),
                                             vmem_limit_bytes=56 * 1024 * 1024),
        name="s5_core",
    )(proj, win, wout, mtoe, lam_t)


def _s5_glu_kernel(y_ref, u_ref, z_ref, d_ref, w_ref, o_ref):
    y = y_ref[0] + d_ref[...] * u_ref[0]
    g = jax.nn.gelu(y, approximate=True)
    gate = jax.nn.sigmoid(_dot(g.astype(BF16), w_ref[...]))
    o_ref[0] = (g * gate * _silu(z_ref[0])).astype(BF16)


def _s5_glu_call(yssm, proj, d_row, w_glu):
    bsz, s, _ = yssm.shape
    tr = ROW_TILE
    return pl.pallas_call(
        _s5_glu_kernel,
        grid=(bsz, s // tr),
        in_specs=[
            pl.BlockSpec((1, tr, SSM_W), lambda b, i: (b, i, 0)),
            pl.BlockSpec((1, tr, SSM_W), lambda b, i: (b, i, C_U // SSM_W)),
            pl.BlockSpec((1, tr, SSM_W), lambda b, i: (b, i, C_Z // SSM_W)),
            pl.BlockSpec((1, SSM_W), lambda b, i: (0, 0)),
            pl.BlockSpec((SSM_W, SSM_W), lambda b, i: (0, 0)),
        ],
        out_specs=pl.BlockSpec((1, tr, SSM_W), lambda b, i: (b, i, 0)),
        out_shape=jax.ShapeDtypeStruct((bsz, s, SSM_W), BF16),
        compiler_params=_cparams("parallel", "parallel"),
        name="s5_glu",
    )(yssm, proj, proj, d_row, w_glu)


def _s5_branch(proj, p, nctx):
    win, wout, mtoe, lam_t = _s5_weights(p['a_re'], p['a_im'], p['log_dt'], p['b_re'], p['b_im'],
                                         p['c_re'], p['c_im'])
    yssm = _s5_core_call(proj, win, wout, mtoe, lam_t, nctx)
    return _s5_glu_call(yssm, proj, p['d'].astype(F32).reshape(1, SSM_W), p['w_glu'].astype(BF16))


def _rope_tables(n_lat, nctx):
    rows = n_lat // GRID_W
    row = jnp.broadcast_to(jnp.arange(rows, dtype=F32)[:, None], (rows, GRID_W)).reshape(-1)
    col = jnp.broadcast_to(jnp.arange(GRID_W, dtype=F32)[None, :], (rows, GRID_W)).reshape(-1)
    n_freq = HD // 4
    inv = ROPE_THETA ** (-jnp.arange(n_freq, dtype=F32) / n_freq)
    ang = jnp.concatenate([row[:, None] * inv, col[:, None] * inv], axis=-1)
    cs, sn = jnp.cos(ang), jnp.sin(ang)
    cos2 = jnp.concatenate([cs, cs], axis=-1)
    sin2 = jnp.concatenate([-sn, sn], axis=-1)
    cos2 = jnp.concatenate([jnp.ones((nctx, HD), F32), cos2], axis=0)
    sin2 = jnp.concatenate([jnp.zeros((nctx, HD), F32), sin2], axis=0)
    return cos2, sin2


def _attn_prep_kernel(q_ref, k_ref, v_ref, cos_ref, sin_ref, qw_ref, kw_ref, qo_ref, ko_ref, vo_ref):
    cs = cos_ref[...]
    sn = sin_ref[...]

    def norm_rope(x, w):
        y = _rms(x, w)
        return y * cs + pltpu.roll(y, HD // 2, 1) * sn

    q = q_ref[0]
    for h in range(AH):
        r = norm_rope(q[:, h * HD:(h + 1) * HD], qw_ref[...])
        qo_ref[0, :, h * HD:(h + 1) * HD] = (r * (HD ** -0.5 * math.log2(math.e))).astype(BF16)
    k = k_ref[0]
    for h in range(AKV):
        ko_ref[0, :, h * HD:(h + 1) * HD] = norm_rope(k[:, h * HD:(h + 1) * HD], kw_ref[...]).astype(BF16)
    vo_ref[0] = v_ref[0].astype(BF16)


def _attn_prep_call(proj, cos2, sin2, qw, kw):
    bsz, s, _ = proj.shape
    tr = ROW_TILE
    return pl.pallas_call(
        _attn_prep_kernel,
        grid=(bsz, s // tr),
        in_specs=[
            pl.BlockSpec((1, tr, AW), lambda b, i: (b, i, C_AQ // AW)),
            pl.BlockSpec((1, tr, KVW), lambda b, i: (b, i, C_AK // KVW)),
            pl.BlockSpec((1, tr, KVW), lambda b, i: (b, i, C_AV // KVW)),
            pl.BlockSpec((tr, HD), lambda b, i: (i, 0)),
            pl.BlockSpec((tr, HD), lambda b, i: (i, 0)),
            pl.BlockSpec((1, HD), lambda b, i: (0, 0)),
            pl.BlockSpec((1, HD), lambda b, i: (0, 0)),
        ],
        out_specs=[
            pl.BlockSpec((1, tr, AW), lambda b, i: (b, i, 0)),
            pl.BlockSpec((1, tr, KVW), lambda b, i: (b, i, 0)),
            pl.BlockSpec((1, tr, KVW), lambda b, i: (b, i, 0)),
        ],
        out_shape=[
            jax.ShapeDtypeStruct((bsz, s, AW), BF16),
            jax.ShapeDtypeStruct((bsz, s, KVW), BF16),
            jax.ShapeDtypeStruct((bsz, s, KVW), BF16),
        ],
        compiler_params=_cparams("parallel", "parallel"),
        name="attn_prep",
    )(proj, proj, proj, cos2, sin2, qw, kw)


def _attn_kernel(q_ref, k_ref, v_ref, z_ref, o_ref, s_sc, *, tq, nct, nctx, ck, nck):
    qi = pl.program_id(2)
    q = q_ref[0]
    q4 = jnp.concatenate([q[:, h * HD:(h + 1) * HD] for h in range(AGRP)], axis=0)

    def finish(acc, l):
        o = acc / l
        z = z_ref[0]
        for h in range(AGRP):
            o_ref[0, :, h * HD:(h + 1) * HD] = (
                o[h * tq:(h + 1) * tq] * _silu(z[:, h * HD:(h + 1) * HD])).astype(BF16)

    @pl.when(qi < nct)
    def _():
        s = _dot_nt(q4, k_ref[0, :nctx, :])
        m = jnp.max(s, axis=-1, keepdims=True)
        p = jnp.exp2(s - m)
        l = jnp.sum(p, axis=-1, keepdims=True)
        finish(_dot(p.astype(BF16), v_ref[0, :nctx, :]), l)

    @pl.when(qi >= nct)
    def _():
        s_sc[0] = _dot_nt(q4, k_ref[0, 0:ck, :])
        m = l = acc = None
        for j in range(nck):
            if j + 1 < nck:
                s_sc[(j + 1) % 2] = _dot_nt(q4, k_ref[0, (j + 1) * ck:(j + 2) * ck, :])
            s = s_sc[j % 2]
            s_max = jnp.max(s, axis=-1, keepdims=True)
            m_new = s_max if j == 0 else jnp.maximum(m, s_max)
            p = jnp.exp2(s - m_new)
            pv = _dot(p.astype(BF16), v_ref[0, j * ck:(j + 1) * ck, :])
            p_sum = jnp.sum(p, axis=-1, keepdims=True)
            if j == 0:
                l, acc = p_sum, pv
            else:
                alpha = jnp.exp2(m - m_new)
                l = alpha * l + p_sum
                acc = alpha * acc + pv
            m = m_new
        finish(acc, l)


def _attn_call(qs, ks, vs, proj, nctx):
    bsz, s, _ = qs.shape
    tq = ROW_TILE
    nct = nctx // tq
    ck = _pick_tile(s, 1536, 128)
    nck = s // ck
    zw = AGRP * HD
    return pl.pallas_call(
        functools.partial(_attn_kernel, tq=tq, nct=nct, nctx=nctx, ck=ck, nck=nck),
        grid=(bsz, AKV, s // tq),
        in_specs=[
            pl.BlockSpec((1, tq, zw), lambda b, g, i: (b, i, g)),
            pl.BlockSpec((1, s, HD), lambda b, g, i: (b, 0, g)),
            pl.BlockSpec((1, s, HD), lambda b, g, i: (b, 0, g)),
            pl.BlockSpec((1, tq, zw), lambda b, g, i: (b, i, C_AZ // zw + g)),
        ],
        out_specs=pl.BlockSpec((1, tq, zw), lambda b, g, i: (b, i, g)),
        out_shape=jax.ShapeDtypeStruct((bsz, s, AW), BF16),
        scratch_shapes=[pltpu.VMEM((2, AGRP * tq, ck), F32)],
        compiler_params=_cparams("parallel", "parallel", "parallel"),
        name="attention",
    )(qs, ks, vs, proj)


def _mlstm_prep_kernel(q_ref, k_ref, qp_ref, kp_ref, qn_ref, kn_ref, w_ref, b_ref, qo_ref, ko_ref,
                       *, tr, nct, nt):
    i = pl.program_id(1)
    first = jnp.logical_or(i == 0, i == nct)
    last = jnp.logical_or(i == nct - 1, i == nt - 1)
    rid = lax.broadcasted_iota(jnp.int32, (tr, MW), 0)
    w = w_ref[...]
    bias = b_ref[...]

    def conv(x, prev_row, next_row, off):
        prev_row = jnp.where(first, 0.0, prev_row)
        next_row = jnp.where(last, 0.0, next_row)
        xp = jnp.where(rid == 0, prev_row, pltpu.roll(x, 1, 0))
        xn = jnp.where(rid == tr - 1, next_row, pltpu.roll(x, tr - 1, 0))
        y = (w[0:1, off:off + MW] * xp + w[1:2, off:off + MW] * x + w[2:3, off:off + MW] * xn
             + bias[:, off:off + MW])
        return _silu(y)

    qo_ref[0] = conv(q_ref[0], qp_ref[0, 7:8, :], qn_ref[0, 0:1, :], 0).astype(BF16)
    ko_ref[0] = (conv(k_ref[0], kp_ref[0, 7:8, :], kn_ref[0, 0:1, :], MW) * (MD ** -0.5)).astype(BF16)


def _mlstm_prep_call(proj, conv_w, conv_b, nctx):
    bsz, s, _ = proj.shape
    tr = ROW_TILE
    nt = s // tr
    nct = nctx // tr
    r8 = tr // 8
    n8 = s // 8
    cq = C_MQ // MW
    ck = C_MK // MW

    def prev_map(c):
        return lambda b, i: (b, jnp.maximum(i * r8 - 1, 0), c)

    def next_map(c):
        return lambda b, i: (b, jnp.minimum((i + 1) * r8, n8 - 1), c)

    return pl.pallas_call(
        functools.partial(_mlstm_prep_kernel, tr=tr, nct=nct, nt=nt),
        grid=(bsz, nt),
        in_specs=[
            pl.BlockSpec((1, tr, MW), lambda b, i: (b, i, cq)),
            pl.BlockSpec((1, tr, MW), lambda b, i: (b, i, ck)),
            pl.BlockSpec((1, 8, MW), prev_map(cq)),
            pl.BlockSpec((1, 8, MW), prev_map(ck)),
            pl.BlockSpec((1, 8, MW), next_map(cq)),
            pl.BlockSpec((1, 8, MW), next_map(ck)),
            pl.BlockSpec((8, 2 * MW), lambda b, i: (0, 0)),
            pl.BlockSpec((1, 2 * MW), lambda b, i: (0, 0)),
        ],
        out_specs=[
            pl.BlockSpec((1, tr, MW), lambda b, i: (b, i, 0)),
            pl.BlockSpec((1, tr, MW), lambda b, i: (b, i, 0)),
        ],
        out_shape=[
            jax.ShapeDtypeStruct((bsz, s, MW), BF16),
            jax.ShapeDtypeStruct((bsz, s, MW), BF16),
        ],
        compiler_params=_cparams("parallel", "parallel"),
        name="mlstm_prep",
    )(proj, proj, proj, proj, proj, proj, conv_w, conv_b)


def _mlstm_bwd_chunk(step, nctx_ch, nch):
    return jnp.where(step < nctx_ch, nctx_ch - 1 - step, nch - 1 - (step - nctx_ch))


def _mlstm_chunk(dirn, q, k, v, gc, gr, o_ref, b, ct_sc, n_sc, m_sc, base):
    row = _iota((MCH, MCH), 0)
    col = _iota((MCH, MCH), 1)
    allowed = (col <= row) if dirn == 0 else (col >= row)
    tri = jnp.where(allowed, 1.0, 0.0).astype(BF16)
    c3 = _split3(_log_sigmoid(gc))
    r3 = _split3(_log_sigmoid(gr))
    b_c = _dot(tri, c3[0]) + _dot(tri, c3[1]) + _dot(tri, c3[2])
    b_r = _dot_nt(r3[0], tri) + _dot_nt(r3[1], tri) + _dot_nt(r3[2], tri)
    bl_r = b_r[:, MCH - 1:MCH] if dirn == 0 else b_r[:, 0:1]

    for h in range(MH):
        st = base + h
        bc = b_c[:, MH + h:MH + h + 1]
        br = b_r[MH + h:MH + h + 1, :]
        li_r = gr[h:h + 1, :]
        li_c = gc[:, h:h + 1]
        m_st = m_sc[st][:1, :1]
        qh = q[:, h * MD:(h + 1) * MD]
        kh = k[:, h * MD:(h + 1) * MD]
        vh = v[:, h * MD:(h + 1) * MD]

        log_d = jnp.where(allowed, bc - br + li_r, -jnp.inf)
        inter = bc + m_st
        m_t = jnp.maximum(inter, jnp.max(log_d, axis=-1, keepdims=True))
        s = _dot_nt(qh, kh) * jnp.exp(log_d - m_t)
        w_inter = jnp.exp(inter - m_t)
        num = _dot(s.astype(BF16), vh) + w_inter * _dot(qh, ct_sc[st].astype(BF16))
        den = (jnp.sum(s, axis=-1, keepdims=True)
               + w_inter * jnp.sum(qh.astype(F32) * n_sc[st][:1, :], axis=-1, keepdims=True))
        o_ref[b, :, h * MD:(h + 1) * MD] = num / jnp.maximum(jnp.abs(den), jnp.exp(-m_t))

        b_last = bl_r[MH + h:MH + h + 1, :]
        log_w = b_last - bc + li_c
        m_new = jnp.maximum(b_last + m_st, jnp.max(log_w, axis=0, keepdims=True))
        wgt = jnp.exp(log_w - m_new)
        decay = jnp.exp(b_last + m_st - m_new)
        kw = kh.astype(F32) * wgt
        ct_sc[st] = decay * ct_sc[st] + _dot_tn(kw.astype(BF16), vh)
        n_sc[st] = jnp.broadcast_to(decay * n_sc[st][:1, :] + jnp.sum(kw, axis=0, keepdims=True), (8, MD))
        m_sc[st] = jnp.broadcast_to(m_new, (8, 128))


def _mlstm_kernel(qf_ref, kf_ref, vf_ref, gcf_ref, grf_ref, qb_ref, kb_ref, vb_ref, gcb_ref, grb_ref,
                  of_ref, ob_ref, ct_sc, n_sc, m_sc, *, bsz):
    @pl.when(pl.program_id(0) == 0)
    def _():
        ct_sc[...] = jnp.zeros_like(ct_sc)
        n_sc[...] = jnp.zeros_like(n_sc)
        m_sc[...] = jnp.zeros_like(m_sc)

    for b in range(bsz):
        _mlstm_chunk(0, qf_ref[b], kf_ref[b], vf_ref[b].astype(BF16), gcf_ref[b, 0], grf_ref[b, 0],
                     of_ref, b, ct_sc, n_sc, m_sc, b * MH)
        _mlstm_chunk(1, qb_ref[b], kb_ref[b], vb_ref[b].astype(BF16), gcb_ref[b, 0], grb_ref[b, 0],
                     ob_ref, b, ct_sc, n_sc, m_sc, (bsz + b) * MH)


def _mlstm_call(qc, kc, proj, g_col, g_row, nctx):
    bsz, s, _ = qc.shape
    nch = s // MCH
    nctx_ch = nctx // MCH
    cv = C_MV // MW

    def cb(st):
        return _mlstm_bwd_chunk(st, nctx_ch, nch)

    nst = 2 * bsz * MH
    return pl.pallas_call(
        functools.partial(_mlstm_kernel, bsz=bsz),
        grid=(nch,),
        in_specs=[
            pl.BlockSpec((bsz, MCH, MW), lambda st: (0, st, 0)),
            pl.BlockSpec((bsz, MCH, MW), lambda st: (0, st, 0)),
            pl.BlockSpec((bsz, MCH, MW), lambda st: (0, st, cv)),
            pl.BlockSpec((bsz, 1, MCH, 128), lambda st: (0, 0, st, 0)),
            pl.BlockSpec((bsz, 1, 8, MCH), lambda st: (0, 0, 0, st)),
            pl.BlockSpec((bsz, MCH, MW), lambda st: (0, cb(st), 0)),
            pl.BlockSpec((bsz, MCH, MW), lambda st: (0, cb(st), 0)),
            pl.BlockSpec((bsz, MCH, MW), lambda st: (0, cb(st), cv)),
            pl.BlockSpec((bsz, 1, MCH, 128), lambda st: (0, 1, cb(st), 0)),
            pl.BlockSpec((bsz, 1, 8, MCH), lambda st: (0, 1, 0, cb(st))),
        ],
        out_specs=[
            pl.BlockSpec((bsz, MCH, MW), lambda st: (0, st, 0)),
            pl.BlockSpec((bsz, MCH, MW), lambda st: (0, cb(st), 0)),
        ],
        out_shape=[
            jax.ShapeDtypeStruct((bsz, s, MW), F32),
            jax.ShapeDtypeStruct((bsz, s, MW), F32),
        ],
        scratch_shapes=[
            pltpu.VMEM((nst, MD, MD), F32),
            pltpu.VMEM((nst, 8, MD), F32),
            pltpu.VMEM((nst, 8, 128), F32),
        ],
        compiler_params=_cparams("arbitrary"),
        name="mlstm_scan",
    )(qc, kc, proj, g_col, g_row, qc, kc, proj, g_col, g_row)


def _mlstm_out_kernel(hf_ref, hb_ref, o_ref, z_ref, nw_ref, y_ref):
    hs = hf_ref[0] + hb_ref[0]
    og = o_ref[0]
    zg = z_ref[0]
    nw = nw_ref[...]
    for h in range(MH):
        sl = slice(h * MD, (h + 1) * MD)
        hn = _rms(hs[:, sl], nw[:, sl])
        y_ref[0, :, sl] = (jax.nn.sigmoid(og[:, sl]) * hn * _silu(zg[:, sl])).astype(BF16)


def _mlstm_out_call(h_f, h_b, proj, norm_w):
    bsz, s, _ = h_f.shape
    tr = ROW_TILE
    return pl.pallas_call(
        _mlstm_out_kernel,
        grid=(bsz, s // tr),
        in_specs=[
            pl.BlockSpec((1, tr, MW), lambda b, i: (b, i, 0)),
            pl.BlockSpec((1, tr, MW), lambda b, i: (b, i, 0)),
            pl.BlockSpec((1, tr, MW), lambda b, i: (b, i, C_MO // MW)),
            pl.BlockSpec((1, tr, MW), lambda b, i: (b, i, C_MZ // MW)),
            pl.BlockSpec((1, MW), lambda b, i: (0, 0)),
        ],
        out_specs=pl.BlockSpec((1, tr, MW), lambda b, i: (b, i, 0)),
        out_shape=jax.ShapeDtypeStruct((bsz, s, MW), BF16),
        compiler_params=_cparams("parallel", "parallel"),
        name="mlstm_out",
    )(h_f, h_b, proj, proj, norm_w)


def _mix(ya_ref, yb_ref, yc_ref, wa_ref, wb_ref, wc_ref):
    return _dot(ya_ref[0], wa_ref[...]) + _dot(yb_ref[0], wb_ref[...]) + _dot(yc_ref[0], wc_ref[...])


def _outproj_mid_kernel(ya_ref, yb_ref, yc_ref, wa_ref, wb_ref, wc_ref, x_ref, ctx_ref, g_ref,
                        nw_ref, sh_ref, sc_ref, h_ref, xn_ref, *, nct):
    i = pl.program_id(1)
    upd = g_ref[0] * _mix(ya_ref, yb_ref, yc_ref, wa_ref, wb_ref, wc_ref)

    def emit(res):
        h = res + upd
        h_ref[0] = h
        xn_ref[0] = (_rms(h, nw_ref[0]) * (1.0 + sc_ref[0]) + sh_ref[0]).astype(BF16)

    @pl.when(i < nct)
    def _():
        emit(ctx_ref[0])

    @pl.when(i >= nct)
    def _():
        emit(x_ref[0])


def _outproj_mid_call(ya, yb, yc, w_out, x, ctx, mod, nw, layer):
    bsz, s, _ = ya.shape
    d = x.shape[-1]
    nctx = ctx.shape[1]
    tr = ROW_TILE
    nct = nctx // tr
    wa, wb, wc = w_out[:SSM_W], w_out[SSM_W:SSM_W + AW], w_out[SSM_W + AW:]
    nxt = layer + 1
    return pl.pallas_call(
        functools.partial(_outproj_mid_kernel, nct=nct),
        grid=(bsz, s // tr),
        in_specs=[
            pl.BlockSpec((1, tr, SSM_W), lambda b, i: (b, i, 0)),
            pl.BlockSpec((1, tr, AW), lambda b, i: (b, i, 0)),
            pl.BlockSpec((1, tr, MW), lambda b, i: (b, i, 0)),
            pl.BlockSpec((SSM_W, d), lambda b, i: (0, 0)),
            pl.BlockSpec((AW, d), lambda b, i: (0, 0)),
            pl.BlockSpec((MW, d), lambda b, i: (0, 0)),
            pl.BlockSpec((1, tr, d), lambda b, i: (b, jnp.maximum(i - nct, 0), 0)),
            pl.BlockSpec((1, tr, d), lambda b, i: (b, jnp.minimum(i, nct - 1), 0)),
            _mod_spec(d, layer, 2, nct, bsz),
            pl.BlockSpec((1, 1, d), lambda b, i: (nxt, 0, 0)),
            _mod_spec(d, nxt, 0, nct, bsz),
            _mod_spec(d, nxt, 1, nct, bsz),
        ],
        out_specs=[
            pl.BlockSpec((1, tr, d), lambda b, i: (b, i, 0)),
            pl.BlockSpec((1, tr, d), lambda b, i: (b, i, 0)),
        ],
        out_shape=[
            jax.ShapeDtypeStruct((bsz, s, d), F32),
            jax.ShapeDtypeStruct((bsz, s, d), BF16),
        ],
        compiler_params=_cparams("parallel", "parallel"),
        name="out_proj_mid",
    )(ya, yb, yc, wa, wb, wc, x, ctx, mod, nw, mod, mod)


def _outproj_last_kernel(ya_ref, yb_ref, yc_ref, wa_ref, wb_ref, wc_ref, h_ref, g_ref, nw_ref, o_ref):
    h = h_ref[0] + g_ref[0] * _mix(ya_ref, yb_ref, yc_ref, wa_ref, wb_ref, wc_ref)
    o_ref[0] = _rms(h, nw_ref[...])


def _outproj_last_call(ya, yb, yc, w_out, h_prev, mod, final_w, layer, nctx):
    bsz, s, d = h_prev.shape
    tr = ROW_TILE
    nct = nctx // tr
    n_lat = s - nctx
    wa, wb, wc = w_out[:SSM_W], w_out[SSM_W:SSM_W + AW], w_out[SSM_W + AW:]
    return pl.pallas_call(
        _outproj_last_kernel,
        grid=(bsz, n_lat // tr),
        in_specs=[
            pl.BlockSpec((1, tr, SSM_W), lambda b, i: (b, i + nct, 0)),
            pl.BlockSpec((1, tr, AW), lambda b, i: (b, i + nct, 0)),
            pl.BlockSpec((1, tr, MW), lambda b, i: (b, i + nct, 0)),
            pl.BlockSpec((SSM_W, d), lambda b, i: (0, 0)),
            pl.BlockSpec((AW, d), lambda b, i: (0, 0)),
            pl.BlockSpec((MW, d), lambda b, i: (0, 0)),
            pl.BlockSpec((1, tr, d), lambda b, i: (b, i + nct, 0)),
            pl.BlockSpec((1, 1, d), lambda b, i: (layer * 8 + b, 0, 2)),
            pl.BlockSpec((1, d), lambda b, i: (0, 0)),
        ],
        out_specs=pl.BlockSpec((1, tr, d), lambda b, i: (b, i, 0)),
        out_shape=jax.ShapeDtypeStruct((bsz, n_lat, d), F32),
        compiler_params=_cparams("parallel", "parallel"),
        name="out_proj_last",
    )(ya, yb, yc, wa, wb, wc, h_prev, mod, final_w)


def _reorder_w_in(w):
    g0 = C_MO + MW
    return jnp.concatenate([w[:, :g0], w[:, g0 + 4 * MH:]], axis=1), w[:, g0:g0 + 4 * MH]


def kernel(x, c, ctx, c_ctx, norm_w, ada_w, ada_b, w_in, mlstm_gate_b, ssm_a_re, ssm_a_im, ssm_log_dt,
           ssm_b_re, ssm_b_im, ssm_c_re, ssm_c_im, ssm_d, ssm_w_glu, attn_q_norm, attn_k_norm,
           mlstm_conv_w, mlstm_conv_b, mlstm_norm_w, w_out, final_norm_w):
    bsz, n_lat, d = x.shape
    nctx = ctx.shape[1]
    s = nctx + n_lat
    depth = norm_w.shape[0]
    assert bsz < 8 and nctx % ROW_TILE == 0 and n_lat % ROW_TILE == 0 and depth == 2

    c_rows = jnp.zeros((8, d), F32).at[:bsz].set(c).at[bsz].set(c_ctx)
    mod = _ada_call(c_rows, ada_w, ada_b).reshape(depth * 8, 1, 3 * d)
    cos2, sin2 = _rope_tables(n_lat, nctx)

    norm_w3 = norm_w.astype(F32).reshape(depth, 1, d)
    xn = _prenorm_call(x, ctx, norm_w3, mod, 0)
    h_prev = None
    out = None
    for layer in range(depth):
        w_main, w_g = _reorder_w_in(w_in[layer])
        proj = _inproj_call(xn.reshape(bsz * s, d), w_main.astype(BF16)).reshape(bsz, s, PROJ_W)

        wg = w_g.reshape(d, 2, 2 * MH)
        w_gc = jnp.zeros((2, d, 128), F32).at[:, :, :2 * MH].set(wg.transpose(1, 0, 2)).astype(BF16)
        w_gr = wg.transpose(1, 2, 0).astype(BF16)
        gb = mlstm_gate_b[layer].astype(F32).reshape(2, 2 * MH)
        b_gc = jnp.zeros((2, 1, 128), F32).at[:, 0, :2 * MH].set(gb)
        b_gr = jnp.broadcast_to(gb[:, :, None], (2, 2 * MH, 128))
        g_col, g_row = _gates_call(xn, w_gc, w_gr, b_gc, b_gr)

        ya = _s5_branch(proj, dict(a_re=ssm_a_re[layer], a_im=ssm_a_im[layer], log_dt=ssm_log_dt[layer],
                                   b_re=ssm_b_re[layer], b_im=ssm_b_im[layer], c_re=ssm_c_re[layer],
                                   c_im=ssm_c_im[layer], d=ssm_d[layer], w_glu=ssm_w_glu[layer]), nctx)

        qs, ks, vs = _attn_prep_call(proj, cos2, sin2, attn_q_norm[layer].reshape(1, HD).astype(F32),
                                     attn_k_norm[layer].reshape(1, HD).astype(F32))
        yb = _attn_call(qs, ks, vs, proj, nctx)

        conv_w = jnp.zeros((8, 2 * MW), F32).at[:3].set(mlstm_conv_w[layer].astype(F32))
        qc, kc = _mlstm_prep_call(proj, conv_w, mlstm_conv_b[layer].astype(F32).reshape(1, 2 * MW), nctx)
        h_f, h_b = _mlstm_call(qc, kc, proj, g_col, g_row, nctx)
        yc = _mlstm_out_call(h_f, h_b, proj, mlstm_norm_w[layer].astype(F32).reshape(1, MW))

        w_o = w_out[layer].astype(BF16)
        if layer < depth - 1:
            h_prev, xn = _outproj_mid_call(ya, yb, yc, w_o, x, ctx, mod, norm_w3, layer)
        else:
            out = _outproj_last_call(ya, yb, yc, w_o, h_prev, mod, final_norm_w.reshape(1, d), layer, nctx)
    return out
```

```python
import functools
import math

import jax
import jax.numpy as jnp
from jax import lax
from jax.experimental import pallas as pl
from jax.experimental.pallas import tpu as pltpu

F32 = jnp.float32
BF16 = jnp.bfloat16
EPS = 1e-6

SSM_W = 512
SSM_P = 16
SSM_G = 32
SSM_N = 64
S5_T = 16
S5_SLAB_G = 8
S5_SLABS = SSM_G // S5_SLAB_G
S5_ST = S5_SLAB_G * SSM_N
AH = 8
AKV = 2
AGRP = AH // AKV
HD = 128
AW = AH * HD
KVW = AKV * HD
GRID_W = 64
ROPE_THETA = 10000.0
MH = 4
MD = 128
MW = MH * MD
MCH = 128
C_U, C_Z, C_AQ, C_AK, C_AV, C_AZ, C_MQ, C_MK, C_MV, C_MO, C_MZ = (
    0, 512, 1024, 2048, 2304, 2560, 3584, 4096, 4608, 5120, 5632)
PROJ_W = 6144
MIX_W = SSM_W + AW + MW

ROW_TILE = 256
VMEM_LIMIT = 48 * 1024 * 1024


def _cparams(*sem):
    return pltpu.CompilerParams(dimension_semantics=sem, vmem_limit_bytes=VMEM_LIMIT)


def _dot(a, b):
    return jnp.dot(a, b, preferred_element_type=F32)


def _dot_nt(a, b):
    return lax.dot_general(a, b, (((1,), (1,)), ((), ())), preferred_element_type=F32)


def _dot_tn(a, b):
    return lax.dot_general(a, b, (((0,), (0,)), ((), ())), preferred_element_type=F32)


def _split3(x):
    hi = x.astype(BF16)
    r1 = x - hi.astype(F32)
    mid = r1.astype(BF16)
    lo = (r1 - mid.astype(F32)).astype(BF16)
    return hi, mid, lo


def _silu(x):
    return x * jax.nn.sigmoid(x)


def _log_sigmoid(x):
    return jnp.minimum(x, 0.0) - jnp.log1p(jnp.exp(-jnp.abs(x)))


def _rms(x, w):
    return x * lax.rsqrt(jnp.mean(x * x, axis=-1, keepdims=True) + EPS) * w


def _pick_tile(n, cap, mult):
    best = mult
    for t in range(mult, min(n, cap) + 1, mult):
        if n % t == 0:
            best = t
    return best


def _ada_kernel(c_ref, w_ref, b_ref, o_ref):
    a = _silu(c_ref[...])
    w = w_ref[0]
    a_hi = a.astype(BF16)
    a_lo = (a - a_hi.astype(F32)).astype(BF16)
    w_hi = w.astype(BF16)
    w_lo = (w - w_hi.astype(F32)).astype(BF16)
    o_ref[0] = _dot(a_hi, w_hi) + _dot(a_hi, w_lo) + _dot(a_lo, w_hi) + b_ref[0]


def _ada_call(c_rows, ada_w, ada_b):
    depth, d, n3 = ada_w.shape
    tn = 512
    return pl.pallas_call(
        _ada_kernel,
        grid=(depth, n3 // tn),
        in_specs=[
            pl.BlockSpec((8, d), lambda l, j: (0, 0)),
            pl.BlockSpec((1, d, tn), lambda l, j: (l, 0, j)),
            pl.BlockSpec((1, 1, tn), lambda l, j: (l, 0, j)),
        ],
        out_specs=pl.BlockSpec((1, 8, tn), lambda l, j: (l, 0, j)),
        out_shape=jax.ShapeDtypeStruct((depth, 8, n3), F32),
        compiler_params=_cparams("parallel", "parallel"),
        name="ada_mod",
    )(c_rows, ada_w, ada_b.reshape(depth, 1, n3))


def _prenorm_kernel(x_ref, ctx_ref, nw_ref, sh_ref, sc_ref, o_ref, *, nct):
    i = pl.program_id(1)

    def emit(h):
        y = _rms(h, nw_ref[0])
        o_ref[0] = (y * (1.0 + sc_ref[0]) + sh_ref[0]).astype(BF16)

    @pl.when(i < nct)
    def _():
        emit(ctx_ref[0])

    @pl.when(i >= nct)
    def _():
        emit(x_ref[0])


def _mod_spec(d, layer, part, nct, nb):
    def idx(b, i):
        return (layer * 8 + jnp.where(i < nct, nb, b), 0, part)
    return pl.BlockSpec((1, 1, d), idx)


def _prenorm_call(x, ctx, nw, mod, layer):
    bsz, n, d = x.shape
    nctx = ctx.shape[1]
    tr = ROW_TILE
    nct = nctx // tr
    s = nctx + n
    return pl.pallas_call(
        functools.partial(_prenorm_kernel, nct=nct),
        grid=(bsz, s // tr),
        in_specs=[
            pl.BlockSpec((1, tr, d), lambda b, i: (b, jnp.maximum(i - nct, 0), 0)),
            pl.BlockSpec((1, tr, d), lambda b, i: (b, jnp.minimum(i, nct - 1), 0)),
            pl.BlockSpec((1, 1, d), lambda b, i: (layer, 0, 0)),
            _mod_spec(d, layer, 0, nct, bsz),
            _mod_spec(d, layer, 1, nct, bsz),
        ],
        out_specs=pl.BlockSpec((1, tr, d), lambda b, i: (b, i, 0)),
        out_shape=jax.ShapeDtypeStruct((bsz, s, d), BF16),
        compiler_params=_cparams("parallel", "parallel"),
        name="prenorm",
    )(x, ctx, nw, mod, mod)


def _matmul_kernel(x_ref, w_ref, o_ref):
    o_ref[...] = _dot(x_ref[...], w_ref[...])


def _inproj_call(xn2d, w):
    m, d = xn2d.shape
    n = w.shape[1]
    tm = _pick_tile(m, 1056, 16)
    tn = 1024
    return pl.pallas_call(
        _matmul_kernel,
        grid=(n // tn, m // tm),
        in_specs=[
            pl.BlockSpec((tm, d), lambda j, i: (i, 0)),
            pl.BlockSpec((d, tn), lambda j, i: (0, j)),
        ],
        out_specs=pl.BlockSpec((tm, tn), lambda j, i: (i, j)),
        out_shape=jax.ShapeDtypeStruct((m, n), F32),
        compiler_params=_cparams("parallel", "parallel"),
        name="in_proj",
    )(xn2d, w)


def _gates_kernel(x_ref, wr_ref, br_ref, gr_ref):
    x = x_ref[0]
    for dirn in range(2):
        gr_ref[0, dirn] = _dot_nt(wr_ref[dirn], x) + br_ref[dirn][:, :1]


def _gates_call(xn, w_gr, b_gr):
    bsz, s, d = xn.shape
    tr = ROW_TILE
    return pl.pallas_call(
        _gates_kernel,
        grid=(bsz, s // tr),
        in_specs=[
            pl.BlockSpec((1, tr, d), lambda b, i: (b, i, 0)),
            pl.BlockSpec((2, 8, d), lambda b, i: (0, 0, 0)),
            pl.BlockSpec((2, 8, 128), lambda b, i: (0, 0, 0)),
        ],
        out_specs=pl.BlockSpec((1, 2, 8, tr), lambda b, i: (b, 0, 0, i)),
        out_shape=jax.ShapeDtypeStruct((bsz, 2, 8, s), F32),
        compiler_params=_cparams("parallel", "parallel"),
        name="mlstm_gates",
    )(xn, w_gr, b_gr)


def _s5_weights(a_re, a_im, log_dt, b_re, b_im, c_re, c_im):
    t_ = S5_T
    a_re = a_re.astype(F32)
    a_im = a_im.astype(F32)
    dt = jnp.exp(log_dt.astype(F32))[..., None]
    mag = jnp.exp(a_re * dt)
    lam_re = mag * jnp.cos(a_im * dt)
    lam_im = mag * jnp.sin(a_im * dt)
    inv_abs2 = 1.0 / (a_re * a_re + a_im * a_im)
    num_re, num_im = lam_re - 1.0, lam_im
    f_re = (num_re * a_re + num_im * a_im) * inv_abs2
    f_im = (num_im * a_re - num_re * a_im) * inv_abs2
    b_re = b_re.astype(F32)[None]
    b_im = b_im.astype(F32)[None]
    bb_re = f_re[:, :, None, :] * b_re - f_im[:, :, None, :] * b_im
    bb_im = f_re[:, :, None, :] * b_im + f_im[:, :, None, :] * b_re
    c_re = c_re.astype(F32)
    c_im = c_im.astype(F32)

    pr = [jnp.ones_like(lam_re)]
    pi = [jnp.zeros_like(lam_im)]
    for _ in range(t_):
        pr.append(pr[-1] * lam_re - pi[-1] * lam_im)
        pi.append(pr[-2] * lam_im + pi[-1] * lam_re)

    def powers(exps_f, exps_b):
        re = jnp.stack([jnp.stack([pr[e][0] for e in exps_f]), jnp.stack([pr[e][1] for e in exps_b])])
        im = jnp.stack([jnp.stack([pi[e][0] for e in exps_f]), jnp.stack([pi[e][1] for e in exps_b])])
        return re[:, :, :, None, :], im[:, :, :, None, :]

    def slab_rows(w):
        w = w.reshape(2, t_, S5_SLABS, S5_SLAB_G, SSM_P, 2 * SSM_N).transpose(0, 2, 1, 3, 4, 5)
        return w.reshape(2, S5_SLABS, t_ * 128, 2 * SSM_N).astype(BF16)

    steps = list(range(t_))
    p_re, p_im = powers([t_ - 1 - t for t in steps], steps)
    win = slab_rows(jnp.concatenate([bb_re[:, None] * p_re - bb_im[:, None] * p_im,
                                     bb_re[:, None] * p_im + bb_im[:, None] * p_re], axis=-1))
    p_re, p_im = powers([t + 1 for t in steps], [t_ - t for t in steps])
    wout = slab_rows(jnp.concatenate([c_re[:, None] * p_re - c_im[:, None] * p_im,
                                      -(c_re[:, None] * p_im + c_im[:, None] * p_re)], axis=-1))

    p_re, p_im = powers(steps, steps)
    kr = c_re[:, None] * p_re - c_im[:, None] * p_im
    ki = c_re[:, None] * p_im + c_im[:, None] * p_re
    kk = (jnp.einsum('dgqn,dtgpn->dgqtp', bb_re, kr, precision=lax.Precision.HIGHEST)
          - jnp.einsum('dgqn,dtgpn->dgqtp', bb_im, ki, precision=lax.Precision.HIGHEST))
    k_f = kk[0].reshape(S5_SLABS, 128, t_ * SSM_P)
    k_b = kk[1][:, :, ::-1, :].reshape(S5_SLABS, 128, t_ * SSM_P)

    def lam_row(v):
        return v.reshape(2, S5_SLABS, 1, S5_ST)

    lam_t = jnp.concatenate([lam_row(pr[t_]), lam_row(pi[t_])], axis=-1)
    return win, wout, k_f, k_b, lam_t


def _iota(shape, dim):
    return lax.broadcasted_iota(jnp.int32, shape, dim)


def _s5_core_kernel(u_ref, win_ref, wout_ref, kf_ref, kb_ref, lam_ref, y_ref, lhs_sc, w_sc, s_sc, acc_sc,
                    *, nch, nctx_ch):
    t_ = S5_T
    n2 = 2 * SSM_N
    st2 = 2 * S5_ST
    lg_p = SSM_P.bit_length() - 1
    lg_n = SSM_N.bit_length() - 1
    lg_st = S5_ST.bit_length() - 1

    for t in range(t_):
        lhs_sc[:, t * 128:(t + 1) * 128] = u_ref[0, pl.ds(t, nch, stride=t_), :].astype(BF16)

    k1, c1 = _iota((n2, st2), 0), _iota((n2, st2), 1)
    e_in = jnp.where(((k1 >> lg_n) == (c1 >> lg_st)) & ((k1 & (SSM_N - 1)) == (c1 & (SSM_N - 1))),
                     1.0, 0.0).astype(BF16)
    r1, c1b = _iota((128, st2), 0), _iota((128, st2), 1)
    m_in = (r1 >> lg_p) == ((c1b & (S5_ST - 1)) >> lg_n)

    def expand_state_table(tab_ref, d):
        for t in range(t_):
            rows = slice(t * 128, (t + 1) * 128)
            w_sc[rows, :st2] = jnp.where(m_in, _dot(tab_ref[d, 0, rows, :], e_in), 0.0).astype(BF16)

    for d in range(2):
        expand_state_table(win_ref, d)
        s_sc[d] = _dot(lhs_sc[...], w_sc[:, :st2])

    lam_f = lam_ref[0, 0]
    lam_b = lam_ref[1, 0]
    lrf, lif = lam_f[:, :S5_ST], lam_f[:, S5_ST:]
    lrb, lib = lam_b[:, :S5_ST], lam_b[:, S5_ST:]

    def step(k, carry):
        hrf, hif, hrb, hib = carry
        cb = jnp.where(k < nctx_ch, nctx_ch - 1 - k, nch - 1 - (k - nctx_ch))
        sf = s_sc[0, pl.ds(k, 1), :]
        sb = s_sc[1, pl.ds(cb, 1), :]
        s_sc[0, pl.ds(k, 1), :] = jnp.concatenate([hrf, hif], axis=-1)
        s_sc[1, pl.ds(cb, 1), :] = jnp.concatenate([hrb, hib], axis=-1)
        return (lrf * hrf - lif * hif + sf[:, :S5_ST], lrf * hif + lif * hrf + sf[:, S5_ST:],
                lrb * hrb - lib * hib + sb[:, :S5_ST], lrb * hib + lib * hrb + sb[:, S5_ST:])

    zero = jnp.zeros((1, S5_ST), F32)
    lax.fori_loop(0, nch, step, (zero, zero, zero, zero))

    k2, c2 = _iota((t_ * SSM_P, t_ * 128), 0), _iota((t_ * SSM_P, t_ * 128), 1)
    e_m = jnp.where(((k2 >> lg_p) == (c2 >> 7)) & ((k2 & (SSM_P - 1)) == (c2 & (SSM_P - 1))),
                    1.0, 0.0).astype(BF16)
    r2, c2b = _iota((128, t_ * 128), 0), _iota((128, t_ * 128), 1)
    m_m = (r2 >> lg_p) == ((c2b & 127) >> lg_p)
    k_f = kf_ref[0]
    k_b = kb_ref[0]
    lag_col = _iota((128, t_ * SSM_P), 1)
    for t in range(t_):
        rows = slice(t * 128, (t + 1) * 128)
        fwd = k_f if t == 0 else pltpu.roll(k_f, t * SSM_P, 1)
        bwd = k_b if t == t_ - 1 else pltpu.roll(k_b, (t + 1) * SSM_P, 1)
        mc = (jnp.where(lag_col >= t * SSM_P, fwd, 0.0)
              + jnp.where(lag_col < (t + 1) * SSM_P, bwd, 0.0)).astype(BF16)
        w_sc[rows, :] = jnp.where(m_m, _dot(mc, e_m), 0.0).astype(BF16)
    acc_sc[...] = _dot(lhs_sc[...], w_sc[...])

    for d in range(2):
        expand_state_table(wout_ref, d)
        acc_sc[...] += _dot_nt(s_sc[d].astype(BF16), w_sc[:, :st2])

    for t in range(t_):
        y_ref[0, pl.ds(t, nch, stride=t_), :] = acc_sc[:, t * 128:(t + 1) * 128]


def _s5_core_call(proj, win, wout, k_f, k_b, lam_t, nctx):
    bsz, s, _ = proj.shape
    t_ = S5_T
    nch = s // t_
    kdim = t_ * 128
    n2 = 2 * SSM_N
    st2 = 2 * S5_ST
    cu = C_U // 128
    return pl.pallas_call(
        functools.partial(_s5_core_kernel, nch=nch, nctx_ch=nctx // t_),
        grid=(S5_SLABS, bsz),
        in_specs=[
            pl.BlockSpec((1, s, 128), lambda sl, b: (b, 0, cu + sl)),
            pl.BlockSpec((2, 1, kdim, n2), lambda sl, b: (0, sl, 0, 0)),
            pl.BlockSpec((2, 1, kdim, n2), lambda sl, b: (0, sl, 0, 0)),
            pl.BlockSpec((1, 128, t_ * SSM_P), lambda sl, b: (sl, 0, 0)),
            pl.BlockSpec((1, 128, t_ * SSM_P), lambda sl, b: (sl, 0, 0)),
            pl.BlockSpec((2, 1, 1, st2), lambda sl, b: (0, sl, 0, 0)),
        ],
        out_specs=pl.BlockSpec((1, s, 128), lambda sl, b: (b, 0, sl)),
        out_shape=jax.ShapeDtypeStruct((bsz, s, SSM_W), F32),
        scratch_shapes=[
            pltpu.VMEM((nch, kdim), BF16),
            pltpu.VMEM((kdim, kdim), BF16),
            pltpu.VMEM((2, nch, st2), F32),
            pltpu.VMEM((nch, kdim), F32),
        ],
        compiler_params=pltpu.CompilerParams(dimension_semantics=("parallel", "parallel"),
                                             vmem_limit_bytes=56 * 1024 * 1024),
        name="s5_core",
    )(proj, win, wout, k_f, k_b, lam_t)


def _s5_glu_kernel(y_ref, u_ref, z_ref, d_ref, w_ref, o_ref):
    y = y_ref[0] + d_ref[...] * u_ref[0]
    g = jax.nn.gelu(y, approximate=True)
    gate = jax.nn.sigmoid(_dot(g.astype(BF16), w_ref[...]))
    o_ref[0] = (g * gate * _silu(z_ref[0])).astype(BF16)


def _s5_glu_call(yssm, proj, d_row, w_glu):
    bsz, s, _ = yssm.shape
    tr = ROW_TILE
    return pl.pallas_call(
        _s5_glu_kernel,
        grid=(bsz, s // tr),
        in_specs=[
            pl.BlockSpec((1, tr, SSM_W), lambda b, i: (b, i, 0)),
            pl.BlockSpec((1, tr, SSM_W), lambda b, i: (b, i, C_U // SSM_W)),
            pl.BlockSpec((1, tr, SSM_W), lambda b, i: (b, i, C_Z // SSM_W)),
            pl.BlockSpec((1, SSM_W), lambda b, i: (0, 0)),
            pl.BlockSpec((SSM_W, SSM_W), lambda b, i: (0, 0)),
        ],
        out_specs=pl.BlockSpec((1, tr, SSM_W), lambda b, i: (b, i, 0)),
        out_shape=jax.ShapeDtypeStruct((bsz, s, SSM_W), BF16),
        compiler_params=_cparams("parallel", "parallel"),
        name="s5_glu",
    )(yssm, proj, proj, d_row, w_glu)


def _s5_branch(proj, p, nctx):
    win, wout, k_f, k_b, lam_t = _s5_weights(p['a_re'], p['a_im'], p['log_dt'], p['b_re'], p['b_im'],
                                             p['c_re'], p['c_im'])
    yssm = _s5_core_call(proj, win, wout, k_f, k_b, lam_t, nctx)
    return _s5_glu_call(yssm, proj, p['d'].astype(F32).reshape(1, SSM_W), p['w_glu'].astype(BF16))


def _rope_tables(n_lat, nctx):
    rows = n_lat // GRID_W
    row = jnp.broadcast_to(jnp.arange(rows, dtype=F32)[:, None], (rows, GRID_W)).reshape(-1)
    col = jnp.broadcast_to(jnp.arange(GRID_W, dtype=F32)[None, :], (rows, GRID_W)).reshape(-1)
    n_freq = HD // 4
    inv = ROPE_THETA ** (-jnp.arange(n_freq, dtype=F32) / n_freq)
    ang = jnp.concatenate([row[:, None] * inv, col[:, None] * inv], axis=-1)
    cs, sn = jnp.cos(ang), jnp.sin(ang)
    cos2 = jnp.concatenate([cs, cs], axis=-1)
    sin2 = jnp.concatenate([-sn, sn], axis=-1)
    cos2 = jnp.concatenate([jnp.ones((nctx, HD), F32), cos2], axis=0)
    sin2 = jnp.concatenate([jnp.zeros((nctx, HD), F32), sin2], axis=0)
    return cos2, sin2


def _attn_prep_kernel(q_ref, k_ref, v_ref, cos_ref, sin_ref, qw_ref, kw_ref, qo_ref, ko_ref, vo_ref):
    cs = cos_ref[...]
    sn = sin_ref[...]

    def norm_rope(x, w):
        y = _rms(x, w)
        return y * cs + pltpu.roll(y, HD // 2, 1) * sn

    q = q_ref[0]
    for h in range(AH):
        r = norm_rope(q[:, h * HD:(h + 1) * HD], qw_ref[...])
        qo_ref[0, :, h * HD:(h + 1) * HD] = (r * (HD ** -0.5 * math.log2(math.e))).astype(BF16)
    k = k_ref[0]
    for h in range(AKV):
        ko_ref[0, :, h * HD:(h + 1) * HD] = norm_rope(k[:, h * HD:(h + 1) * HD], kw_ref[...]).astype(BF16)
    vo_ref[0] = v_ref[0].astype(BF16)


def _attn_prep_call(proj, cos2, sin2, qw, kw):
    bsz, s, _ = proj.shape
    tr = ROW_TILE
    return pl.pallas_call(
        _attn_prep_kernel,
        grid=(bsz, s // tr),
        in_specs=[
            pl.BlockSpec((1, tr, AW), lambda b, i: (b, i, C_AQ // AW)),
            pl.BlockSpec((1, tr, KVW), lambda b, i: (b, i, C_AK // KVW)),
            pl.BlockSpec((1, tr, KVW), lambda b, i: (b, i, C_AV // KVW)),
            pl.BlockSpec((tr, HD), lambda b, i: (i, 0)),
            pl.BlockSpec((tr, HD), lambda b, i: (i, 0)),
            pl.BlockSpec((1, HD), lambda b, i: (0, 0)),
            pl.BlockSpec((1, HD), lambda b, i: (0, 0)),
        ],
        out_specs=[
            pl.BlockSpec((1, tr, AW), lambda b, i: (b, i, 0)),
            pl.BlockSpec((1, tr, KVW), lambda b, i: (b, i, 0)),
            pl.BlockSpec((1, tr, KVW), lambda b, i: (b, i, 0)),
        ],
        out_shape=[
            jax.ShapeDtypeStruct((bsz, s, AW), BF16),
            jax.ShapeDtypeStruct((bsz, s, KVW), BF16),
            jax.ShapeDtypeStruct((bsz, s, KVW), BF16),
        ],
        compiler_params=_cparams("parallel", "parallel"),
        name="attn_prep",
    )(proj, proj, proj, cos2, sin2, qw, kw)


def _attn_kernel(q_ref, k_ref, v_ref, z_ref, o_ref, s_sc, *, tq, nct, nctx, ck, nck):
    qi = pl.program_id(2)
    q = q_ref[0]
    q4 = jnp.concatenate([q[:, h * HD:(h + 1) * HD] for h in range(AGRP)], axis=0)

    def finish(acc, l):
        o = acc / l
        z = z_ref[0]
        for h in range(AGRP):
            o_ref[0, :, h * HD:(h + 1) * HD] = (
                o[h * tq:(h + 1) * tq] * _silu(z[:, h * HD:(h + 1) * HD])).astype(BF16)

    @pl.when(qi < nct)
    def _():
        s = _dot_nt(q4, k_ref[0, :nctx, :])
        m = jnp.max(s, axis=-1, keepdims=True)
        p = jnp.exp2(s - m)
        l = jnp.sum(p, axis=-1, keepdims=True)
        finish(_dot(p.astype(BF16), v_ref[0, :nctx, :]), l)

    @pl.when(qi >= nct)
    def _():
        s_sc[0] = _dot_nt(q4, k_ref[0, 0:ck, :])
        m = l = acc = None
        for j in range(nck):
            if j + 1 < nck:
                s_sc[(j + 1) % 2] = _dot_nt(q4, k_ref[0, (j + 1) * ck:(j + 2) * ck, :])
            s = s_sc[j % 2]
            s_max = jnp.max(s, axis=-1, keepdims=True)
            m_new = s_max if j == 0 else jnp.maximum(m, s_max)
            p = jnp.exp2(s - m_new)
            pv = _dot(p.astype(BF16), v_ref[0, j * ck:(j + 1) * ck, :])
            p_sum = jnp.sum(p, axis=-1, keepdims=True)
            if j == 0:
                l, acc = p_sum, pv
            else:
                alpha = jnp.exp2(m - m_new)
                l = alpha * l + p_sum
                acc = alpha * acc + pv
            m = m_new
        finish(acc, l)


def _attn_call(qs, ks, vs, proj, nctx):
    bsz, s, _ = qs.shape
    tq = ROW_TILE
    nct = nctx // tq
    ck = _pick_tile(s, 1536, 128)
    nck = s // ck
    zw = AGRP * HD
    return pl.pallas_call(
        functools.partial(_attn_kernel, tq=tq, nct=nct, nctx=nctx, ck=ck, nck=nck),
        grid=(bsz, AKV, s // tq),
        in_specs=[
            pl.BlockSpec((1, tq, zw), lambda b, g, i: (b, i, g)),
            pl.BlockSpec((1, s, HD), lambda b, g, i: (b, 0, g)),
            pl.BlockSpec((1, s, HD), lambda b, g, i: (b, 0, g)),
            pl.BlockSpec((1, tq, zw), lambda b, g, i: (b, i, C_AZ // zw + g)),
        ],
        out_specs=pl.BlockSpec((1, tq, zw), lambda b, g, i: (b, i, g)),
        out_shape=jax.ShapeDtypeStruct((bsz, s, AW), BF16),
        scratch_shapes=[pltpu.VMEM((2, AGRP * tq, ck), F32)],
        compiler_params=_cparams("parallel", "parallel", "parallel"),
        name="attention",
    )(qs, ks, vs, proj)


def _mlstm_prep_kernel(q_ref, k_ref, qp_ref, kp_ref, qn_ref, kn_ref, w_ref, b_ref, qo_ref, ko_ref,
                       *, tr, nct, nt):
    i = pl.program_id(1)
    first = jnp.logical_or(i == 0, i == nct)
    last = jnp.logical_or(i == nct - 1, i == nt - 1)
    rid = lax.broadcasted_iota(jnp.int32, (tr, MW), 0)
    w = w_ref[...]
    bias = b_ref[...]

    def conv(x, prev_row, next_row, off):
        prev_row = jnp.where(first, 0.0, prev_row)
        next_row = jnp.where(last, 0.0, next_row)
        xp = jnp.where(rid == 0, prev_row, pltpu.roll(x, 1, 0))
        xn = jnp.where(rid == tr - 1, next_row, pltpu.roll(x, tr - 1, 0))
        y = (w[0:1, off:off + MW] * xp + w[1:2, off:off + MW] * x + w[2:3, off:off + MW] * xn
             + bias[:, off:off + MW])
        return _silu(y)

    qo_ref[0] = conv(q_ref[0], qp_ref[0, 7:8, :], qn_ref[0, 0:1, :], 0).astype(BF16)
    ko_ref[0] = (conv(k_ref[0], kp_ref[0, 7:8, :], kn_ref[0, 0:1, :], MW) * (MD ** -0.5)).astype(BF16)


def _mlstm_prep_call(proj, conv_w, conv_b, nctx):
    bsz, s, _ = proj.shape
    tr = ROW_TILE
    nt = s // tr
    nct = nctx // tr
    r8 = tr // 8
    n8 = s // 8
    cq = C_MQ // MW
    ck = C_MK // MW

    def prev_map(c):
        return lambda b, i: (b, jnp.maximum(i * r8 - 1, 0), c)

    def next_map(c):
        return lambda b, i: (b, jnp.minimum((i + 1) * r8, n8 - 1), c)

    return pl.pallas_call(
        functools.partial(_mlstm_prep_kernel, tr=tr, nct=nct, nt=nt),
        grid=(bsz, nt),
        in_specs=[
            pl.BlockSpec((1, tr, MW), lambda b, i: (b, i, cq)),
            pl.BlockSpec((1, tr, MW), lambda b, i: (b, i, ck)),
            pl.BlockSpec((1, 8, MW), prev_map(cq)),
            pl.BlockSpec((1, 8, MW), prev_map(ck)),
            pl.BlockSpec((1, 8, MW), next_map(cq)),
            pl.BlockSpec((1, 8, MW), next_map(ck)),
            pl.BlockSpec((8, 2 * MW), lambda b, i: (0, 0)),
            pl.BlockSpec((1, 2 * MW), lambda b, i: (0, 0)),
        ],
        out_specs=[
            pl.BlockSpec((1, tr, MW), lambda b, i: (b, i, 0)),
            pl.BlockSpec((1, tr, MW), lambda b, i: (b, i, 0)),
        ],
        out_shape=[
            jax.ShapeDtypeStruct((bsz, s, MW), BF16),
            jax.ShapeDtypeStruct((bsz, s, MW), BF16),
        ],
        compiler_params=_cparams("parallel", "parallel"),
        name="mlstm_prep",
    )(proj, proj, proj, proj, proj, proj, conv_w, conv_b)


def _mlstm_bwd_chunk(step, nctx_ch, nch):
    return jnp.where(step < nctx_ch, nctx_ch - 1 - step, nch - 1 - (step - nctx_ch))


def _hi_mid(x):
    hi = x.astype(BF16)
    mid = (x - hi.astype(F32)).astype(BF16)
    return jnp.concatenate([hi, mid], axis=1)


def _mlstm_chunk(dirn, q, k, v, gr, o_ref, b, ct_sc, n_sc, m_sc, base):
    row = _iota((MCH, MCH), 0)
    col = _iota((MCH, MCH), 1)
    allowed = (col <= row) if dirn == 0 else (col >= row)
    eye = col == row
    tri = jnp.where(allowed, 1.0, 0.0).astype(BF16)
    lf_r = _log_sigmoid(gr)
    r3 = _split3(lf_r)
    b_r = _dot_nt(r3[0], tri) + _dot_nt(r3[1], tri) + _dot_nt(r3[2], tri)

    a8 = jnp.concatenate([gr[:MH] - b_r[MH:], gr[:MH] - b_r[MH:]], axis=0)
    lane = _iota((8, MCH), 1)
    cm8 = a8
    sh = 1
    while sh < MCH:
        if dirn == 0:
            cm8 = jnp.maximum(cm8, jnp.where(lane >= sh, pltpu.roll(cm8, sh, 1), -jnp.inf))
        else:
            cm8 = jnp.maximum(cm8, jnp.where(lane < MCH - sh, pltpu.roll(cm8, MCH - sh, 1), -jnp.inf))
        sh *= 2

    ones2 = jnp.ones((2 * MCH, MCH), BF16)
    lhs = []
    for h in range(MH):
        lhs += [_hi_mid(jnp.where(allowed, lf_r[MH + h:MH + h + 1, :], 0.0)),
                _hi_mid(jnp.where(eye, cm8[h:h + 1, :], 0.0)),
                _hi_mid(jnp.where(eye, gr[h:h + 1, :], 0.0))]
    rep = _dot(jnp.concatenate(lhs, axis=0), ones2)

    heads = []
    lhs2 = []
    for h in range(MH):
        st = base + h
        bc = rep[(3 * h) * MCH:(3 * h + 1) * MCH]
        cm = rep[(3 * h + 1) * MCH:(3 * h + 2) * MCH]
        lic = rep[(3 * h + 2) * MCH:(3 * h + 3) * MCH]
        br = b_r[MH + h:MH + h + 1, :]
        li_r = gr[h:h + 1, :]
        m_row = m_sc[st][:1, :]
        qh = q[:, h * MD:(h + 1) * MD]
        kh = k[:, h * MD:(h + 1) * MD]

        log_d = jnp.where(allowed, bc - br + li_r, -jnp.inf)
        m_t = bc + jnp.maximum(m_row, cm)
        s = _dot_nt(qh, kh) * jnp.exp(log_d - m_t)
        s_bf = s.astype(BF16)
        lhs2 += [s_bf, (qh.astype(F32) * n_sc[st][:1, :]).astype(BF16)]
        heads.append((st, bc, lic, m_row, m_t, qh, kh, s_bf))

    rep2 = _dot(jnp.concatenate(lhs2, axis=0), jnp.ones((MCH, MCH), BF16))

    for h, (st, bc, lic, m_row, m_t, qh, kh, s_bf) in enumerate(heads):
        vh = v[:, h * MD:(h + 1) * MD]
        s_sum = rep2[(2 * h) * MCH:(2 * h + 1) * MCH]
        qn = rep2[(2 * h + 1) * MCH:(2 * h + 2) * MCH]
        w_inter = jnp.exp(bc + m_row - m_t)
        num = _dot(s_bf, vh) + w_inter * _dot(qh, ct_sc[st].astype(BF16))
        den = s_sum + w_inter * qn
        o_ref[b, :, h * MD:(h + 1) * MD] = num / jnp.maximum(jnp.abs(den), jnp.exp(-m_t))

        b_last = bc[MCH - 1:MCH, :] if dirn == 0 else bc[0:1, :]
        log_w = b_last - bc + lic
        m_new = jnp.maximum(b_last + m_row, jnp.max(log_w, axis=0, keepdims=True))
        decay = jnp.exp(b_last + m_row - m_new)
        kw = kh.astype(F32) * jnp.exp(log_w - m_new)
        ct_sc[st] = decay * ct_sc[st] + _dot_tn(kw.astype(BF16), vh)
        n_sc[st] = jnp.broadcast_to(decay * n_sc[st][:1, :] + jnp.sum(kw, axis=0, keepdims=True), (8, MD))
        m_sc[st] = jnp.broadcast_to(m_new, (8, 128))


def _mlstm_kernel(qf_ref, kf_ref, vf_ref, grf_ref, qb_ref, kb_ref, vb_ref, grb_ref,
                  of_ref, ob_ref, ct_sc, n_sc, m_sc, *, bsz):
    @pl.when(pl.program_id(0) == 0)
    def _():
        ct_sc[...] = jnp.zeros_like(ct_sc)
        n_sc[...] = jnp.zeros_like(n_sc)
        m_sc[...] = jnp.zeros_like(m_sc)

    for b in range(bsz):
        _mlstm_chunk(0, qf_ref[b], kf_ref[b], vf_ref[b].astype(BF16), grf_ref[b, 0],
                     of_ref, b, ct_sc, n_sc, m_sc, b * MH)
        _mlstm_chunk(1, qb_ref[b], kb_ref[b], vb_ref[b].astype(BF16), grb_ref[b, 0],
                     ob_ref, b, ct_sc, n_sc, m_sc, (bsz + b) * MH)


def _mlstm_call(qc, kc, proj, g_row, nctx):
    bsz, s, _ = qc.shape
    nch = s // MCH
    nctx_ch = nctx // MCH
    cv = C_MV // MW

    def cb(st):
        return _mlstm_bwd_chunk(st, nctx_ch, nch)

    nst = 2 * bsz * MH
    return pl.pallas_call(
        functools.partial(_mlstm_kernel, bsz=bsz),
        grid=(nch,),
        in_specs=[
            pl.BlockSpec((bsz, MCH, MW), lambda st: (0, st, 0)),
            pl.BlockSpec((bsz, MCH, MW), lambda st: (0, st, 0)),
            pl.BlockSpec((bsz, MCH, MW), lambda st: (0, st, cv)),
            pl.BlockSpec((bsz, 1, 8, MCH), lambda st: (0, 0, 0, st)),
            pl.BlockSpec((bsz, MCH, MW), lambda st: (0, cb(st), 0)),
            pl.BlockSpec((bsz, MCH, MW), lambda st: (0, cb(st), 0)),
            pl.BlockSpec((bsz, MCH, MW), lambda st: (0, cb(st), cv)),
            pl.BlockSpec((bsz, 1, 8, MCH), lambda st: (0, 1, 0, cb(st))),
        ],
        out_specs=[
            pl.BlockSpec((bsz, MCH, MW), lambda st: (0, st, 0)),
            pl.BlockSpec((bsz, MCH, MW), lambda st: (0, cb(st), 0)),
        ],
        out_shape=[
            jax.ShapeDtypeStruct((bsz, s, MW), F32),
            jax.ShapeDtypeStruct((bsz, s, MW), F32),
        ],
        scratch_shapes=[
            pltpu.VMEM((nst, MD, MD), F32),
            pltpu.VMEM((nst, 8, MD), F32),
            pltpu.VMEM((nst, 8, 128), F32),
        ],
        compiler_params=_cparams("arbitrary"),
        name="mlstm_scan",
    )(qc, kc, proj, g_row, qc, kc, proj, g_row)


def _mlstm_out_kernel(hf_ref, hb_ref, o_ref, z_ref, nw_ref, y_ref):
    hs = hf_ref[0] + hb_ref[0]
    og = o_ref[0]
    zg = z_ref[0]
    nw = nw_ref[...]
    for h in range(MH):
        sl = slice(h * MD, (h + 1) * MD)
        hn = _rms(hs[:, sl], nw[:, sl])
        y_ref[0, :, sl] = (jax.nn.sigmoid(og[:, sl]) * hn * _silu(zg[:, sl])).astype(BF16)


def _mlstm_out_call(h_f, h_b, proj, norm_w):
    bsz, s, _ = h_f.shape
    tr = ROW_TILE
    return pl.pallas_call(
        _mlstm_out_kernel,
        grid=(bsz, s // tr),
        in_specs=[
            pl.BlockSpec((1, tr, MW), lambda b, i: (b, i, 0)),
            pl.BlockSpec((1, tr, MW), lambda b, i: (b, i, 0)),
            pl.BlockSpec((1, tr, MW), lambda b, i: (b, i, C_MO // MW)),
            pl.BlockSpec((1, tr, MW), lambda b, i: (b, i, C_MZ // MW)),
            pl.BlockSpec((1, MW), lambda b, i: (0, 0)),
        ],
        out_specs=pl.BlockSpec((1, tr, MW), lambda b, i: (b, i, 0)),
        out_shape=jax.ShapeDtypeStruct((bsz, s, MW), BF16),
        compiler_params=_cparams("parallel", "parallel"),
        name="mlstm_out",
    )(h_f, h_b, proj, proj, norm_w)


def _mix(ya_ref, yb_ref, yc_ref, wa_ref, wb_ref, wc_ref):
    return _dot(ya_ref[0], wa_ref[...]) + _dot(yb_ref[0], wb_ref[...]) + _dot(yc_ref[0], wc_ref[...])


def _outproj_mid_kernel(ya_ref, yb_ref, yc_ref, wa_ref, wb_ref, wc_ref, x_ref, ctx_ref, g_ref,
                        nw_ref, sh_ref, sc_ref, h_ref, xn_ref, *, nct):
    i = pl.program_id(1)
    upd = g_ref[0] * _mix(ya_ref, yb_ref, yc_ref, wa_ref, wb_ref, wc_ref)

    def emit(res):
        h = res + upd
        h_ref[0] = h
        xn_ref[0] = (_rms(h, nw_ref[0]) * (1.0 + sc_ref[0]) + sh_ref[0]).astype(BF16)

    @pl.when(i < nct)
    def _():
        emit(ctx_ref[0])

    @pl.when(i >= nct)
    def _():
        emit(x_ref[0])


def _outproj_mid_call(ya, yb, yc, w_out, x, ctx, mod, nw, layer):
    bsz, s, _ = ya.shape
    d = x.shape[-1]
    nctx = ctx.shape[1]
    tr = ROW_TILE
    nct = nctx // tr
    wa, wb, wc = w_out[:SSM_W], w_out[SSM_W:SSM_W + AW], w_out[SSM_W + AW:]
    nxt = layer + 1
    return pl.pallas_call(
        functools.partial(_outproj_mid_kernel, nct=nct),
        grid=(bsz, s // tr),
        in_specs=[
            pl.BlockSpec((1, tr, SSM_W), lambda b, i: (b, i, 0)),
            pl.BlockSpec((1, tr, AW), lambda b, i: (b, i, 0)),
            pl.BlockSpec((1, tr, MW), lambda b, i: (b, i, 0)),
            pl.BlockSpec((SSM_W, d), lambda b, i: (0, 0)),
            pl.BlockSpec((AW, d), lambda b, i: (0, 0)),
            pl.BlockSpec((MW, d), lambda b, i: (0, 0)),
            pl.BlockSpec((1, tr, d), lambda b, i: (b, jnp.maximum(i - nct, 0), 0)),
            pl.BlockSpec((1, tr, d), lambda b, i: (b, jnp.minimum(i, nct - 1), 0)),
            _mod_spec(d, layer, 2, nct, bsz),
            pl.BlockSpec((1, 1, d), lambda b, i: (nxt, 0, 0)),
            _mod_spec(d, nxt, 0, nct, bsz),
            _mod_spec(d, nxt, 1, nct, bsz),
        ],
        out_specs=[
            pl.BlockSpec((1, tr, d), lambda b, i: (b, i, 0)),
            pl.BlockSpec((1, tr, d), lambda b, i: (b, i, 0)),
        ],
        out_shape=[
            jax.ShapeDtypeStruct((bsz, s, d), F32),
            jax.ShapeDtypeStruct((bsz, s, d), BF16),
        ],
        compiler_params=_cparams("parallel", "parallel"),
        name="out_proj_mid",
    )(ya, yb, yc, wa, wb, wc, x, ctx, mod, nw, mod, mod)


def _outproj_last_kernel(ya_ref, yb_ref, yc_ref, wa_ref, wb_ref, wc_ref, h_ref, g_ref, nw_ref, o_ref):
    h = h_ref[0] + g_ref[0] * _mix(ya_ref, yb_ref, yc_ref, wa_ref, wb_ref, wc_ref)
    o_ref[0] = _rms(h, nw_ref[...])


def _outproj_last_call(ya, yb, yc, w_out, h_prev, mod, final_w, layer, nctx):
    bsz, s, d = h_prev.shape
    tr = ROW_TILE
    nct = nctx // tr
    n_lat = s - nctx
    wa, wb, wc = w_out[:SSM_W], w_out[SSM_W:SSM_W + AW], w_out[SSM_W + AW:]
    return pl.pallas_call(
        _outproj_last_kernel,
        grid=(bsz, n_lat // tr),
        in_specs=[
            pl.BlockSpec((1, tr, SSM_W), lambda b, i: (b, i + nct, 0)),
            pl.BlockSpec((1, tr, AW), lambda b, i: (b, i + nct, 0)),
            pl.BlockSpec((1, tr, MW), lambda b, i: (b, i + nct, 0)),
            pl.BlockSpec((SSM_W, d), lambda b, i: (0, 0)),
            pl.BlockSpec((AW, d), lambda b, i: (0, 0)),
            pl.BlockSpec((MW, d), lambda b, i: (0, 0)),
            pl.BlockSpec((1, tr, d), lambda b, i: (b, i + nct, 0)),
            pl.BlockSpec((1, 1, d), lambda b, i: (layer * 8 + b, 0, 2)),
            pl.BlockSpec((1, d), lambda b, i: (0, 0)),
        ],
        out_specs=pl.BlockSpec((1, tr, d), lambda b, i: (b, i, 0)),
        out_shape=jax.ShapeDtypeStruct((bsz, n_lat, d), F32),
        compiler_params=_cparams("parallel", "parallel"),
        name="out_proj_last",
    )(ya, yb, yc, wa, wb, wc, h_prev, mod, final_w)


def _reorder_w_in(w):
    g0 = C_MO + MW
    return jnp.concatenate([w[:, :g0], w[:, g0 + 4 * MH:]], axis=1), w[:, g0:g0 + 4 * MH]


def kernel(x, c, ctx, c_ctx, norm_w, ada_w, ada_b, w_in, mlstm_gate_b, ssm_a_re, ssm_a_im, ssm_log_dt,
           ssm_b_re, ssm_b_im, ssm_c_re, ssm_c_im, ssm_d, ssm_w_glu, attn_q_norm, attn_k_norm,
           mlstm_conv_w, mlstm_conv_b, mlstm_norm_w, w_out, final_norm_w):
    bsz, n_lat, d = x.shape
    nctx = ctx.shape[1]
    s = nctx + n_lat
    depth = norm_w.shape[0]
    assert bsz < 8 and nctx % ROW_TILE == 0 and n_lat % ROW_TILE == 0 and depth == 2

    c_rows = jnp.zeros((8, d), F32).at[:bsz].set(c).at[bsz].set(c_ctx)
    mod = _ada_call(c_rows, ada_w, ada_b).reshape(depth * 8, 1, 3 * d)
    cos2, sin2 = _rope_tables(n_lat, nctx)

    norm_w3 = norm_w.astype(F32).reshape(depth, 1, d)
    xn = _prenorm_call(x, ctx, norm_w3, mod, 0)
    h_prev = None
    out = None
    for layer in range(depth):
        w_main, w_g = _reorder_w_in(w_in[layer])
        proj = _inproj_call(xn.reshape(bsz * s, d), w_main.astype(BF16)).reshape(bsz, s, PROJ_W)

        w_gr = w_g.reshape(d, 2, 2 * MH).transpose(1, 2, 0).astype(BF16)
        gb = mlstm_gate_b[layer].astype(F32).reshape(2, 2 * MH)
        b_gr = jnp.broadcast_to(gb[:, :, None], (2, 2 * MH, 128))
        g_row = _gates_call(xn, w_gr, b_gr)

        ya = _s5_branch(proj, dict(a_re=ssm_a_re[layer], a_im=ssm_a_im[layer], log_dt=ssm_log_dt[layer],
                                   b_re=ssm_b_re[layer], b_im=ssm_b_im[layer], c_re=ssm_c_re[layer],
                                   c_im=ssm_c_im[layer], d=ssm_d[layer], w_glu=ssm_w_glu[layer]), nctx)

        qs, ks, vs = _attn_prep_call(proj, cos2, sin2, attn_q_norm[layer].reshape(1, HD).astype(F32),
                                     attn_k_norm[layer].reshape(1, HD).astype(F32))
        yb = _attn_call(qs, ks, vs, proj, nctx)

        conv_w = jnp.zeros((8, 2 * MW), F32).at[:3].set(mlstm_conv_w[layer].astype(F32))
        qc, kc = _mlstm_prep_call(proj, conv_w, mlstm_conv_b[layer].astype(F32).reshape(1, 2 * MW), nctx)
        h_f, h_b = _mlstm_call(qc, kc, proj, g_row, nctx)
        yc = _mlstm_out_call(h_f, h_b, proj, mlstm_norm_w[layer].astype(F32).reshape(1, MW))

        w_o = w_out[layer].astype(BF16)
        if layer < depth - 1:
            h_prev, xn = _outproj_mid_call(ya, yb, yc, w_o, x, ctx, mod, norm_w3, layer)
        else:
            out = _outproj_last_call(ya, yb, yc, w_o, h_prev, mod, final_norm_w.reshape(1, d), layer, nctx)
    return out
```

```python
import functools
import math

import jax
import jax.numpy as jnp
from jax import lax
from jax.experimental import pallas as pl
from jax.experimental.pallas import tpu as pltpu

F32 = jnp.float32
BF16 = jnp.bfloat16
EPS = 1e-6

SSM_W = 512
SSM_P = 16
SSM_G = 32
SSM_N = 64
S5_T = 16
S5_SLAB_G = 8
S5_SLABS = SSM_G // S5_SLAB_G
S5_ST = S5_SLAB_G * SSM_N
AH = 8
AKV = 2
AGRP = AH // AKV
HD = 128
AW = AH * HD
KVW = AKV * HD
GRID_W = 64
ROPE_THETA = 10000.0
MH = 4
MD = 128
MW = MH * MD
MCH = 128
C_U, C_Z, C_AQ, C_AK, C_AV, C_AZ, C_MQ, C_MK, C_MV, C_MO, C_MZ = (
    0, 512, 1024, 2048, 2304, 2560, 3584, 4096, 4608, 5120, 5632)
PROJ_W = 6144
MIX_W = SSM_W + AW + MW

ROW_TILE = 256
VMEM_LIMIT = 48 * 1024 * 1024


def _cparams(*sem):
    return pltpu.CompilerParams(dimension_semantics=sem, vmem_limit_bytes=VMEM_LIMIT)


def _dot(a, b):
    return jnp.dot(a, b, preferred_element_type=F32)


def _dot_nt(a, b):
    return lax.dot_general(a, b, (((1,), (1,)), ((), ())), preferred_element_type=F32)


def _dot_tn(a, b):
    return lax.dot_general(a, b, (((0,), (0,)), ((), ())), preferred_element_type=F32)


def _split3(x):
    hi = x.astype(BF16)
    r1 = x - hi.astype(F32)
    mid = r1.astype(BF16)
    lo = (r1 - mid.astype(F32)).astype(BF16)
    return hi, mid, lo


def _silu(x):
    return x * jax.nn.sigmoid(x)


def _log_sigmoid(x):
    return jnp.minimum(x, 0.0) - jnp.log1p(jnp.exp(-jnp.abs(x)))


def _rms(x, w):
    return x * lax.rsqrt(jnp.mean(x * x, axis=-1, keepdims=True) + EPS) * w


def _pick_tile(n, cap, mult):
    best = mult
    for t in range(mult, min(n, cap) + 1, mult):
        if n % t == 0:
            best = t
    return best


def _ada_kernel(c_ref, w_ref, b_ref, o_ref):
    a = _silu(c_ref[...])
    w = w_ref[0]
    a_hi = a.astype(BF16)
    a_lo = (a - a_hi.astype(F32)).astype(BF16)
    w_hi = w.astype(BF16)
    w_lo = (w - w_hi.astype(F32)).astype(BF16)
    o_ref[0] = _dot(a_hi, w_hi) + _dot(a_hi, w_lo) + _dot(a_lo, w_hi) + b_ref[0]


def _ada_call(c_rows, ada_w, ada_b):
    depth, d, n3 = ada_w.shape
    tn = 512
    return pl.pallas_call(
        _ada_kernel,
        grid=(depth, n3 // tn),
        in_specs=[
            pl.BlockSpec((8, d), lambda l, j: (0, 0)),
            pl.BlockSpec((1, d, tn), lambda l, j: (l, 0, j)),
            pl.BlockSpec((1, 1, tn), lambda l, j: (l, 0, j)),
        ],
        out_specs=pl.BlockSpec((1, 8, tn), lambda l, j: (l, 0, j)),
        out_shape=jax.ShapeDtypeStruct((depth, 8, n3), F32),
        compiler_params=_cparams("parallel", "parallel"),
        name="ada_mod",
    )(c_rows, ada_w, ada_b.reshape(depth, 1, n3))


def _prenorm_kernel(x_ref, ctx_ref, nw_ref, sh_ref, sc_ref, wgr_ref, bgr_ref, o_ref, gr_ref, *, nct):
    h = jnp.where(pl.program_id(1) < nct, ctx_ref[0], x_ref[0])
    xn = (_rms(h, nw_ref[0]) * (1.0 + sc_ref[0]) + sh_ref[0]).astype(BF16)
    o_ref[0] = xn
    for dirn in range(2):
        gr_ref[0, dirn] = _dot_nt(wgr_ref[dirn], xn) + bgr_ref[dirn][:, :1]


def _mod_spec(d, layer, part, nct, nb):
    def idx(b, i):
        return (layer * 8 + jnp.where(i < nct, nb, b), 0, part)
    return pl.BlockSpec((1, 1, d), idx)


def _prenorm_call(x, ctx, nw, mod, w_gr, b_gr, layer):
    bsz, n, d = x.shape
    nctx = ctx.shape[1]
    tr = ROW_TILE
    nct = nctx // tr
    s = nctx + n
    return pl.pallas_call(
        functools.partial(_prenorm_kernel, nct=nct),
        grid=(bsz, s // tr),
        in_specs=[
            pl.BlockSpec((1, tr, d), lambda b, i: (b, jnp.maximum(i - nct, 0), 0)),
            pl.BlockSpec((1, tr, d), lambda b, i: (b, jnp.minimum(i, nct - 1), 0)),
            pl.BlockSpec((1, 1, d), lambda b, i: (layer, 0, 0)),
            _mod_spec(d, layer, 0, nct, bsz),
            _mod_spec(d, layer, 1, nct, bsz),
            pl.BlockSpec((2, 8, d), lambda b, i: (0, 0, 0)),
            pl.BlockSpec((2, 8, 128), lambda b, i: (0, 0, 0)),
        ],
        out_specs=[
            pl.BlockSpec((1, tr, d), lambda b, i: (b, i, 0)),
            pl.BlockSpec((1, 2, 8, tr), lambda b, i: (b, 0, 0, i)),
        ],
        out_shape=[
            jax.ShapeDtypeStruct((bsz, s, d), BF16),
            jax.ShapeDtypeStruct((bsz, 2, 8, s), F32),
        ],
        compiler_params=_cparams("parallel", "parallel"),
        name="prenorm",
    )(x, ctx, nw, mod, mod, w_gr, b_gr)


def _matmul_kernel(x_ref, w_ref, o_ref):
    o_ref[...] = _dot(x_ref[...], w_ref[...])


def _inproj_call(xn2d, w):
    m, d = xn2d.shape
    n = w.shape[1]
    tm = _pick_tile(m, 1056, 16)
    tn = 1024
    return pl.pallas_call(
        _matmul_kernel,
        grid=(n // tn, m // tm),
        in_specs=[
            pl.BlockSpec((tm, d), lambda j, i: (i, 0)),
            pl.BlockSpec((d, tn), lambda j, i: (0, j)),
        ],
        out_specs=pl.BlockSpec((tm, tn), lambda j, i: (i, j)),
        out_shape=jax.ShapeDtypeStruct((m, n), F32),
        compiler_params=_cparams("parallel", "parallel"),
        name="in_proj",
    )(xn2d, w)


def _s5_weights(a_re, a_im, log_dt, b_re, b_im, c_re, c_im):
    t_ = S5_T
    a_re = a_re.astype(F32)
    a_im = a_im.astype(F32)
    dt = jnp.exp(log_dt.astype(F32))[..., None]
    mag = jnp.exp(a_re * dt)
    lam_re = mag * jnp.cos(a_im * dt)
    lam_im = mag * jnp.sin(a_im * dt)
    inv_abs2 = 1.0 / (a_re * a_re + a_im * a_im)
    num_re, num_im = lam_re - 1.0, lam_im
    f_re = (num_re * a_re + num_im * a_im) * inv_abs2
    f_im = (num_im * a_re - num_re * a_im) * inv_abs2
    b_re = b_re.astype(F32)[None]
    b_im = b_im.astype(F32)[None]
    bb_re = f_re[:, :, None, :] * b_re - f_im[:, :, None, :] * b_im
    bb_im = f_re[:, :, None, :] * b_im + f_im[:, :, None, :] * b_re
    c_re = c_re.astype(F32)
    c_im = c_im.astype(F32)

    pr = [jnp.ones_like(lam_re)]
    pi = [jnp.zeros_like(lam_im)]
    for _ in range(t_):
        pr.append(pr[-1] * lam_re - pi[-1] * lam_im)
        pi.append(pr[-2] * lam_im + pi[-1] * lam_re)

    def powers(exps_f, exps_b):
        re = jnp.stack([jnp.stack([pr[e][0] for e in exps_f]), jnp.stack([pr[e][1] for e in exps_b])])
        im = jnp.stack([jnp.stack([pi[e][0] for e in exps_f]), jnp.stack([pi[e][1] for e in exps_b])])
        return re[:, :, :, None, :], im[:, :, :, None, :]

    def slab_rows(w):
        w = w.reshape(2, t_, S5_SLABS, S5_SLAB_G, SSM_P, 2 * SSM_N).transpose(0, 2, 1, 3, 4, 5)
        return w.reshape(2, S5_SLABS, t_ * 128, 2 * SSM_N).astype(BF16)

    steps = list(range(t_))
    p_re, p_im = powers([t_ - 1 - t for t in steps], steps)
    win = slab_rows(jnp.concatenate([bb_re[:, None] * p_re - bb_im[:, None] * p_im,
                                     bb_re[:, None] * p_im + bb_im[:, None] * p_re], axis=-1))
    p_re, p_im = powers([t + 1 for t in steps], [t_ - t for t in steps])
    wout = slab_rows(jnp.concatenate([c_re[:, None] * p_re - c_im[:, None] * p_im,
                                      -(c_re[:, None] * p_im + c_im[:, None] * p_re)], axis=-1))

    p_re, p_im = powers(steps, steps)
    kr = c_re[:, None] * p_re - c_im[:, None] * p_im
    ki = c_re[:, None] * p_im + c_im[:, None] * p_re
    kk = (jnp.einsum('dgqn,dtgpn->dgqtp', bb_re, kr, precision=lax.Precision.HIGHEST)
          - jnp.einsum('dgqn,dtgpn->dgqtp', bb_im, ki, precision=lax.Precision.HIGHEST))
    k_f = kk[0].reshape(S5_SLABS, 128, t_ * SSM_P)
    k_b = kk[1][:, :, ::-1, :].reshape(S5_SLABS, 128, t_ * SSM_P)

    def lam_row(v):
        return v.reshape(2, S5_SLABS, 1, S5_ST)

    lam_t = jnp.concatenate([lam_row(pr[t_]), lam_row(pi[t_])], axis=-1)
    return win, wout, k_f, k_b, lam_t


def _iota(shape, dim):
    return lax.broadcasted_iota(jnp.int32, shape, dim)


def _s5_core_kernel(u_ref, win_ref, wout_ref, kf_ref, kb_ref, lam_ref, y_ref, lhs_sc, w_sc, s_sc, acc_sc,
                    *, nch, nctx_ch):
    t_ = S5_T
    n2 = 2 * SSM_N
    st2 = 2 * S5_ST
    lg_p = SSM_P.bit_length() - 1
    lg_n = SSM_N.bit_length() - 1
    lg_st = S5_ST.bit_length() - 1

    for t in range(t_):
        lhs_sc[:, t * 128:(t + 1) * 128] = u_ref[0, pl.ds(t, nch, stride=t_), :].astype(BF16)

    k1, c1 = _iota((n2, st2), 0), _iota((n2, st2), 1)
    e_in = jnp.where(((k1 >> lg_n) == (c1 >> lg_st)) & ((k1 & (SSM_N - 1)) == (c1 & (SSM_N - 1))),
                     1.0, 0.0).astype(BF16)
    r1, c1b = _iota((128, st2), 0), _iota((128, st2), 1)
    m_in = (r1 >> lg_p) == ((c1b & (S5_ST - 1)) >> lg_n)

    def expand_state_table(tab_ref, d):
        for t in range(t_):
            rows = slice(t * 128, (t + 1) * 128)
            w_sc[rows, :st2] = jnp.where(m_in, _dot(tab_ref[d, 0, rows, :], e_in), 0.0).astype(BF16)

    for d in range(2):
        expand_state_table(win_ref, d)
        s_sc[d] = _dot(lhs_sc[...], w_sc[:, :st2])

    lam_f = lam_ref[0, 0]
    lam_b = lam_ref[1, 0]
    lrf, lif = lam_f[:, :S5_ST], lam_f[:, S5_ST:]
    lrb, lib = lam_b[:, :S5_ST], lam_b[:, S5_ST:]

    def step(k, carry):
        hrf, hif, hrb, hib = carry
        cb = jnp.where(k < nctx_ch, nctx_ch - 1 - k, nch - 1 - (k - nctx_ch))
        sf = s_sc[0, pl.ds(k, 1), :]
        sb = s_sc[1, pl.ds(cb, 1), :]
        s_sc[0, pl.ds(k, 1), :] = jnp.concatenate([hrf, hif], axis=-1)
        s_sc[1, pl.ds(cb, 1), :] = jnp.concatenate([hrb, hib], axis=-1)
        return (lrf * hrf - lif * hif + sf[:, :S5_ST], lrf * hif + lif * hrf + sf[:, S5_ST:],
                lrb * hrb - lib * hib + sb[:, :S5_ST], lrb * hib + lib * hrb + sb[:, S5_ST:])

    zero = jnp.zeros((1, S5_ST), F32)
    lax.fori_loop(0, nch, step, (zero, zero, zero, zero))

    k2, c2 = _iota((t_ * SSM_P, t_ * 128), 0), _iota((t_ * SSM_P, t_ * 128), 1)
    e_m = jnp.where(((k2 >> lg_p) == (c2 >> 7)) & ((k2 & (SSM_P - 1)) == (c2 & (SSM_P - 1))),
                    1.0, 0.0).astype(BF16)
    r2, c2b = _iota((128, t_ * 128), 0), _iota((128, t_ * 128), 1)
    m_m = (r2 >> lg_p) == ((c2b & 127) >> lg_p)
    k_f = kf_ref[0]
    k_b = kb_ref[0]
    lag_col = _iota((128, t_ * SSM_P), 1)
    for t in range(t_):
        rows = slice(t * 128, (t + 1) * 128)
        fwd = k_f if t == 0 else pltpu.roll(k_f, t * SSM_P, 1)
        bwd = k_b if t == t_ - 1 else pltpu.roll(k_b, (t + 1) * SSM_P, 1)
        mc = (jnp.where(lag_col >= t * SSM_P, fwd, 0.0)
              + jnp.where(lag_col < (t + 1) * SSM_P, bwd, 0.0)).astype(BF16)
        w_sc[rows, :] = jnp.where(m_m, _dot(mc, e_m), 0.0).astype(BF16)
    acc_sc[...] = _dot(lhs_sc[...], w_sc[...])

    for d in range(2):
        expand_state_table(wout_ref, d)
        acc_sc[...] += _dot_nt(s_sc[d].astype(BF16), w_sc[:, :st2])

    for t in range(t_):
        y_ref[0, pl.ds(t, nch, stride=t_), :] = acc_sc[:, t * 128:(t + 1) * 128]


def _s5_core_call(proj, win, wout, k_f, k_b, lam_t, nctx):
    bsz, s, _ = proj.shape
    t_ = S5_T
    nch = s // t_
    kdim = t_ * 128
    n2 = 2 * SSM_N
    st2 = 2 * S5_ST
    cu = C_U // 128
    return pl.pallas_call(
        functools.partial(_s5_core_kernel, nch=nch, nctx_ch=nctx // t_),
        grid=(S5_SLABS, bsz),
        in_specs=[
            pl.BlockSpec((1, s, 128), lambda sl, b: (b, 0, cu + sl)),
            pl.BlockSpec((2, 1, kdim, n2), lambda sl, b: (0, sl, 0, 0)),
            pl.BlockSpec((2, 1, kdim, n2), lambda sl, b: (0, sl, 0, 0)),
            pl.BlockSpec((1, 128, t_ * SSM_P), lambda sl, b: (sl, 0, 0)),
            pl.BlockSpec((1, 128, t_ * SSM_P), lambda sl, b: (sl, 0, 0)),
            pl.BlockSpec((2, 1, 1, st2), lambda sl, b: (0, sl, 0, 0)),
        ],
        out_specs=pl.BlockSpec((1, s, 128), lambda sl, b: (b, 0, sl)),
        out_shape=jax.ShapeDtypeStruct((bsz, s, SSM_W), F32),
        scratch_shapes=[
            pltpu.VMEM((nch, kdim), BF16),
            pltpu.VMEM((kdim, kdim), BF16),
            pltpu.VMEM((2, nch, st2), F32),
            pltpu.VMEM((nch, kdim), F32),
        ],
        compiler_params=pltpu.CompilerParams(dimension_semantics=("parallel", "parallel"),
                                             vmem_limit_bytes=56 * 1024 * 1024),
        name="s5_core",
    )(proj, win, wout, k_f, k_b, lam_t)


def _s5_branch(proj, p, nctx):
    win, wout, k_f, k_b, lam_t = _s5_weights(p['a_re'], p['a_im'], p['log_dt'], p['b_re'], p['b_im'],
                                             p['c_re'], p['c_im'])
    return _s5_core_call(proj, win, wout, k_f, k_b, lam_t, nctx)


def _rope_tables(n_lat, nctx):
    rows = n_lat // GRID_W
    row = jnp.broadcast_to(jnp.arange(rows, dtype=F32)[:, None], (rows, GRID_W)).reshape(-1)
    col = jnp.broadcast_to(jnp.arange(GRID_W, dtype=F32)[None, :], (rows, GRID_W)).reshape(-1)
    n_freq = HD // 4
    inv = ROPE_THETA ** (-jnp.arange(n_freq, dtype=F32) / n_freq)
    ang = jnp.concatenate([row[:, None] * inv, col[:, None] * inv], axis=-1)
    cs, sn = jnp.cos(ang), jnp.sin(ang)
    cos2 = jnp.concatenate([cs, cs], axis=-1)
    sin2 = jnp.concatenate([-sn, sn], axis=-1)
    cos_all = jnp.concatenate([jnp.ones((nctx, HD), F32), cos2], axis=0)
    sin_all = jnp.concatenate([jnp.zeros((nctx, HD), F32), sin2], axis=0)
    return cos_all, sin_all, cos2, sin2


def _norm_rope(x, w, cs, sn):
    y = _rms(x, w)
    return y * cs + pltpu.roll(y, HD // 2, 1) * sn


def _attn_kv_prep_kernel(k_ref, v_ref, cos_ref, sin_ref, kw_ref, ko_ref, vo_ref):
    cs = cos_ref[...]
    sn = sin_ref[...]
    k = k_ref[0]
    for h in range(AKV):
        ko_ref[0, :, h * HD:(h + 1) * HD] = _norm_rope(k[:, h * HD:(h + 1) * HD], kw_ref[...], cs, sn).astype(BF16)
    vo_ref[0] = v_ref[0].astype(BF16)


def _attn_kv_prep_call(proj, cos2, sin2, kw):
    bsz, s, _ = proj.shape
    tr = ROW_TILE
    return pl.pallas_call(
        _attn_kv_prep_kernel,
        grid=(bsz, s // tr),
        in_specs=[
            pl.BlockSpec((1, tr, KVW), lambda b, i: (b, i, C_AK // KVW)),
            pl.BlockSpec((1, tr, KVW), lambda b, i: (b, i, C_AV // KVW)),
            pl.BlockSpec((tr, HD), lambda b, i: (i, 0)),
            pl.BlockSpec((tr, HD), lambda b, i: (i, 0)),
            pl.BlockSpec((1, HD), lambda b, i: (0, 0)),
        ],
        out_specs=[
            pl.BlockSpec((1, tr, KVW), lambda b, i: (b, i, 0)),
            pl.BlockSpec((1, tr, KVW), lambda b, i: (b, i, 0)),
        ],
        out_shape=[
            jax.ShapeDtypeStruct((bsz, s, KVW), BF16),
            jax.ShapeDtypeStruct((bsz, s, KVW), BF16),
        ],
        compiler_params=_cparams("parallel", "parallel"),
        name="attn_kv_prep",
    )(proj, proj, cos2, sin2, kw)


Q_SCALE = HD ** -0.5 * math.log2(math.e)


def _attn_finish(acc, l, z_refs, o_ref, tq):
    o = acc / l
    ntile = len(z_refs)
    for h in range(AGRP):
        for t, z_ref in enumerate(z_refs):
            r0 = (h * ntile + t) * tq
            z = z_ref[0][:, h * HD:(h + 1) * HD]
            o_ref[0, t * tq:(t + 1) * tq, h * HD:(h + 1) * HD] = (o[r0:r0 + tq] * _silu(z)).astype(BF16)


def _attn_ctx_kernel(q_ref, z_ref, qw_ref, k_ref, v_ref, o_ref, *, tq):
    q = q_ref[0]
    q4 = jnp.concatenate([(_rms(q[:, h * HD:(h + 1) * HD], qw_ref[...]) * Q_SCALE).astype(BF16)
                          for h in range(AGRP)], axis=0)
    s = _dot_nt(q4, k_ref[0])
    m = jnp.max(s, axis=-1, keepdims=True)
    p = jnp.exp2(s - m)
    l = jnp.sum(p, axis=-1, keepdims=True)
    _attn_finish(_dot(p.astype(BF16), v_ref[0]), l, [z_ref], o_ref, tq)


def _attn_ctx_call(proj, ks, vs, qw, nctx):
    bsz = proj.shape[0]
    tq = ROW_TILE
    zw = AGRP * HD
    return pl.pallas_call(
        functools.partial(_attn_ctx_kernel, tq=tq),
        grid=(bsz, AKV, nctx // tq),
        in_specs=[
            pl.BlockSpec((1, tq, zw), lambda b, g, i: (b, i, C_AQ // zw + g)),
            pl.BlockSpec((1, tq, zw), lambda b, g, i: (b, i, C_AZ // zw + g)),
            pl.BlockSpec((1, HD), lambda b, g, i: (0, 0)),
            pl.BlockSpec((1, nctx, HD), lambda b, g, i: (b, 0, g)),
            pl.BlockSpec((1, nctx, HD), lambda b, g, i: (b, 0, g)),
        ],
        out_specs=pl.BlockSpec((1, tq, zw), lambda b, g, i: (b, i, g)),
        out_shape=jax.ShapeDtypeStruct((bsz, nctx, AW), BF16),
        compiler_params=_cparams("parallel", "parallel", "parallel"),
        name="attention_ctx",
    )(proj, proj, qw, ks, vs)


def _attn_lat_kernel(qa_ref, qb_ref, za_ref, zb_ref, cos_ref, sin_ref, qw_ref, k_ref, v_ref, o_ref, s_sc,
                     *, tq, ck, nck):
    parts = []
    for h in range(AGRP):
        for t, q_ref in enumerate((qa_ref, qb_ref)):
            x = q_ref[0][:, h * HD:(h + 1) * HD]
            r = _norm_rope(x, qw_ref[...], cos_ref[t * tq:(t + 1) * tq, :], sin_ref[t * tq:(t + 1) * tq, :])
            parts.append((r * Q_SCALE).astype(BF16))
    q4 = jnp.concatenate(parts, axis=0)

    rows = q4.shape[0]

    def chunk_start(j):
        return j * ck if isinstance(j, int) else pl.multiple_of(j * ck, ck)

    def scores(j, slot):
        s_sc[slot] = _dot_nt(q4, k_ref[0, pl.ds(chunk_start(j), ck), :])

    def update(j, slot, carry):
        m, l, acc = carry
        off = chunk_start(j)
        s = s_sc[slot]
        m_new = jnp.maximum(m, jnp.max(s, axis=-1, keepdims=True))
        p = jnp.exp2(s - m_new)
        alpha = jnp.exp2(m - m_new)
        l = alpha * l + jnp.sum(p, axis=-1, keepdims=True)
        acc = alpha * acc + _dot(p.astype(BF16), v_ref[0, pl.ds(off, ck), :])
        return m_new, l, acc

    def pair(i, carry):
        scores(2 * i + 1, 1)
        carry = update(2 * i, 0, carry)
        scores(2 * i + 2, 0)
        return update(2 * i + 1, 1, carry)

    carry = (jnp.full((rows, 1), -jnp.inf, F32), jnp.zeros((rows, 1), F32), jnp.zeros((rows, HD), F32))
    scores(0, 0)
    npair = (nck - 1) // 2
    carry = lax.fori_loop(0, npair, pair, carry)
    if nck % 2 == 0:
        scores(nck - 1, 1)
        carry = update(nck - 2, 0, carry)
        carry = update(nck - 1, 1, carry)
    else:
        carry = update(nck - 1, 0, carry)
    _attn_finish(carry[2], carry[1], [za_ref, zb_ref], o_ref, tq)


def _attn_lat_call(proj, ks, vs, cos_lat, sin_lat, qw, nctx):
    bsz, s, _ = proj.shape
    n_lat = s - nctx
    tq = ROW_TILE
    nct = nctx // tq
    ck = _pick_tile(s, 768, 128)
    nck = s // ck
    zw = AGRP * HD
    cq = C_AQ // zw
    cz = C_AZ // zw
    return pl.pallas_call(
        functools.partial(_attn_lat_kernel, tq=tq, ck=ck, nck=nck),
        grid=(bsz, AKV, n_lat // (2 * tq)),
        in_specs=[
            pl.BlockSpec((1, tq, zw), lambda b, g, i: (b, nct + 2 * i, cq + g)),
            pl.BlockSpec((1, tq, zw), lambda b, g, i: (b, nct + 2 * i + 1, cq + g)),
            pl.BlockSpec((1, tq, zw), lambda b, g, i: (b, nct + 2 * i, cz + g)),
            pl.BlockSpec((1, tq, zw), lambda b, g, i: (b, nct + 2 * i + 1, cz + g)),
            pl.BlockSpec((2 * tq, HD), lambda b, g, i: (i, 0)),
            pl.BlockSpec((2 * tq, HD), lambda b, g, i: (i, 0)),
            pl.BlockSpec((1, HD), lambda b, g, i: (0, 0)),
            pl.BlockSpec((1, s, HD), lambda b, g, i: (b, 0, g)),
            pl.BlockSpec((1, s, HD), lambda b, g, i: (b, 0, g)),
        ],
        out_specs=pl.BlockSpec((1, 2 * tq, zw), lambda b, g, i: (b, i, g)),
        out_shape=jax.ShapeDtypeStruct((bsz, n_lat, AW), BF16),
        scratch_shapes=[pltpu.VMEM((2, 2 * AGRP * tq, ck), F32)],
        compiler_params=_cparams("parallel", "parallel", "parallel"),
        name="attention",
    )(proj, proj, proj, proj, cos_lat, sin_lat, qw, ks, vs)


def _mlstm_prep_kernel(q_ref, k_ref, qp_ref, kp_ref, qn_ref, kn_ref, w_ref, b_ref, qo_ref, ko_ref,
                       *, tr, nct, nt):
    i = pl.program_id(1)
    first = jnp.logical_or(i == 0, i == nct)
    last = jnp.logical_or(i == nct - 1, i == nt - 1)
    rid = lax.broadcasted_iota(jnp.int32, (tr, MW), 0)
    w = w_ref[...]
    bias = b_ref[...]

    def conv(x, prev_row, next_row, off):
        prev_row = jnp.where(first, 0.0, prev_row)
        next_row = jnp.where(last, 0.0, next_row)
        xp = jnp.where(rid == 0, prev_row, pltpu.roll(x, 1, 0))
        xn = jnp.where(rid == tr - 1, next_row, pltpu.roll(x, tr - 1, 0))
        y = (w[0:1, off:off + MW] * xp + w[1:2, off:off + MW] * x + w[2:3, off:off + MW] * xn
             + bias[:, off:off + MW])
        return _silu(y)

    qo_ref[0] = conv(q_ref[0], qp_ref[0, 7:8, :], qn_ref[0, 0:1, :], 0).astype(BF16)
    ko_ref[0] = (conv(k_ref[0], kp_ref[0, 7:8, :], kn_ref[0, 0:1, :], MW) * (MD ** -0.5)).astype(BF16)


def _mlstm_prep_call(proj, conv_w, conv_b, nctx):
    bsz, s, _ = proj.shape
    tr = ROW_TILE
    nt = s // tr
    nct = nctx // tr
    r8 = tr // 8
    n8 = s // 8
    cq = C_MQ // MW
    ck = C_MK // MW

    def prev_map(c):
        return lambda b, i: (b, jnp.maximum(i * r8 - 1, 0), c)

    def next_map(c):
        return lambda b, i: (b, jnp.minimum((i + 1) * r8, n8 - 1), c)

    return pl.pallas_call(
        functools.partial(_mlstm_prep_kernel, tr=tr, nct=nct, nt=nt),
        grid=(bsz, nt),
        in_specs=[
            pl.BlockSpec((1, tr, MW), lambda b, i: (b, i, cq)),
            pl.BlockSpec((1, tr, MW), lambda b, i: (b, i, ck)),
            pl.BlockSpec((1, 8, MW), prev_map(cq)),
            pl.BlockSpec((1, 8, MW), prev_map(ck)),
            pl.BlockSpec((1, 8, MW), next_map(cq)),
            pl.BlockSpec((1, 8, MW), next_map(ck)),
            pl.BlockSpec((8, 2 * MW), lambda b, i: (0, 0)),
            pl.BlockSpec((1, 2 * MW), lambda b, i: (0, 0)),
        ],
        out_specs=[
            pl.BlockSpec((1, tr, MW), lambda b, i: (b, i, 0)),
            pl.BlockSpec((1, tr, MW), lambda b, i: (b, i, 0)),
        ],
        out_shape=[
            jax.ShapeDtypeStruct((bsz, s, MW), BF16),
            jax.ShapeDtypeStruct((bsz, s, MW), BF16),
        ],
        compiler_params=_cparams("parallel", "parallel"),
        name="mlstm_prep",
    )(proj, proj, proj, proj, proj, proj, conv_w, conv_b)


def _mlstm_bwd_chunk(step, nctx_ch, nch):
    return jnp.where(step < nctx_ch, nctx_ch - 1 - step, nch - 1 - (step - nctx_ch))


def _hi_mid(x):
    hi = x.astype(BF16)
    mid = (x - hi.astype(F32)).astype(BF16)
    return jnp.concatenate([hi, mid], axis=1)


def _mlstm_chunk(dirn, q, k, v, gr, o_ref, b, ct_sc, n_sc, m_sc, base):
    row = _iota((MCH, MCH), 0)
    col = _iota((MCH, MCH), 1)
    allowed = (col <= row) if dirn == 0 else (col >= row)
    eye = col == row
    tri = jnp.where(allowed, 1.0, 0.0).astype(BF16)
    lf_r = _log_sigmoid(gr)
    r3 = _split3(lf_r)
    b_r = _dot_nt(r3[0], tri) + _dot_nt(r3[1], tri) + _dot_nt(r3[2], tri)

    a8 = jnp.concatenate([gr[:MH] - b_r[MH:], gr[:MH] - b_r[MH:]], axis=0)
    lane = _iota((8, MCH), 1)
    cm8 = a8
    sh = 1
    while sh < MCH:
        if dirn == 0:
            cm8 = jnp.maximum(cm8, jnp.where(lane >= sh, pltpu.roll(cm8, sh, 1), -jnp.inf))
        else:
            cm8 = jnp.maximum(cm8, jnp.where(lane < MCH - sh, pltpu.roll(cm8, MCH - sh, 1), -jnp.inf))
        sh *= 2

    ones2 = jnp.ones((2 * MCH, MCH), BF16)
    lhs = []
    for h in range(MH):
        lhs += [_hi_mid(jnp.where(allowed, lf_r[MH + h:MH + h + 1, :], 0.0)),
                _hi_mid(jnp.where(eye, cm8[h:h + 1, :], 0.0)),
                _hi_mid(jnp.where(eye, gr[h:h + 1, :], 0.0))]
    rep = _dot(jnp.concatenate(lhs, axis=0), ones2)

    heads = []
    lhs2 = []
    for h in range(MH):
        st = base + h
        bc = rep[(3 * h) * MCH:(3 * h + 1) * MCH]
        cm = rep[(3 * h + 1) * MCH:(3 * h + 2) * MCH]
        lic = rep[(3 * h + 2) * MCH:(3 * h + 3) * MCH]
        br = b_r[MH + h:MH + h + 1, :]
        li_r = gr[h:h + 1, :]
        m_row = m_sc[st][:1, :]
        qh = q[:, h * MD:(h + 1) * MD]
        kh = k[:, h * MD:(h + 1) * MD]

        log_d = jnp.where(allowed, bc - br + li_r, -jnp.inf)
        m_t = bc + jnp.maximum(m_row, cm)
        s = _dot_nt(qh, kh) * jnp.exp(log_d - m_t)
        s_bf = s.astype(BF16)
        lhs2 += [s_bf, (qh.astype(F32) * n_sc[st][:1, :]).astype(BF16)]
        heads.append((st, bc, lic, m_row, m_t, qh, kh, s_bf))

    rep2 = _dot(jnp.concatenate(lhs2, axis=0), jnp.ones((MCH, MCH), BF16))

    for h, (st, bc, lic, m_row, m_t, qh, kh, s_bf) in enumerate(heads):
        vh = v[:, h * MD:(h + 1) * MD]
        s_sum = rep2[(2 * h) * MCH:(2 * h + 1) * MCH]
        qn = rep2[(2 * h + 1) * MCH:(2 * h + 2) * MCH]
        w_inter = jnp.exp(bc + m_row - m_t)
        num = _dot(s_bf, vh) + w_inter * _dot(qh, ct_sc[st].astype(BF16))
        den = s_sum + w_inter * qn
        o_ref[b, :, h * MD:(h + 1) * MD] = num / jnp.maximum(jnp.abs(den), jnp.exp(-m_t))

        b_last = bc[MCH - 1:MCH, :] if dirn == 0 else bc[0:1, :]
        log_w = b_last - bc + lic
        m_new = jnp.maximum(b_last + m_row, jnp.max(log_w, axis=0, keepdims=True))
        decay = jnp.exp(b_last + m_row - m_new)
        kw = kh.astype(F32) * jnp.exp(log_w - m_new)
        ct_sc[st] = decay * ct_sc[st] + _dot_tn(kw.astype(BF16), vh)
        n_sc[st] = jnp.broadcast_to(decay * n_sc[st][:1, :] + jnp.sum(kw, axis=0, keepdims=True), (8, MD))
        m_sc[st] = jnp.broadcast_to(m_new, (8, 128))


def _mlstm_kernel(qf_ref, kf_ref, vf_ref, grf_ref, qb_ref, kb_ref, vb_ref, grb_ref,
                  of_ref, ob_ref, ct_sc, n_sc, m_sc, *, bsz):
    @pl.when(pl.program_id(0) == 0)
    def _():
        ct_sc[...] = jnp.zeros_like(ct_sc)
        n_sc[...] = jnp.zeros_like(n_sc)
        m_sc[...] = jnp.zeros_like(m_sc)

    for b in range(bsz):
        _mlstm_chunk(0, qf_ref[b], kf_ref[b], vf_ref[b].astype(BF16), grf_ref[b, 0],
                     of_ref, b, ct_sc, n_sc, m_sc, b * MH)
        _mlstm_chunk(1, qb_ref[b], kb_ref[b], vb_ref[b].astype(BF16), grb_ref[b, 0],
                     ob_ref, b, ct_sc, n_sc, m_sc, (bsz + b) * MH)


def _mlstm_call(qc, kc, proj, g_row, nctx):
    bsz, s, _ = qc.shape
    nch = s // MCH
    nctx_ch = nctx // MCH
    cv = C_MV // MW

    def cb(st):
        return _mlstm_bwd_chunk(st, nctx_ch, nch)

    nst = 2 * bsz * MH
    return pl.pallas_call(
        functools.partial(_mlstm_kernel, bsz=bsz),
        grid=(nch,),
        in_specs=[
            pl.BlockSpec((bsz, MCH, MW), lambda st: (0, st, 0)),
            pl.BlockSpec((bsz, MCH, MW), lambda st: (0, st, 0)),
            pl.BlockSpec((bsz, MCH, MW), lambda st: (0, st, cv)),
            pl.BlockSpec((bsz, 1, 8, MCH), lambda st: (0, 0, 0, st)),
            pl.BlockSpec((bsz, MCH, MW), lambda st: (0, cb(st), 0)),
            pl.BlockSpec((bsz, MCH, MW), lambda st: (0, cb(st), 0)),
            pl.BlockSpec((bsz, MCH, MW), lambda st: (0, cb(st), cv)),
            pl.BlockSpec((bsz, 1, 8, MCH), lambda st: (0, 1, 0, cb(st))),
        ],
        out_specs=[
            pl.BlockSpec((bsz, MCH, MW), lambda st: (0, st, 0)),
            pl.BlockSpec((bsz, MCH, MW), lambda st: (0, cb(st), 0)),
        ],
        out_shape=[
            jax.ShapeDtypeStruct((bsz, s, MW), F32),
            jax.ShapeDtypeStruct((bsz, s, MW), F32),
        ],
        scratch_shapes=[
            pltpu.VMEM((nst, MD, MD), F32),
            pltpu.VMEM((nst, 8, MD), F32),
            pltpu.VMEM((nst, 8, 128), F32),
        ],
        compiler_params=_cparams("arbitrary"),
        name="mlstm_scan",
    )(qc, kc, proj, g_row, qc, kc, proj, g_row)


def _mixer_specs(tr, row_of):
    def tok(col_block):
        return pl.BlockSpec((1, tr, SSM_W), lambda b, i: (b, row_of(i), col_block))
    const = lambda shape: pl.BlockSpec(shape, lambda b, i: (0,) * len(shape))
    return [tok(0), tok(C_U // SSM_W), tok(C_Z // SSM_W), const((1, SSM_W)), const((SSM_W, SSM_W)),
            tok(0), tok(0), tok(C_MO // MW), tok(C_MZ // MW), const((1, MW))]


def _mixer_outputs(y_ref, u_ref, sz_ref, d_ref, wg_ref, hf_ref, hb_ref, mo_ref, mz_ref, mn_ref):
    g = jax.nn.gelu(y_ref[0] + d_ref[...] * u_ref[0], approximate=True)
    ya = (g * jax.nn.sigmoid(_dot(g.astype(BF16), wg_ref[...])) * _silu(sz_ref[0])).astype(BF16)
    hs = hf_ref[0] + hb_ref[0]
    og, zg, nw = mo_ref[0], mz_ref[0], mn_ref[...]
    yc = []
    for h in range(MH):
        sl = slice(h * MD, (h + 1) * MD)
        yc.append((jax.nn.sigmoid(og[:, sl]) * _rms(hs[:, sl], nw[:, sl]) * _silu(zg[:, sl])).astype(BF16))
    return ya, jnp.concatenate(yc, axis=1)


def _outproj_mid_kernel(*refs, nct):
    mixer, (ybc_ref, ybl_ref, wa_ref, wb_ref, wc_ref, x_ref, ctx_ref, g_ref, nw_ref, sh_ref, sc_ref,
            wgr_ref, bgr_ref, h_ref, xn_ref, gr_ref) = refs[:10], refs[10:]
    is_ctx = pl.program_id(1) < nct
    ya, yc = _mixer_outputs(*mixer)
    yb = jnp.where(is_ctx, ybc_ref[0], ybl_ref[0])
    mix = _dot(ya, wa_ref[...]) + _dot(yb, wb_ref[...]) + _dot(yc, wc_ref[...])
    h = jnp.where(is_ctx, ctx_ref[0], x_ref[0]) + g_ref[0] * mix
    h_ref[0] = h
    xn = (_rms(h, nw_ref[0]) * (1.0 + sc_ref[0]) + sh_ref[0]).astype(BF16)
    xn_ref[0] = xn
    for dirn in range(2):
        gr_ref[0, dirn] = _dot_nt(wgr_ref[dirn], xn) + bgr_ref[dirn][:, :1]


def _outproj_mid_call(yssm, h_f, h_b, yb_ctx, yb_lat, proj, d_row, w_glu, mnorm, w_out, x, ctx, mod, nw,
                      w_gr, b_gr, layer):
    bsz, s, _ = yssm.shape
    d = x.shape[-1]
    nctx = ctx.shape[1]
    tr = ROW_TILE
    nct = nctx // tr
    wa, wb, wc = w_out[:SSM_W], w_out[SSM_W:SSM_W + AW], w_out[SSM_W + AW:]
    nxt = layer + 1
    lat = lambda b, i: (b, jnp.maximum(i - nct, 0), 0)
    cxt = lambda b, i: (b, jnp.minimum(i, nct - 1), 0)
    return pl.pallas_call(
        functools.partial(_outproj_mid_kernel, nct=nct),
        grid=(bsz, s // tr),
        in_specs=_mixer_specs(tr, lambda i: i) + [
            pl.BlockSpec((1, tr, AW), cxt),
            pl.BlockSpec((1, tr, AW), lat),
            pl.BlockSpec((SSM_W, d), lambda b, i: (0, 0)),
            pl.BlockSpec((AW, d), lambda b, i: (0, 0)),
            pl.BlockSpec((MW, d), lambda b, i: (0, 0)),
            pl.BlockSpec((1, tr, d), lat),
            pl.BlockSpec((1, tr, d), cxt),
            _mod_spec(d, layer, 2, nct, bsz),
            pl.BlockSpec((1, 1, d), lambda b, i: (nxt, 0, 0)),
            _mod_spec(d, nxt, 0, nct, bsz),
            _mod_spec(d, nxt, 1, nct, bsz),
            pl.BlockSpec((2, 8, d), lambda b, i: (0, 0, 0)),
            pl.BlockSpec((2, 8, 128), lambda b, i: (0, 0, 0)),
        ],
        out_specs=[
            pl.BlockSpec((1, tr, d), lambda b, i: (b, i, 0)),
            pl.BlockSpec((1, tr, d), lambda b, i: (b, i, 0)),
            pl.BlockSpec((1, 2, 8, tr), lambda b, i: (b, 0, 0, i)),
        ],
        out_shape=[
            jax.ShapeDtypeStruct((bsz, s, d), F32),
            jax.ShapeDtypeStruct((bsz, s, d), BF16),
            jax.ShapeDtypeStruct((bsz, 2, 8, s), F32),
        ],
        compiler_params=_cparams("parallel", "parallel"),
        name="out_proj_mid",
    )(yssm, proj, proj, d_row, w_glu, h_f, h_b, proj, proj, mnorm,
      yb_ctx, yb_lat, wa, wb, wc, x, ctx, mod, nw, mod, mod, w_gr, b_gr)


def _outproj_last_kernel(*refs):
    mixer, (ybl_ref, wa_ref, wb_ref, wc_ref, h_ref, g_ref, nw_ref, o_ref) = refs[:10], refs[10:]
    ya, yc = _mixer_outputs(*mixer)
    mix = _dot(ya, wa_ref[...]) + _dot(ybl_ref[0], wb_ref[...]) + _dot(yc, wc_ref[...])
    o_ref[0] = _rms(h_ref[0] + g_ref[0] * mix, nw_ref[...])


def _outproj_last_call(yssm, h_f, h_b, yb_lat, proj, d_row, w_glu, mnorm, w_out, h_prev, mod, final_w,
                       layer, nctx):
    bsz, s, d = h_prev.shape
    tr = ROW_TILE
    nct = nctx // tr
    n_lat = s - nctx
    wa, wb, wc = w_out[:SSM_W], w_out[SSM_W:SSM_W + AW], w_out[SSM_W + AW:]
    return pl.pallas_call(
        _outproj_last_kernel,
        grid=(bsz, n_lat // tr),
        in_specs=_mixer_specs(tr, lambda i: i + nct) + [
            pl.BlockSpec((1, tr, AW), lambda b, i: (b, i, 0)),
            pl.BlockSpec((SSM_W, d), lambda b, i: (0, 0)),
            pl.BlockSpec((AW, d), lambda b, i: (0, 0)),
            pl.BlockSpec((MW, d), lambda b, i: (0, 0)),
            pl.BlockSpec((1, tr, d), lambda b, i: (b, i + nct, 0)),
            pl.BlockSpec((1, 1, d), lambda b, i: (layer * 8 + b, 0, 2)),
            pl.BlockSpec((1, d), lambda b, i: (0, 0)),
        ],
        out_specs=pl.BlockSpec((1, tr, d), lambda b, i: (b, i, 0)),
        out_shape=jax.ShapeDtypeStruct((bsz, n_lat, d), F32),
        compiler_params=_cparams("parallel", "parallel"),
        name="out_proj_last",
    )(yssm, proj, proj, d_row, w_glu, h_f, h_b, proj, proj, mnorm,
      yb_lat, wa, wb, wc, h_prev, mod, final_w)


def _reorder_w_in(w):
    g0 = C_MO + MW
    return jnp.concatenate([w[:, :g0], w[:, g0 + 4 * MH:]], axis=1), w[:, g0:g0 + 4 * MH]


def kernel(x, c, ctx, c_ctx, norm_w, ada_w, ada_b, w_in, mlstm_gate_b, ssm_a_re, ssm_a_im, ssm_log_dt,
           ssm_b_re, ssm_b_im, ssm_c_re, ssm_c_im, ssm_d, ssm_w_glu, attn_q_norm, attn_k_norm,
           mlstm_conv_w, mlstm_conv_b, mlstm_norm_w, w_out, final_norm_w):
    bsz, n_lat, d = x.shape
    nctx = ctx.shape[1]
    s = nctx + n_lat
    depth = norm_w.shape[0]
    assert bsz < 8 and nctx % ROW_TILE == 0 and n_lat % ROW_TILE == 0 and depth == 2

    c_rows = jnp.zeros((8, d), F32).at[:bsz].set(c).at[bsz].set(c_ctx)
    mod = _ada_call(c_rows, ada_w, ada_b).reshape(depth * 8, 1, 3 * d)
    cos_all, sin_all, cos_lat, sin_lat = _rope_tables(n_lat, nctx)

    w_main, w_gr, b_gr = [], [], []
    for layer in range(depth):
        wm, wg = _reorder_w_in(w_in[layer])
        w_main.append(wm.astype(BF16))
        w_gr.append(wg.reshape(d, 2, 2 * MH).transpose(1, 2, 0).astype(BF16))
        gb = mlstm_gate_b[layer].astype(F32).reshape(2, 2 * MH)
        b_gr.append(jnp.broadcast_to(gb[:, :, None], (2, 2 * MH, 128)))

    norm_w3 = norm_w.astype(F32).reshape(depth, 1, d)
    xn, g_row = _prenorm_call(x, ctx, norm_w3, mod, w_gr[0], b_gr[0], 0)
    h_prev = None
    out = None
    for layer in range(depth):
        last = layer == depth - 1
        proj = _inproj_call(xn.reshape(bsz * s, d), w_main[layer]).reshape(bsz, s, PROJ_W)

        yssm = _s5_branch(proj, dict(a_re=ssm_a_re[layer], a_im=ssm_a_im[layer], log_dt=ssm_log_dt[layer],
                                     b_re=ssm_b_re[layer], b_im=ssm_b_im[layer], c_re=ssm_c_re[layer],
                                     c_im=ssm_c_im[layer]), nctx)

        qw = attn_q_norm[layer].reshape(1, HD).astype(F32)
        ks, vs = _attn_kv_prep_call(proj, cos_all, sin_all, attn_k_norm[layer].reshape(1, HD).astype(F32))
        yb_lat = _attn_lat_call(proj, ks, vs, cos_lat, sin_lat, qw, nctx)

        conv_w = jnp.zeros((8, 2 * MW), F32).at[:3].set(mlstm_conv_w[layer].astype(F32))
        qc, kc = _mlstm_prep_call(proj, conv_w, mlstm_conv_b[layer].astype(F32).reshape(1, 2 * MW), nctx)
        h_f, h_b = _mlstm_call(qc, kc, proj, g_row, nctx)

        d_row = ssm_d[layer].astype(F32).reshape(1, SSM_W)
        w_glu = ssm_w_glu[layer].astype(BF16)
        mnorm = mlstm_norm_w[layer].astype(F32).reshape(1, MW)
        w_o = w_out[layer].astype(BF16)
        if not last:
            yb_ctx = _attn_ctx_call(proj, ks, vs, qw, nctx)
            h_prev, xn, g_row = _outproj_mid_call(yssm, h_f, h_b, yb_ctx, yb_lat, proj, d_row, w_glu, mnorm,
                                                  w_o, x, ctx, mod, norm_w3, w_gr[layer + 1], b_gr[layer + 1],
                                                  layer)
        else:
            out = _outproj_last_call(yssm, h_f, h_b, yb_lat, proj, d_row, w_glu, mnorm, w_o, h_prev, mod,
                                     final_norm_w.reshape(1, d), layer, nctx)
    return out
```

```python
import functools
import math

import jax
import jax.numpy as jnp
from jax import lax
from jax.experimental import pallas as pl
from jax.experimental.pallas import tpu as pltpu

F32 = jnp.float32
BF16 = jnp.bfloat16
EPS = 1e-6

SSM_W = 512
SSM_P = 16
SSM_G = 32
SSM_N = 64
S5_T = 16
S5_SLAB_G = 8
S5_SLABS = SSM_G // S5_SLAB_G
S5_ST = S5_SLAB_G * SSM_N
AH = 8
AKV = 2
AGRP = AH // AKV
HD = 128
AW = AH * HD
KVW = AKV * HD
GRID_W = 64
ROPE_THETA = 10000.0
MH = 4
MD = 128
MW = MH * MD
MCH = 128
C_U, C_Z, C_AQ, C_AK, C_AV, C_AZ, C_MQ, C_MK, C_MV, C_MO, C_MZ = (
    0, 512, 1024, 2048, 2304, 2560, 3584, 4096, 4608, 5120, 5632)
PROJ_W = 6144
MIX_W = SSM_W + AW + MW

ROW_TILE = 256
VMEM_LIMIT = 48 * 1024 * 1024


def _cparams(*sem):
    return pltpu.CompilerParams(dimension_semantics=sem, vmem_limit_bytes=VMEM_LIMIT)


def _dot(a, b):
    return jnp.dot(a, b, preferred_element_type=F32)


def _dot_nt(a, b):
    return lax.dot_general(a, b, (((1,), (1,)), ((), ())), preferred_element_type=F32)


def _dot_tn(a, b):
    return lax.dot_general(a, b, (((0,), (0,)), ((), ())), preferred_element_type=F32)


def _split3(x):
    hi = x.astype(BF16)
    r1 = x - hi.astype(F32)
    mid = r1.astype(BF16)
    lo = (r1 - mid.astype(F32)).astype(BF16)
    return hi, mid, lo


def _silu(x):
    return x * jax.nn.sigmoid(x)


def _log_sigmoid(x):
    return jnp.minimum(x, 0.0) - jnp.log1p(jnp.exp(-jnp.abs(x)))


def _rms(x, w):
    return x * lax.rsqrt(jnp.mean(x * x, axis=-1, keepdims=True) + EPS) * w


def _pick_tile(n, cap, mult):
    best = mult
    for t in range(mult, min(n, cap) + 1, mult):
        if n % t == 0:
            best = t
    return best


def _ada_kernel(c_ref, w_ref, b_ref, o_ref):
    a = _silu(c_ref[...])
    w = w_ref[0]
    a_hi = a.astype(BF16)
    a_lo = (a - a_hi.astype(F32)).astype(BF16)
    w_hi = w.astype(BF16)
    w_lo = (w - w_hi.astype(F32)).astype(BF16)
    o_ref[0] = _dot(a_hi, w_hi) + _dot(a_hi, w_lo) + _dot(a_lo, w_hi) + b_ref[0]


def _ada_call(c_rows, ada_w, ada_b):
    depth, d, n3 = ada_w.shape
    tn = 512
    return pl.pallas_call(
        _ada_kernel,
        grid=(depth, n3 // tn),
        in_specs=[
            pl.BlockSpec((8, d), lambda l, j: (0, 0)),
            pl.BlockSpec((1, d, tn), lambda l, j: (l, 0, j)),
            pl.BlockSpec((1, 1, tn), lambda l, j: (l, 0, j)),
        ],
        out_specs=pl.BlockSpec((1, 8, tn), lambda l, j: (l, 0, j)),
        out_shape=jax.ShapeDtypeStruct((depth, 8, n3), F32),
        compiler_params=_cparams("parallel", "parallel"),
        name="ada_mod",
    )(c_rows, ada_w, ada_b.reshape(depth, 1, n3))


def _prenorm_kernel(x_ref, ctx_ref, nw_ref, sh_ref, sc_ref, wgr_ref, bgr_ref, o_ref, gr_ref, *, nct):
    h = jnp.where(pl.program_id(1) < nct, ctx_ref[0], x_ref[0])
    xn = (_rms(h, nw_ref[0]) * (1.0 + sc_ref[0]) + sh_ref[0]).astype(BF16)
    o_ref[0] = xn
    for dirn in range(2):
        gr_ref[0, dirn] = _dot_nt(wgr_ref[dirn], xn) + bgr_ref[dirn][:, :1]


def _mod_spec(d, layer, part, nct, nb):
    def idx(b, i):
        return (layer * 8 + jnp.where(i < nct, nb, b), 0, part)
    return pl.BlockSpec((1, 1, d), idx)


def _prenorm_call(x, ctx, nw, mod, w_gr, b_gr, layer):
    bsz, n, d = x.shape
    nctx = ctx.shape[1]
    tr = ROW_TILE
    nct = nctx // tr
    s = nctx + n
    return pl.pallas_call(
        functools.partial(_prenorm_kernel, nct=nct),
        grid=(bsz, s // tr),
        in_specs=[
            pl.BlockSpec((1, tr, d), lambda b, i: (b, jnp.maximum(i - nct, 0), 0)),
            pl.BlockSpec((1, tr, d), lambda b, i: (b, jnp.minimum(i, nct - 1), 0)),
            pl.BlockSpec((1, 1, d), lambda b, i: (layer, 0, 0)),
            _mod_spec(d, layer, 0, nct, bsz),
            _mod_spec(d, layer, 1, nct, bsz),
            pl.BlockSpec((2, 8, d), lambda b, i: (0, 0, 0)),
            pl.BlockSpec((2, 8, 128), lambda b, i: (0, 0, 0)),
        ],
        out_specs=[
            pl.BlockSpec((1, tr, d), lambda b, i: (b, i, 0)),
            pl.BlockSpec((1, 2, 8, tr), lambda b, i: (b, 0, 0, i)),
        ],
        out_shape=[
            jax.ShapeDtypeStruct((bsz, s, d), BF16),
            jax.ShapeDtypeStruct((bsz, 2, 8, s), F32),
        ],
        compiler_params=_cparams("parallel", "parallel"),
        name="prenorm",
    )(x, ctx, nw, mod, mod, w_gr, b_gr)


def _matmul_kernel(x_ref, w_ref, o_ref):
    o_ref[...] = _dot(x_ref[...], w_ref[...])


def _inproj_call(xn2d, w):
    m, d = xn2d.shape
    n = w.shape[1]
    tm = _pick_tile(m, 1056, 16)
    tn = 1024
    return pl.pallas_call(
        _matmul_kernel,
        grid=(n // tn, m // tm),
        in_specs=[
            pl.BlockSpec((tm, d), lambda j, i: (i, 0)),
            pl.BlockSpec((d, tn), lambda j, i: (0, j)),
        ],
        out_specs=pl.BlockSpec((tm, tn), lambda j, i: (i, j)),
        out_shape=jax.ShapeDtypeStruct((m, n), F32),
        compiler_params=_cparams("parallel", "parallel"),
        name="in_proj",
    )(xn2d, w)


def _s5_weights(a_re, a_im, log_dt, b_re, b_im, c_re, c_im):
    t_ = S5_T
    a_re = a_re.astype(F32)
    a_im = a_im.astype(F32)
    dt = jnp.exp(log_dt.astype(F32))[..., None]
    mag = jnp.exp(a_re * dt)
    lam_re = mag * jnp.cos(a_im * dt)
    lam_im = mag * jnp.sin(a_im * dt)
    inv_abs2 = 1.0 / (a_re * a_re + a_im * a_im)
    num_re, num_im = lam_re - 1.0, lam_im
    f_re = (num_re * a_re + num_im * a_im) * inv_abs2
    f_im = (num_im * a_re - num_re * a_im) * inv_abs2
    b_re = b_re.astype(F32)[None]
    b_im = b_im.astype(F32)[None]
    bb_re = f_re[:, :, None, :] * b_re - f_im[:, :, None, :] * b_im
    bb_im = f_re[:, :, None, :] * b_im + f_im[:, :, None, :] * b_re
    c_re = c_re.astype(F32)
    c_im = c_im.astype(F32)

    pr = [jnp.ones_like(lam_re)]
    pi = [jnp.zeros_like(lam_im)]
    for _ in range(t_):
        pr.append(pr[-1] * lam_re - pi[-1] * lam_im)
        pi.append(pr[-2] * lam_im + pi[-1] * lam_re)

    def powers(exps_f, exps_b):
        re = jnp.stack([jnp.stack([pr[e][0] for e in exps_f]), jnp.stack([pr[e][1] for e in exps_b])])
        im = jnp.stack([jnp.stack([pi[e][0] for e in exps_f]), jnp.stack([pi[e][1] for e in exps_b])])
        return re[:, :, :, None, :], im[:, :, :, None, :]

    def slab_rows(w):
        return w.reshape(2, t_, S5_SLABS, 128, 2 * SSM_N).astype(BF16)

    steps = list(range(t_))
    p_re, p_im = powers([t_ - 1 - t for t in steps], steps)
    win = slab_rows(jnp.concatenate([bb_re[:, None] * p_re - bb_im[:, None] * p_im,
                                     bb_re[:, None] * p_im + bb_im[:, None] * p_re], axis=-1))
    p_re, p_im = powers([t + 1 for t in steps], [t_ - t for t in steps])
    wout = slab_rows(jnp.concatenate([c_re[:, None] * p_re - c_im[:, None] * p_im,
                                      -(c_re[:, None] * p_im + c_im[:, None] * p_re)], axis=-1))

    p_re, p_im = powers(steps, steps)
    kr = c_re[:, None] * p_re - c_im[:, None] * p_im
    ki = c_re[:, None] * p_im + c_im[:, None] * p_re
    kk = jnp.einsum('dgqm,dtgpm->dgqtp', jnp.concatenate([bb_re, bb_im], axis=-1),
                    jnp.concatenate([kr, -ki], axis=-1), precision=lax.Precision.HIGH)
    k_f = kk[0].reshape(S5_SLABS, 128, t_ * SSM_P)
    k_b = kk[1][:, :, ::-1, :].reshape(S5_SLABS, 128, t_ * SSM_P)

    def lam_row(v):
        return v.reshape(2, S5_SLABS, 1, S5_ST)

    lam_t = jnp.concatenate([lam_row(pr[t_]), lam_row(pi[t_])], axis=-1)
    return win, wout, k_f, k_b, lam_t


def _iota(shape, dim):
    return lax.broadcasted_iota(jnp.int32, shape, dim)


def _s5_core_kernel(u_ref, win_ref, wout_ref, kf_ref, kb_ref, lam_ref, y_ref, lhs_sc, w_sc, s_sc, acc_sc,
                    *, nch, nctx_ch):
    t_ = S5_T
    n2 = 2 * SSM_N
    st2 = 2 * S5_ST
    lg_p = SSM_P.bit_length() - 1
    lg_n = SSM_N.bit_length() - 1
    lg_st = S5_ST.bit_length() - 1

    for t in range(t_):
        lhs_sc[:, t * 128:(t + 1) * 128] = u_ref[0, pl.ds(t, nch, stride=t_), :].astype(BF16)

    k1, c1 = _iota((n2, st2), 0), _iota((n2, st2), 1)
    e_in = jnp.where(((k1 >> lg_n) == (c1 >> lg_st)) & ((k1 & (SSM_N - 1)) == (c1 & (SSM_N - 1))),
                     1.0, 0.0).astype(BF16)
    r1, c1b = _iota((128, st2), 0), _iota((128, st2), 1)
    m_in = (r1 >> lg_p) == ((c1b & (S5_ST - 1)) >> lg_n)

    def expand_state_table(tab_ref, d):
        for t in range(t_):
            rows = slice(t * 128, (t + 1) * 128)
            w_sc[rows, :st2] = jnp.where(m_in, _dot(tab_ref[d, t, 0], e_in), 0.0).astype(BF16)

    for d in range(2):
        expand_state_table(win_ref, d)
        s_sc[d] = _dot(lhs_sc[...], w_sc[:, :st2])

    lam_f = lam_ref[0, 0]
    lam_b = lam_ref[1, 0]
    lrf, lif = lam_f[:, :S5_ST], lam_f[:, S5_ST:]
    lrb, lib = lam_b[:, :S5_ST], lam_b[:, S5_ST:]

    def step(k, carry):
        hrf, hif, hrb, hib = carry
        cb = jnp.where(k < nctx_ch, nctx_ch - 1 - k, nch - 1 - (k - nctx_ch))
        sf = s_sc[0, pl.ds(k, 1), :]
        sb = s_sc[1, pl.ds(cb, 1), :]
        s_sc[0, pl.ds(k, 1), :] = jnp.concatenate([hrf, hif], axis=-1)
        s_sc[1, pl.ds(cb, 1), :] = jnp.concatenate([hrb, hib], axis=-1)
        return (lrf * hrf - lif * hif + sf[:, :S5_ST], lrf * hif + lif * hrf + sf[:, S5_ST:],
                lrb * hrb - lib * hib + sb[:, :S5_ST], lrb * hib + lib * hrb + sb[:, S5_ST:])

    zero = jnp.zeros((1, S5_ST), F32)
    lax.fori_loop(0, nch, step, (zero, zero, zero, zero))

    k2, c2 = _iota((t_ * SSM_P, t_ * 128), 0), _iota((t_ * SSM_P, t_ * 128), 1)
    e_m = jnp.where(((k2 >> lg_p) == (c2 >> 7)) & ((k2 & (SSM_P - 1)) == (c2 & (SSM_P - 1))),
                    1.0, 0.0).astype(BF16)
    r2, c2b = _iota((128, t_ * 128), 0), _iota((128, t_ * 128), 1)
    m_m = (r2 >> lg_p) == ((c2b & 127) >> lg_p)
    k_f = kf_ref[0]
    k_b = kb_ref[0]
    lag_col = _iota((128, t_ * SSM_P), 1)
    for t in range(t_):
        rows = slice(t * 128, (t + 1) * 128)
        fwd = k_f if t == 0 else pltpu.roll(k_f, t * SSM_P, 1)
        bwd = k_b if t == t_ - 1 else pltpu.roll(k_b, (t + 1) * SSM_P, 1)
        mc = (jnp.where(lag_col >= t * SSM_P, fwd, 0.0)
              + jnp.where(lag_col < (t + 1) * SSM_P, bwd, 0.0)).astype(BF16)
        w_sc[rows, :] = jnp.where(m_m, _dot(mc, e_m), 0.0).astype(BF16)
    acc_sc[...] = _dot(lhs_sc[...], w_sc[...])

    for d in range(2):
        expand_state_table(wout_ref, d)
        acc_sc[...] += _dot_nt(s_sc[d].astype(BF16), w_sc[:, :st2])

    for t in range(t_):
        y_ref[0, pl.ds(t, nch, stride=t_), :] = acc_sc[:, t * 128:(t + 1) * 128]


def _s5_core_call(proj, win, wout, k_f, k_b, lam_t, nctx):
    bsz, s, _ = proj.shape
    t_ = S5_T
    nch = s // t_
    kdim = t_ * 128
    n2 = 2 * SSM_N
    st2 = 2 * S5_ST
    cu = C_U // 128
    return pl.pallas_call(
        functools.partial(_s5_core_kernel, nch=nch, nctx_ch=nctx // t_),
        grid=(S5_SLABS, bsz),
        in_specs=[
            pl.BlockSpec((1, s, 128), lambda sl, b: (b, 0, cu + sl)),
            pl.BlockSpec((2, t_, 1, 128, n2), lambda sl, b: (0, 0, sl, 0, 0)),
            pl.BlockSpec((2, t_, 1, 128, n2), lambda sl, b: (0, 0, sl, 0, 0)),
            pl.BlockSpec((1, 128, t_ * SSM_P), lambda sl, b: (sl, 0, 0)),
            pl.BlockSpec((1, 128, t_ * SSM_P), lambda sl, b: (sl, 0, 0)),
            pl.BlockSpec((2, 1, 1, st2), lambda sl, b: (0, sl, 0, 0)),
        ],
        out_specs=pl.BlockSpec((1, s, 128), lambda sl, b: (b, 0, sl)),
        out_shape=jax.ShapeDtypeStruct((bsz, s, SSM_W), F32),
        scratch_shapes=[
            pltpu.VMEM((nch, kdim), BF16),
            pltpu.VMEM((kdim, kdim), BF16),
            pltpu.VMEM((2, nch, st2), F32),
            pltpu.VMEM((nch, kdim), F32),
        ],
        compiler_params=pltpu.CompilerParams(dimension_semantics=("parallel", "parallel"),
                                             vmem_limit_bytes=56 * 1024 * 1024),
        name="s5_core",
    )(proj, win, wout, k_f, k_b, lam_t)


def _s5_branch(proj, p, nctx):
    win, wout, k_f, k_b, lam_t = _s5_weights(p['a_re'], p['a_im'], p['log_dt'], p['b_re'], p['b_im'],
                                             p['c_re'], p['c_im'])
    return _s5_core_call(proj, win, wout, k_f, k_b, lam_t, nctx)


def _rope_tables(n_lat, nctx):
    rows = n_lat // GRID_W
    row = jnp.broadcast_to(jnp.arange(rows, dtype=F32)[:, None], (rows, GRID_W)).reshape(-1)
    col = jnp.broadcast_to(jnp.arange(GRID_W, dtype=F32)[None, :], (rows, GRID_W)).reshape(-1)
    n_freq = HD // 4
    inv = ROPE_THETA ** (-jnp.arange(n_freq, dtype=F32) / n_freq)
    ang = jnp.concatenate([row[:, None] * inv, col[:, None] * inv], axis=-1)
    cs, sn = jnp.cos(ang), jnp.sin(ang)
    cos2 = jnp.concatenate([cs, cs], axis=-1)
    sin2 = jnp.concatenate([-sn, sn], axis=-1)
    cos_all = jnp.concatenate([jnp.ones((nctx, HD), F32), cos2], axis=0)
    sin_all = jnp.concatenate([jnp.zeros((nctx, HD), F32), sin2], axis=0)
    return cos_all, sin_all, cos2, sin2


def _norm_rope(x, w, cs, sn):
    y = _rms(x, w)
    return y * cs + pltpu.roll(y, HD // 2, 1) * sn


def _attn_kv_prep_kernel(k_ref, v_ref, cos_ref, sin_ref, kw_ref, ko_ref, vo_ref):
    cs = cos_ref[...]
    sn = sin_ref[...]
    k = k_ref[0]
    for h in range(AKV):
        ko_ref[0, :, h * HD:(h + 1) * HD] = _norm_rope(k[:, h * HD:(h + 1) * HD], kw_ref[...], cs, sn).astype(BF16)
    vo_ref[0] = v_ref[0].astype(BF16)


def _attn_kv_prep_call(proj, cos2, sin2, kw):
    bsz, s, _ = proj.shape
    tr = ROW_TILE
    return pl.pallas_call(
        _attn_kv_prep_kernel,
        grid=(bsz, s // tr),
        in_specs=[
            pl.BlockSpec((1, tr, KVW), lambda b, i: (b, i, C_AK // KVW)),
            pl.BlockSpec((1, tr, KVW), lambda b, i: (b, i, C_AV // KVW)),
            pl.BlockSpec((tr, HD), lambda b, i: (i, 0)),
            pl.BlockSpec((tr, HD), lambda b, i: (i, 0)),
            pl.BlockSpec((1, HD), lambda b, i: (0, 0)),
        ],
        out_specs=[
            pl.BlockSpec((1, tr, KVW), lambda b, i: (b, i, 0)),
            pl.BlockSpec((1, tr, KVW), lambda b, i: (b, i, 0)),
        ],
        out_shape=[
            jax.ShapeDtypeStruct((bsz, s, KVW), BF16),
            jax.ShapeDtypeStruct((bsz, s, KVW), BF16),
        ],
        compiler_params=_cparams("parallel", "parallel"),
        name="attn_kv_prep",
    )(proj, proj, cos2, sin2, kw)


Q_SCALE = HD ** -0.5 * math.log2(math.e)


def _attn_finish(acc, l, z_refs, o_ref, tq):
    o = acc / l
    ntile = len(z_refs)
    for h in range(AGRP):
        for t, z_ref in enumerate(z_refs):
            r0 = (h * ntile + t) * tq
            z = z_ref[0][:, h * HD:(h + 1) * HD]
            o_ref[0, t * tq:(t + 1) * tq, h * HD:(h + 1) * HD] = (o[r0:r0 + tq] * _silu(z)).astype(BF16)


def _attn_ctx_kernel(q_ref, z_ref, qw_ref, k_ref, v_ref, o_ref, *, tq):
    q = q_ref[0]
    q4 = jnp.concatenate([(_rms(q[:, h * HD:(h + 1) * HD], qw_ref[...]) * Q_SCALE).astype(BF16)
                          for h in range(AGRP)], axis=0)
    s = _dot_nt(q4, k_ref[0])
    m = jnp.max(s, axis=-1, keepdims=True)
    p = jnp.exp2(s - m)
    l = jnp.sum(p, axis=-1, keepdims=True)
    _attn_finish(_dot(p.astype(BF16), v_ref[0]), l, [z_ref], o_ref, tq)


def _attn_ctx_call(proj, ks, vs, qw, nctx):
    bsz = proj.shape[0]
    tq = ROW_TILE
    zw = AGRP * HD
    return pl.pallas_call(
        functools.partial(_attn_ctx_kernel, tq=tq),
        grid=(bsz, AKV, nctx // tq),
        in_specs=[
            pl.BlockSpec((1, tq, zw), lambda b, g, i: (b, i, C_AQ // zw + g)),
            pl.BlockSpec((1, tq, zw), lambda b, g, i: (b, i, C_AZ // zw + g)),
            pl.BlockSpec((1, HD), lambda b, g, i: (0, 0)),
            pl.BlockSpec((1, nctx, HD), lambda b, g, i: (b, 0, g)),
            pl.BlockSpec((1, nctx, HD), lambda b, g, i: (b, 0, g)),
        ],
        out_specs=pl.BlockSpec((1, tq, zw), lambda b, g, i: (b, i, g)),
        out_shape=jax.ShapeDtypeStruct((bsz, nctx, AW), BF16),
        compiler_params=_cparams("parallel", "parallel", "parallel"),
        name="attention_ctx",
    )(proj, proj, qw, ks, vs)


def _attn_lat_kernel(q_ref, z_ref, cos_ref, sin_ref, qw_ref, k_ref, v_ref, o_ref, s_sc, *, tq, ck, nck):
    q = q_ref[0]
    q4 = jnp.concatenate(
        [(_norm_rope(q[:, h * HD:(h + 1) * HD], qw_ref[...], cos_ref[...], sin_ref[...]) * Q_SCALE).astype(BF16)
         for h in range(AGRP)], axis=0)

    s_sc[0] = _dot_nt(q4, k_ref[0, 0:ck, :])
    m = l = acc = None
    for j in range(nck):
        if j + 1 < nck:
            s_sc[(j + 1) % 2] = _dot_nt(q4, k_ref[0, (j + 1) * ck:(j + 2) * ck, :])
        s = s_sc[j % 2]
        s_max = jnp.max(s, axis=-1, keepdims=True)
        m_new = s_max if j == 0 else jnp.maximum(m, s_max)
        p = jnp.exp2(s - m_new)
        pv = _dot(p.astype(BF16), v_ref[0, j * ck:(j + 1) * ck, :])
        p_sum = jnp.sum(p, axis=-1, keepdims=True)
        if j == 0:
            l, acc = p_sum, pv
        else:
            alpha = jnp.exp2(m - m_new)
            l = alpha * l + p_sum
            acc = alpha * acc + pv
        m = m_new
    _attn_finish(acc, l, [z_ref], o_ref, tq)


def _attn_lat_call(proj, ks, vs, cos_lat, sin_lat, qw, nctx):
    bsz, s, _ = proj.shape
    n_lat = s - nctx
    tq = ROW_TILE
    nct = nctx // tq
    ck = _pick_tile(s, 1536, 128)
    nck = s // ck
    zw = AGRP * HD
    cq = C_AQ // zw
    cz = C_AZ // zw
    return pl.pallas_call(
        functools.partial(_attn_lat_kernel, tq=tq, ck=ck, nck=nck),
        grid=(bsz, AKV, n_lat // tq),
        in_specs=[
            pl.BlockSpec((1, tq, zw), lambda b, g, i: (b, nct + i, cq + g)),
            pl.BlockSpec((1, tq, zw), lambda b, g, i: (b, nct + i, cz + g)),
            pl.BlockSpec((tq, HD), lambda b, g, i: (i, 0)),
            pl.BlockSpec((tq, HD), lambda b, g, i: (i, 0)),
            pl.BlockSpec((1, HD), lambda b, g, i: (0, 0)),
            pl.BlockSpec((1, s, HD), lambda b, g, i: (b, 0, g)),
            pl.BlockSpec((1, s, HD), lambda b, g, i: (b, 0, g)),
        ],
        out_specs=pl.BlockSpec((1, tq, zw), lambda b, g, i: (b, i, g)),
        out_shape=jax.ShapeDtypeStruct((bsz, n_lat, AW), BF16),
        scratch_shapes=[pltpu.VMEM((2, AGRP * tq, ck), F32)],
        compiler_params=_cparams("parallel", "parallel", "parallel"),
        name="attention",
    )(proj, proj, cos_lat, sin_lat, qw, ks, vs)


def _mlstm_prep_kernel(q_ref, k_ref, qp_ref, kp_ref, qn_ref, kn_ref, w_ref, b_ref, qo_ref, ko_ref,
                       *, tr, nct, nt):
    i = pl.program_id(1)
    first = jnp.logical_or(i == 0, i == nct)
    last = jnp.logical_or(i == nct - 1, i == nt - 1)
    rid = lax.broadcasted_iota(jnp.int32, (tr, MW), 0)
    w = w_ref[...]
    bias = b_ref[...]

    def conv(x, prev_row, next_row, off):
        prev_row = jnp.where(first, 0.0, prev_row)
        next_row = jnp.where(last, 0.0, next_row)
        xp = jnp.where(rid == 0, prev_row, pltpu.roll(x, 1, 0))
        xn = jnp.where(rid == tr - 1, next_row, pltpu.roll(x, tr - 1, 0))
        y = (w[0:1, off:off + MW] * xp + w[1:2, off:off + MW] * x + w[2:3, off:off + MW] * xn
             + bias[:, off:off + MW])
        return _silu(y)

    qo_ref[0] = conv(q_ref[0], qp_ref[0, 7:8, :], qn_ref[0, 0:1, :], 0).astype(BF16)
    ko_ref[0] = (conv(k_ref[0], kp_ref[0, 7:8, :], kn_ref[0, 0:1, :], MW) * (MD ** -0.5)).astype(BF16)


def _mlstm_prep_call(proj, conv_w, conv_b, nctx):
    bsz, s, _ = proj.shape
    tr = ROW_TILE
    nt = s // tr
    nct = nctx // tr
    r8 = tr // 8
    n8 = s // 8
    cq = C_MQ // MW
    ck = C_MK // MW

    def prev_map(c):
        return lambda b, i: (b, jnp.maximum(i * r8 - 1, 0), c)

    def next_map(c):
        return lambda b, i: (b, jnp.minimum((i + 1) * r8, n8 - 1), c)

    return pl.pallas_call(
        functools.partial(_mlstm_prep_kernel, tr=tr, nct=nct, nt=nt),
        grid=(bsz, nt),
        in_specs=[
            pl.BlockSpec((1, tr, MW), lambda b, i: (b, i, cq)),
            pl.BlockSpec((1, tr, MW), lambda b, i: (b, i, ck)),
            pl.BlockSpec((1, 8, MW), prev_map(cq)),
            pl.BlockSpec((1, 8, MW), prev_map(ck)),
            pl.BlockSpec((1, 8, MW), next_map(cq)),
            pl.BlockSpec((1, 8, MW), next_map(ck)),
            pl.BlockSpec((8, 2 * MW), lambda b, i: (0, 0)),
            pl.BlockSpec((1, 2 * MW), lambda b, i: (0, 0)),
        ],
        out_specs=[
            pl.BlockSpec((1, tr, MW), lambda b, i: (b, i, 0)),
            pl.BlockSpec((1, tr, MW), lambda b, i: (b, i, 0)),
        ],
        out_shape=[
            jax.ShapeDtypeStruct((bsz, s, MW), BF16),
            jax.ShapeDtypeStruct((bsz, s, MW), BF16),
        ],
        compiler_params=_cparams("parallel", "parallel"),
        name="mlstm_prep",
    )(proj, proj, proj, proj, proj, proj, conv_w, conv_b)


def _mlstm_bwd_chunk(step, nctx_ch, nch):
    return jnp.where(step < nctx_ch, nctx_ch - 1 - step, nch - 1 - (step - nctx_ch))


def _hi_mid(x):
    hi = x.astype(BF16)
    mid = (x - hi.astype(F32)).astype(BF16)
    return jnp.concatenate([hi, mid], axis=1)


def _mlstm_chunk(dirn, q, k, v, gr, o_ref, b, ct_sc, n_sc, m_sc, base):
    row = _iota((MCH, MCH), 0)
    col = _iota((MCH, MCH), 1)
    allowed = (col <= row) if dirn == 0 else (col >= row)
    eye = col == row
    tri = jnp.where(allowed, 1.0, 0.0).astype(BF16)
    lf_r = _log_sigmoid(gr)
    r3 = _split3(lf_r)
    b_r = _dot_nt(r3[0], tri) + _dot_nt(r3[1], tri) + _dot_nt(r3[2], tri)

    a8 = jnp.concatenate([gr[:MH] - b_r[MH:], gr[:MH] - b_r[MH:]], axis=0)
    lane = _iota((8, MCH), 1)
    cm8 = a8
    sh = 1
    while sh < MCH:
        if dirn == 0:
            cm8 = jnp.maximum(cm8, jnp.where(lane >= sh, pltpu.roll(cm8, sh, 1), -jnp.inf))
        else:
            cm8 = jnp.maximum(cm8, jnp.where(lane < MCH - sh, pltpu.roll(cm8, MCH - sh, 1), -jnp.inf))
        sh *= 2

    ones2 = jnp.ones((2 * MCH, MCH), BF16)
    lhs = []
    for h in range(MH):
        lhs += [_hi_mid(jnp.where(allowed, lf_r[MH + h:MH + h + 1, :], 0.0)),
                _hi_mid(jnp.where(eye, cm8[h:h + 1, :], 0.0)),
                _hi_mid(jnp.where(eye, gr[h:h + 1, :], 0.0))]
    rep = _dot(jnp.concatenate(lhs, axis=0), ones2)

    heads = []
    lhs2 = []
    for h in range(MH):
        st = base + h
        bc = rep[(3 * h) * MCH:(3 * h + 1) * MCH]
        cm = rep[(3 * h + 1) * MCH:(3 * h + 2) * MCH]
        lic = rep[(3 * h + 2) * MCH:(3 * h + 3) * MCH]
        br = b_r[MH + h:MH + h + 1, :]
        li_r = gr[h:h + 1, :]
        m_row = m_sc[st][:1, :]
        qh = q[:, h * MD:(h + 1) * MD]
        kh = k[:, h * MD:(h + 1) * MD]

        log_d = jnp.where(allowed, bc - br + li_r, -jnp.inf)
        m_t = bc + jnp.maximum(m_row, cm)
        s = _dot_nt(qh, kh) * jnp.exp(log_d - m_t)
        s_bf = s.astype(BF16)
        lhs2 += [s_bf, (qh.astype(F32) * n_sc[st][:1, :]).astype(BF16)]
        heads.append((st, bc, lic, m_row, m_t, qh, kh, s_bf))

    rep2 = _dot(jnp.concatenate(lhs2, axis=0), jnp.ones((MCH, MCH), BF16))

    for h, (st, bc, lic, m_row, m_t, qh, kh, s_bf) in enumerate(heads):
        vh = v[:, h * MD:(h + 1) * MD]
        s_sum = rep2[(2 * h) * MCH:(2 * h + 1) * MCH]
        qn = rep2[(2 * h + 1) * MCH:(2 * h + 2) * MCH]
        w_inter = jnp.exp(bc + m_row - m_t)
        num = _dot(s_bf, vh) + w_inter * _dot(qh, ct_sc[st].astype(BF16))
        den = s_sum + w_inter * qn
        o_ref[b, :, h * MD:(h + 1) * MD] = num / jnp.maximum(jnp.abs(den), jnp.exp(-m_t))

        b_last = bc[MCH - 1:MCH, :] if dirn == 0 else bc[0:1, :]
        log_w = b_last - bc + lic
        m_new = jnp.maximum(b_last + m_row, jnp.max(log_w, axis=0, keepdims=True))
        decay = jnp.exp(b_last + m_row - m_new)
        kw = kh.astype(F32) * jnp.exp(log_w - m_new)
        ct_sc[st] = decay * ct_sc[st] + _dot_tn(kw.astype(BF16), vh)
        n_sc[st] = jnp.broadcast_to(decay * n_sc[st][:1, :] + jnp.sum(kw, axis=0, keepdims=True), (8, MD))
        m_sc[st] = jnp.broadcast_to(m_new, (8, 128))


def _mlstm_kernel(qf_ref, kf_ref, vf_ref, grf_ref, qb_ref, kb_ref, vb_ref, grb_ref,
                  of_ref, ob_ref, ct_sc, n_sc, m_sc, *, bsz):
    @pl.when(pl.program_id(0) == 0)
    def _():
        ct_sc[...] = jnp.zeros_like(ct_sc)
        n_sc[...] = jnp.zeros_like(n_sc)
        m_sc[...] = jnp.zeros_like(m_sc)

    for b in range(bsz):
        _mlstm_chunk(0, qf_ref[b], kf_ref[b], vf_ref[b].astype(BF16), grf_ref[b, 0],
                     of_ref, b, ct_sc, n_sc, m_sc, b * MH)
        _mlstm_chunk(1, qb_ref[b], kb_ref[b], vb_ref[b].astype(BF16), grb_ref[b, 0],
                     ob_ref, b, ct_sc, n_sc, m_sc, (bsz + b) * MH)


def _mlstm_call(qc, kc, proj, g_row, nctx):
    bsz, s, _ = qc.shape
    nch = s // MCH
    nctx_ch = nctx // MCH
    cv = C_MV // MW

    def cb(st):
        return _mlstm_bwd_chunk(st, nctx_ch, nch)

    nst = 2 * bsz * MH
    return pl.pallas_call(
        functools.partial(_mlstm_kernel, bsz=bsz),
        grid=(nch,),
        in_specs=[
            pl.BlockSpec((bsz, MCH, MW), lambda st: (0, st, 0)),
            pl.BlockSpec((bsz, MCH, MW), lambda st: (0, st, 0)),
            pl.BlockSpec((bsz, MCH, MW), lambda st: (0, st, cv)),
            pl.BlockSpec((bsz, 1, 8, MCH), lambda st: (0, 0, 0, st)),
            pl.BlockSpec((bsz, MCH, MW), lambda st: (0, cb(st), 0)),
            pl.BlockSpec((bsz, MCH, MW), lambda st: (0, cb(st), 0)),
            pl.BlockSpec((bsz, MCH, MW), lambda st: (0, cb(st), cv)),
            pl.BlockSpec((bsz, 1, 8, MCH), lambda st: (0, 1, 0, cb(st))),
        ],
        out_specs=[
            pl.BlockSpec((bsz, MCH, MW), lambda st: (0, st, 0)),
            pl.BlockSpec((bsz, MCH, MW), lambda st: (0, cb(st), 0)),
        ],
        out_shape=[
            jax.ShapeDtypeStruct((bsz, s, MW), F32),
            jax.ShapeDtypeStruct((bsz, s, MW), F32),
        ],
        scratch_shapes=[
            pltpu.VMEM((nst, MD, MD), F32),
            pltpu.VMEM((nst, 8, MD), F32),
            pltpu.VMEM((nst, 8, 128), F32),
        ],
        compiler_params=_cparams("arbitrary"),
        name="mlstm_scan",
    )(qc, kc, proj, g_row, qc, kc, proj, g_row)


def _mixer_specs(tr, row_of):
    def tok(col_block):
        return pl.BlockSpec((1, tr, SSM_W), lambda b, i: (b, row_of(i), col_block))
    const = lambda shape: pl.BlockSpec(shape, lambda b, i: (0,) * len(shape))
    return [tok(0), tok(C_U // SSM_W), tok(C_Z // SSM_W), const((1, SSM_W)), const((SSM_W, SSM_W)),
            tok(0), tok(0), tok(C_MO // MW), tok(C_MZ // MW), const((1, MW))]


def _mixer_outputs(y_ref, u_ref, sz_ref, d_ref, wg_ref, hf_ref, hb_ref, mo_ref, mz_ref, mn_ref):
    g = jax.nn.gelu(y_ref[0] + d_ref[...] * u_ref[0], approximate=True)
    ya = (g * jax.nn.sigmoid(_dot(g.astype(BF16), wg_ref[...])) * _silu(sz_ref[0])).astype(BF16)
    hs = hf_ref[0] + hb_ref[0]
    og, zg, nw = mo_ref[0], mz_ref[0], mn_ref[...]
    yc = []
    for h in range(MH):
        sl = slice(h * MD, (h + 1) * MD)
        yc.append((jax.nn.sigmoid(og[:, sl]) * _rms(hs[:, sl], nw[:, sl]) * _silu(zg[:, sl])).astype(BF16))
    return ya, jnp.concatenate(yc, axis=1)


def _mix_dot(ya, yb, yc, w_ref):
    return (_dot(ya, w_ref[:SSM_W, :]) + _dot(yb, w_ref[SSM_W:SSM_W + AW, :])
            + _dot(yc, w_ref[SSM_W + AW:, :]))


def _outproj_mid_kernel(*refs, nct):
    mixer, (ybc_ref, ybl_ref, w_ref, x_ref, ctx_ref, g_ref, nw_ref, sh_ref, sc_ref,
            wgr_ref, bgr_ref, h_ref, xn_ref, gr_ref) = refs[:10], refs[10:]
    is_ctx = pl.program_id(1) < nct
    ya, yc = _mixer_outputs(*mixer)
    yb = jnp.where(is_ctx, ybc_ref[0], ybl_ref[0])
    mix = _mix_dot(ya, yb, yc, w_ref)
    h = jnp.where(is_ctx, ctx_ref[0], x_ref[0]) + g_ref[0] * mix
    h_ref[0] = h
    xn = (_rms(h, nw_ref[0]) * (1.0 + sc_ref[0]) + sh_ref[0]).astype(BF16)
    xn_ref[0] = xn
    for dirn in range(2):
        gr_ref[0, dirn] = _dot_nt(wgr_ref[dirn], xn) + bgr_ref[dirn][:, :1]


def _outproj_mid_call(yssm, h_f, h_b, yb_ctx, yb_lat, proj, d_row, w_glu, mnorm, w_out, x, ctx, mod, nw,
                      w_gr, b_gr, layer):
    bsz, s, _ = yssm.shape
    d = x.shape[-1]
    nctx = ctx.shape[1]
    tr = ROW_TILE
    nct = nctx // tr
    nxt = layer + 1
    lat = lambda b, i: (b, jnp.maximum(i - nct, 0), 0)
    cxt = lambda b, i: (b, jnp.minimum(i, nct - 1), 0)
    return pl.pallas_call(
        functools.partial(_outproj_mid_kernel, nct=nct),
        grid=(bsz, s // tr),
        in_specs=_mixer_specs(tr, lambda i: i) + [
            pl.BlockSpec((1, tr, AW), cxt),
            pl.BlockSpec((1, tr, AW), lat),
            pl.BlockSpec((MIX_W, d), lambda b, i: (0, 0)),
            pl.BlockSpec((1, tr, d), lat),
            pl.BlockSpec((1, tr, d), cxt),
            _mod_spec(d, layer, 2, nct, bsz),
            pl.BlockSpec((1, 1, d), lambda b, i: (nxt, 0, 0)),
            _mod_spec(d, nxt, 0, nct, bsz),
            _mod_spec(d, nxt, 1, nct, bsz),
            pl.BlockSpec((2, 8, d), lambda b, i: (0, 0, 0)),
            pl.BlockSpec((2, 8, 128), lambda b, i: (0, 0, 0)),
        ],
        out_specs=[
            pl.BlockSpec((1, tr, d), lambda b, i: (b, i, 0)),
            pl.BlockSpec((1, tr, d), lambda b, i: (b, i, 0)),
            pl.BlockSpec((1, 2, 8, tr), lambda b, i: (b, 0, 0, i)),
        ],
        out_shape=[
            jax.ShapeDtypeStruct((bsz, s, d), F32),
            jax.ShapeDtypeStruct((bsz, s, d), BF16),
            jax.ShapeDtypeStruct((bsz, 2, 8, s), F32),
        ],
        compiler_params=_cparams("parallel", "parallel"),
        name="out_proj_mid",
    )(yssm, proj, proj, d_row, w_glu, h_f, h_b, proj, proj, mnorm,
      yb_ctx, yb_lat, w_out, x, ctx, mod, nw, mod, mod, w_gr, b_gr)


def _outproj_last_kernel(*refs):
    mixer, (ybl_ref, w_ref, h_ref, g_ref, nw_ref, o_ref) = refs[:10], refs[10:]
    ya, yc = _mixer_outputs(*mixer)
    mix = _mix_dot(ya, ybl_ref[0], yc, w_ref)
    o_ref[0] = _rms(h_ref[0] + g_ref[0] * mix, nw_ref[...])


def _outproj_last_call(yssm, h_f, h_b, yb_lat, proj, d_row, w_glu, mnorm, w_out, h_prev, mod, final_w,
                       layer, nctx):
    bsz, s, d = h_prev.shape
    tr = ROW_TILE
    nct = nctx // tr
    n_lat = s - nctx
    return pl.pallas_call(
        _outproj_last_kernel,
        grid=(bsz, n_lat // tr),
        in_specs=_mixer_specs(tr, lambda i: i + nct) + [
            pl.BlockSpec((1, tr, AW), lambda b, i: (b, i, 0)),
            pl.BlockSpec((MIX_W, d), lambda b, i: (0, 0)),
            pl.BlockSpec((1, tr, d), lambda b, i: (b, i + nct, 0)),
            pl.BlockSpec((1, 1, d), lambda b, i: (layer * 8 + b, 0, 2)),
            pl.BlockSpec((1, d), lambda b, i: (0, 0)),
        ],
        out_specs=pl.BlockSpec((1, tr, d), lambda b, i: (b, i, 0)),
        out_shape=jax.ShapeDtypeStruct((bsz, n_lat, d), F32),
        compiler_params=_cparams("parallel", "parallel"),
        name="out_proj_last",
    )(yssm, proj, proj, d_row, w_glu, h_f, h_b, proj, proj, mnorm,
      yb_lat, w_out, h_prev, mod, final_w)


def _reorder_w_in(w):
    g0 = C_MO + MW
    return jnp.concatenate([w[:, :g0], w[:, g0 + 4 * MH:]], axis=1), w[:, g0:g0 + 4 * MH]


def kernel(x, c, ctx, c_ctx, norm_w, ada_w, ada_b, w_in, mlstm_gate_b, ssm_a_re, ssm_a_im, ssm_log_dt,
           ssm_b_re, ssm_b_im, ssm_c_re, ssm_c_im, ssm_d, ssm_w_glu, attn_q_norm, attn_k_norm,
           mlstm_conv_w, mlstm_conv_b, mlstm_norm_w, w_out, final_norm_w):
    bsz, n_lat, d = x.shape
    nctx = ctx.shape[1]
    s = nctx + n_lat
    depth = norm_w.shape[0]
    assert bsz < 8 and nctx % ROW_TILE == 0 and n_lat % ROW_TILE == 0 and depth == 2

    c_rows = jnp.zeros((8, d), F32).at[:bsz].set(c).at[bsz].set(c_ctx)
    mod = _ada_call(c_rows, ada_w, ada_b).reshape(depth * 8, 1, 3 * d)
    cos_all, sin_all, cos_lat, sin_lat = _rope_tables(n_lat, nctx)

    w_main, w_gr, b_gr = [], [], []
    for layer in range(depth):
        wm, wg = _reorder_w_in(w_in[layer])
        w_main.append(wm.astype(BF16))
        w_gr.append(wg.reshape(d, 2, 2 * MH).transpose(1, 2, 0).astype(BF16))
        gb = mlstm_gate_b[layer].astype(F32).reshape(2, 2 * MH)
        b_gr.append(jnp.broadcast_to(gb[:, :, None], (2, 2 * MH, 128)))

    norm_w3 = norm_w.astype(F32).reshape(depth, 1, d)
    xn, g_row = _prenorm_call(x, ctx, norm_w3, mod, w_gr[0], b_gr[0], 0)
    h_prev = None
    out = None
    for layer in range(depth):
        last = layer == depth - 1
        proj = _inproj_call(xn.reshape(bsz * s, d), w_main[layer]).reshape(bsz, s, PROJ_W)

        yssm = _s5_branch(proj, dict(a_re=ssm_a_re[layer], a_im=ssm_a_im[layer], log_dt=ssm_log_dt[layer],
                                     b_re=ssm_b_re[layer], b_im=ssm_b_im[layer], c_re=ssm_c_re[layer],
                                     c_im=ssm_c_im[layer]), nctx)

        qw = attn_q_norm[layer].reshape(1, HD).astype(F32)
        ks, vs = _attn_kv_prep_call(proj, cos_all, sin_all, attn_k_norm[layer].reshape(1, HD).astype(F32))
        yb_lat = _attn_lat_call(proj, ks, vs, cos_lat, sin_lat, qw, nctx)

        conv_w = jnp.zeros((8, 2 * MW), F32).at[:3].set(mlstm_conv_w[layer].astype(F32))
        qc, kc = _mlstm_prep_call(proj, conv_w, mlstm_conv_b[layer].astype(F32).reshape(1, 2 * MW), nctx)
        h_f, h_b = _mlstm_call(qc, kc, proj, g_row, nctx)

        d_row = ssm_d[layer].astype(F32).reshape(1, SSM_W)
        w_glu = ssm_w_glu[layer].astype(BF16)
        mnorm = mlstm_norm_w[layer].astype(F32).reshape(1, MW)
        w_o = w_out[layer].astype(BF16)
        if not last:
            yb_ctx = _attn_ctx_call(proj, ks, vs, qw, nctx)
            h_prev, xn, g_row = _outproj_mid_call(yssm, h_f, h_b, yb_ctx, yb_lat, proj, d_row, w_glu, mnorm,
                                                  w_o, x, ctx, mod, norm_w3, w_gr[layer + 1], b_gr[layer + 1],
                                                  layer)
        else:
            out = _outproj_last_call(yssm, h_f, h_b, yb_lat, proj, d_row, w_glu, mnorm, w_o, h_prev, mod,
                                     final_norm_w.reshape(1, d), layer, nctx)
    return out
```

```python
import functools
import math

import jax
import jax.numpy as jnp
from jax import lax
from jax.experimental import pallas as pl
from jax.experimental.pallas import tpu as pltpu

F32 = jnp.float32
BF16 = jnp.bfloat16
EPS = 1e-6

SSM_W = 512
SSM_P = 16
SSM_G = 32
SSM_N = 64
S5_T = 16
S5_SLAB_G = 8
S5_SLABS = SSM_G // S5_SLAB_G
S5_ST = S5_SLAB_G * SSM_N
AH = 8
AKV = 2
AGRP = AH // AKV
HD = 128
AW = AH * HD
KVW = AKV * HD
GRID_W = 64
ROPE_THETA = 10000.0
MH = 4
MD = 128
MW = MH * MD
MCH = 128
C_U, C_Z, C_AQ, C_AK, C_AV, C_AZ, C_MQ, C_MK, C_MV, C_MO, C_MZ = (
    0, 512, 1024, 2048, 2304, 2560, 3584, 4096, 4608, 5120, 5632)
PROJ_W = 6144
MIX_W = SSM_W + AW + MW

ROW_TILE = 256
VMEM_LIMIT = 48 * 1024 * 1024


def _cparams(*sem):
    return pltpu.CompilerParams(dimension_semantics=sem, vmem_limit_bytes=VMEM_LIMIT)


def _dot(a, b):
    return jnp.dot(a, b, preferred_element_type=F32)


def _dot_nt(a, b):
    return lax.dot_general(a, b, (((1,), (1,)), ((), ())), preferred_element_type=F32)


def _dot_tn(a, b):
    return lax.dot_general(a, b, (((0,), (0,)), ((), ())), preferred_element_type=F32)


def _split3(x):
    hi = x.astype(BF16)
    r1 = x - hi.astype(F32)
    mid = r1.astype(BF16)
    lo = (r1 - mid.astype(F32)).astype(BF16)
    return hi, mid, lo


def _silu(x):
    return x * jax.nn.sigmoid(x)


def _log_sigmoid(x):
    return jnp.minimum(x, 0.0) - jnp.log1p(jnp.exp(-jnp.abs(x)))


def _rms(x, w):
    return x * lax.rsqrt(jnp.mean(x * x, axis=-1, keepdims=True) + EPS) * w


def _pick_tile(n, cap, mult):
    best = mult
    for t in range(mult, min(n, cap) + 1, mult):
        if n % t == 0:
            best = t
    return best


def _ada_kernel(c_ref, w_ref, b_ref, o_ref):
    a = _silu(c_ref[...])
    w = w_ref[0]
    a_hi = a.astype(BF16)
    a_lo = (a - a_hi.astype(F32)).astype(BF16)
    w_hi = w.astype(BF16)
    w_lo = (w - w_hi.astype(F32)).astype(BF16)
    o_ref[0] = _dot(a_hi, w_hi) + _dot(a_hi, w_lo) + _dot(a_lo, w_hi) + b_ref[0]


def _ada_call(c_rows, ada_w, ada_b):
    depth, d, n3 = ada_w.shape
    tn = 512
    return pl.pallas_call(
        _ada_kernel,
        grid=(depth, n3 // tn),
        in_specs=[
            pl.BlockSpec((8, d), lambda l, j: (0, 0)),
            pl.BlockSpec((1, d, tn), lambda l, j: (l, 0, j)),
            pl.BlockSpec((1, 1, tn), lambda l, j: (l, 0, j)),
        ],
        out_specs=pl.BlockSpec((1, 8, tn), lambda l, j: (l, 0, j)),
        out_shape=jax.ShapeDtypeStruct((depth, 8, n3), F32),
        compiler_params=_cparams("parallel", "parallel"),
        name="ada_mod",
    )(c_rows, ada_w, ada_b.reshape(depth, 1, n3))


def _prenorm_kernel(x_ref, ctx_ref, nw_ref, sh_ref, sc_ref, wgr_ref, bgr_ref, o_ref, gr_ref, *, nct):
    h = jnp.where(pl.program_id(1) < nct, ctx_ref[0], x_ref[0])
    xn = (_rms(h, nw_ref[0]) * (1.0 + sc_ref[0]) + sh_ref[0]).astype(BF16)
    o_ref[0] = xn
    for dirn in range(2):
        gr_ref[0, dirn] = _dot_nt(wgr_ref[dirn], xn) + bgr_ref[dirn][:, :1]


def _mod_spec(d, layer, part, nct, nb):
    def idx(b, i):
        return (layer * 8 + jnp.where(i < nct, nb, b), 0, part)
    return pl.BlockSpec((1, 1, d), idx)


def _prenorm_call(x, ctx, nw, mod, w_gr, b_gr, layer):
    bsz, n, d = x.shape
    nctx = ctx.shape[1]
    tr = ROW_TILE
    nct = nctx // tr
    s = nctx + n
    return pl.pallas_call(
        functools.partial(_prenorm_kernel, nct=nct),
        grid=(bsz, s // tr),
        in_specs=[
            pl.BlockSpec((1, tr, d), lambda b, i: (b, jnp.maximum(i - nct, 0), 0)),
            pl.BlockSpec((1, tr, d), lambda b, i: (b, jnp.minimum(i, nct - 1), 0)),
            pl.BlockSpec((1, 1, d), lambda b, i: (layer, 0, 0)),
            _mod_spec(d, layer, 0, nct, bsz),
            _mod_spec(d, layer, 1, nct, bsz),
            pl.BlockSpec((2, 8, d), lambda b, i: (0, 0, 0)),
            pl.BlockSpec((2, 8, 128), lambda b, i: (0, 0, 0)),
        ],
        out_specs=[
            pl.BlockSpec((1, tr, d), lambda b, i: (b, i, 0)),
            pl.BlockSpec((1, 2, 8, tr), lambda b, i: (b, 0, 0, i)),
        ],
        out_shape=[
            jax.ShapeDtypeStruct((bsz, s, d), BF16),
            jax.ShapeDtypeStruct((bsz, 2, 8, s), F32),
        ],
        compiler_params=_cparams("parallel", "parallel"),
        name="prenorm",
    )(x, ctx, nw, mod, mod, w_gr, b_gr)


def _matmul_kernel(x_ref, w_ref, o_ref):
    o_ref[...] = _dot(x_ref[...], w_ref[...])


def _inproj_call(xn2d, w):
    m, d = xn2d.shape
    n = w.shape[1]
    tm = _pick_tile(m, 1056, 16)
    tn = 1024
    return pl.pallas_call(
        _matmul_kernel,
        grid=(n // tn, m // tm),
        in_specs=[
            pl.BlockSpec((tm, d), lambda j, i: (i, 0)),
            pl.BlockSpec((d, tn), lambda j, i: (0, j)),
        ],
        out_specs=pl.BlockSpec((tm, tn), lambda j, i: (i, j)),
        out_shape=jax.ShapeDtypeStruct((m, n), F32),
        compiler_params=_cparams("parallel", "parallel"),
        name="in_proj",
    )(xn2d, w)


def _s5_weights(a_re, a_im, log_dt, b_re, b_im, c_re, c_im):
    t_ = S5_T
    a_re = a_re.astype(F32)
    a_im = a_im.astype(F32)
    dt = jnp.exp(log_dt.astype(F32))[..., None]
    mag = jnp.exp(a_re * dt)
    lam_re = mag * jnp.cos(a_im * dt)
    lam_im = mag * jnp.sin(a_im * dt)
    inv_abs2 = 1.0 / (a_re * a_re + a_im * a_im)
    num_re, num_im = lam_re - 1.0, lam_im
    f_re = (num_re * a_re + num_im * a_im) * inv_abs2
    f_im = (num_im * a_re - num_re * a_im) * inv_abs2
    b_re = b_re.astype(F32)[None]
    b_im = b_im.astype(F32)[None]
    bb_re = f_re[:, :, None, :] * b_re - f_im[:, :, None, :] * b_im
    bb_im = f_re[:, :, None, :] * b_im + f_im[:, :, None, :] * b_re
    c_re = c_re.astype(F32)
    c_im = c_im.astype(F32)

    pr = [jnp.ones_like(lam_re)]
    pi = [jnp.zeros_like(lam_im)]
    for _ in range(t_):
        pr.append(pr[-1] * lam_re - pi[-1] * lam_im)
        pi.append(pr[-2] * lam_im + pi[-1] * lam_re)

    def powers(exps_f, exps_b):
        re = jnp.stack([jnp.stack([pr[e][0] for e in exps_f]), jnp.stack([pr[e][1] for e in exps_b])])
        im = jnp.stack([jnp.stack([pi[e][0] for e in exps_f]), jnp.stack([pi[e][1] for e in exps_b])])
        return re[:, :, :, None, :], im[:, :, :, None, :]

    def slab_rows(w):
        return w.reshape(2, t_, S5_SLABS, 128, 2 * SSM_N).astype(BF16)

    steps = list(range(t_))
    p_re, p_im = powers([t_ - 1 - t for t in steps], steps)
    win = slab_rows(jnp.concatenate([bb_re[:, None] * p_re - bb_im[:, None] * p_im,
                                     bb_re[:, None] * p_im + bb_im[:, None] * p_re], axis=-1))
    p_re, p_im = powers([t + 1 for t in steps], [t_ - t for t in steps])
    wout = slab_rows(jnp.concatenate([c_re[:, None] * p_re - c_im[:, None] * p_im,
                                      -(c_re[:, None] * p_im + c_im[:, None] * p_re)], axis=-1))

    p_re, p_im = powers(steps, steps)
    kr = c_re[:, None] * p_re - c_im[:, None] * p_im
    ki = c_re[:, None] * p_im + c_im[:, None] * p_re
    kk = jnp.einsum('dgqm,dtgpm->dgqtp', jnp.concatenate([bb_re, bb_im], axis=-1),
                    jnp.concatenate([kr, -ki], axis=-1), precision=lax.Precision.HIGH)
    k_f = kk[0].reshape(S5_SLABS, 128, t_ * SSM_P)
    k_b = kk[1][:, :, ::-1, :].reshape(S5_SLABS, 128, t_ * SSM_P)

    def lam_row(v):
        return v.reshape(2, S5_SLABS, 1, S5_ST)

    lam_t = jnp.concatenate([lam_row(pr[t_]), lam_row(pi[t_])], axis=-1)
    return win, wout, k_f, k_b, lam_t


def _iota(shape, dim):
    return lax.broadcasted_iota(jnp.int32, shape, dim)


def _s5_core_kernel(u_ref, win_ref, wout_ref, kf_ref, kb_ref, lam_ref, y_ref, lhs_sc, w_sc, s_sc, acc_sc,
                    *, nch, nctx_ch):
    t_ = S5_T
    n2 = 2 * SSM_N
    st2 = 2 * S5_ST
    lg_p = SSM_P.bit_length() - 1
    lg_n = SSM_N.bit_length() - 1
    lg_st = S5_ST.bit_length() - 1

    for t in range(t_):
        lhs_sc[:, t * 128:(t + 1) * 128] = u_ref[0, pl.ds(t, nch, stride=t_), :].astype(BF16)

    k1, c1 = _iota((n2, st2), 0), _iota((n2, st2), 1)
    e_in = jnp.where(((k1 >> lg_n) == (c1 >> lg_st)) & ((k1 & (SSM_N - 1)) == (c1 & (SSM_N - 1))),
                     1.0, 0.0).astype(BF16)
    r1, c1b = _iota((128, st2), 0), _iota((128, st2), 1)
    m_in = (r1 >> lg_p) == ((c1b & (S5_ST - 1)) >> lg_n)

    def expand_state_table(tab_ref, d):
        for t in range(t_):
            rows = slice(t * 128, (t + 1) * 128)
            w_sc[rows, :st2] = jnp.where(m_in, _dot(tab_ref[d, t, 0], e_in), 0.0).astype(BF16)

    for d in range(2):
        expand_state_table(win_ref, d)
        s_sc[d] = _dot(lhs_sc[...], w_sc[:, :st2])

    lam_f = lam_ref[0, 0]
    lam_b = lam_ref[1, 0]
    lrf, lif = lam_f[:, :S5_ST], lam_f[:, S5_ST:]
    lrb, lib = lam_b[:, :S5_ST], lam_b[:, S5_ST:]

    def step(k, carry):
        hrf, hif, hrb, hib = carry
        cb = jnp.where(k < nctx_ch, nctx_ch - 1 - k, nch - 1 - (k - nctx_ch))
        sf = s_sc[0, pl.ds(k, 1), :]
        sb = s_sc[1, pl.ds(cb, 1), :]
        s_sc[0, pl.ds(k, 1), :] = jnp.concatenate([hrf, hif], axis=-1)
        s_sc[1, pl.ds(cb, 1), :] = jnp.concatenate([hrb, hib], axis=-1)
        return (lrf * hrf - lif * hif + sf[:, :S5_ST], lrf * hif + lif * hrf + sf[:, S5_ST:],
                lrb * hrb - lib * hib + sb[:, :S5_ST], lrb * hib + lib * hrb + sb[:, S5_ST:])

    zero = jnp.zeros((1, S5_ST), F32)
    lax.fori_loop(0, nch, step, (zero, zero, zero, zero))

    k2, c2 = _iota((t_ * SSM_P, t_ * 128), 0), _iota((t_ * SSM_P, t_ * 128), 1)
    e_m = jnp.where(((k2 >> lg_p) == (c2 >> 7)) & ((k2 & (SSM_P - 1)) == (c2 & (SSM_P - 1))),
                    1.0, 0.0).astype(BF16)
    r2, c2b = _iota((128, t_ * 128), 0), _iota((128, t_ * 128), 1)
    m_m = (r2 >> lg_p) == ((c2b & 127) >> lg_p)
    k_f = kf_ref[0]
    k_b = kb_ref[0]
    lag_col = _iota((128, t_ * SSM_P), 1)
    for t in range(t_):
        rows = slice(t * 128, (t + 1) * 128)
        fwd = k_f if t == 0 else pltpu.roll(k_f, t * SSM_P, 1)
        bwd = k_b if t == t_ - 1 else pltpu.roll(k_b, (t + 1) * SSM_P, 1)
        mc = (jnp.where(lag_col >= t * SSM_P, fwd, 0.0)
              + jnp.where(lag_col < (t + 1) * SSM_P, bwd, 0.0)).astype(BF16)
        w_sc[rows, :] = jnp.where(m_m, _dot(mc, e_m), 0.0).astype(BF16)
    acc_sc[...] = _dot(lhs_sc[...], w_sc[...])

    for d in range(2):
        expand_state_table(wout_ref, d)
        acc_sc[...] += _dot_nt(s_sc[d].astype(BF16), w_sc[:, :st2])

    for t in range(t_):
        y_ref[0, pl.ds(t, nch, stride=t_), :] = acc_sc[:, t * 128:(t + 1) * 128]


def _s5_core_call(proj, win, wout, k_f, k_b, lam_t, nctx):
    bsz, s, _ = proj.shape
    t_ = S5_T
    nch = s // t_
    kdim = t_ * 128
    n2 = 2 * SSM_N
    st2 = 2 * S5_ST
    cu = C_U // 128
    return pl.pallas_call(
        functools.partial(_s5_core_kernel, nch=nch, nctx_ch=nctx // t_),
        grid=(S5_SLABS, bsz),
        in_specs=[
            pl.BlockSpec((1, s, 128), lambda sl, b: (b, 0, cu + sl)),
            pl.BlockSpec((2, t_, 1, 128, n2), lambda sl, b: (0, 0, sl, 0, 0)),
            pl.BlockSpec((2, t_, 1, 128, n2), lambda sl, b: (0, 0, sl, 0, 0)),
            pl.BlockSpec((1, 128, t_ * SSM_P), lambda sl, b: (sl, 0, 0)),
            pl.BlockSpec((1, 128, t_ * SSM_P), lambda sl, b: (sl, 0, 0)),
            pl.BlockSpec((2, 1, 1, st2), lambda sl, b: (0, sl, 0, 0)),
        ],
        out_specs=pl.BlockSpec((1, s, 128), lambda sl, b: (b, 0, sl)),
        out_shape=jax.ShapeDtypeStruct((bsz, s, SSM_W), F32),
        scratch_shapes=[
            pltpu.VMEM((nch, kdim), BF16),
            pltpu.VMEM((kdim, kdim), BF16),
            pltpu.VMEM((2, nch, st2), F32),
            pltpu.VMEM((nch, kdim), F32),
        ],
        compiler_params=pltpu.CompilerParams(dimension_semantics=("parallel", "parallel"),
                                             vmem_limit_bytes=56 * 1024 * 1024),
        name="s5_core",
    )(proj, win, wout, k_f, k_b, lam_t)


def _s5_branch(proj, p, nctx):
    win, wout, k_f, k_b, lam_t = _s5_weights(p['a_re'], p['a_im'], p['log_dt'], p['b_re'], p['b_im'],
                                             p['c_re'], p['c_im'])
    return _s5_core_call(proj, win, wout, k_f, k_b, lam_t, nctx)


def _rope_tables(n_lat, nctx):
    rows = n_lat // GRID_W
    row = jnp.broadcast_to(jnp.arange(rows, dtype=F32)[:, None], (rows, GRID_W)).reshape(-1)
    col = jnp.broadcast_to(jnp.arange(GRID_W, dtype=F32)[None, :], (rows, GRID_W)).reshape(-1)
    n_freq = HD // 4
    inv = ROPE_THETA ** (-jnp.arange(n_freq, dtype=F32) / n_freq)
    ang = jnp.concatenate([row[:, None] * inv, col[:, None] * inv], axis=-1)
    cs, sn = jnp.cos(ang), jnp.sin(ang)
    cos2 = jnp.concatenate([cs, cs], axis=-1)
    sin2 = jnp.concatenate([-sn, sn], axis=-1)
    cos_all = jnp.concatenate([jnp.ones((nctx, HD), F32), cos2], axis=0)
    sin_all = jnp.concatenate([jnp.zeros((nctx, HD), F32), sin2], axis=0)
    return cos_all, sin_all, cos2, sin2


def _norm_rope(x, w, cs, sn):
    y = _rms(x, w)
    return y * cs + pltpu.roll(y, HD // 2, 1) * sn


def _attn_kv_prep_kernel(k_ref, v_ref, cos_ref, sin_ref, kw_ref, ko_ref, vo_ref):
    cs = cos_ref[...]
    sn = sin_ref[...]
    k = k_ref[0]
    for h in range(AKV):
        ko_ref[0, :, h * HD:(h + 1) * HD] = _norm_rope(k[:, h * HD:(h + 1) * HD], kw_ref[...], cs, sn).astype(BF16)
    vo_ref[0] = v_ref[0].astype(BF16)


def _attn_kv_prep_call(proj, cos2, sin2, kw):
    bsz, s, _ = proj.shape
    tr = ROW_TILE
    return pl.pallas_call(
        _attn_kv_prep_kernel,
        grid=(bsz, s // tr),
        in_specs=[
            pl.BlockSpec((1, tr, KVW), lambda b, i: (b, i, C_AK // KVW)),
            pl.BlockSpec((1, tr, KVW), lambda b, i: (b, i, C_AV // KVW)),
            pl.BlockSpec((tr, HD), lambda b, i: (i, 0)),
            pl.BlockSpec((tr, HD), lambda b, i: (i, 0)),
            pl.BlockSpec((1, HD), lambda b, i: (0, 0)),
        ],
        out_specs=[
            pl.BlockSpec((1, tr, KVW), lambda b, i: (b, i, 0)),
            pl.BlockSpec((1, tr, KVW), lambda b, i: (b, i, 0)),
        ],
        out_shape=[
            jax.ShapeDtypeStruct((bsz, s, KVW), BF16),
            jax.ShapeDtypeStruct((bsz, s, KVW), BF16),
        ],
        compiler_params=_cparams("parallel", "parallel"),
        name="attn_kv_prep",
    )(proj, proj, cos2, sin2, kw)


Q_SCALE = HD ** -0.5 * math.log2(math.e)


def _attn_finish(acc, l, z_refs, o_ref, tq):
    o = acc / l
    ntile = len(z_refs)
    for h in range(AGRP):
        for t, z_ref in enumerate(z_refs):
            r0 = (h * ntile + t) * tq
            z = z_ref[0][:, h * HD:(h + 1) * HD]
            o_ref[0, t * tq:(t + 1) * tq, h * HD:(h + 1) * HD] = (o[r0:r0 + tq] * _silu(z)).astype(BF16)


def _attn_ctx_kernel(q_ref, z_ref, qw_ref, k_ref, v_ref, o_ref, *, tq):
    q = q_ref[0]
    q4 = jnp.concatenate([(_rms(q[:, h * HD:(h + 1) * HD], qw_ref[...]) * Q_SCALE).astype(BF16)
                          for h in range(AGRP)], axis=0)
    s = _dot_nt(q4, k_ref[0])
    m = jnp.max(s, axis=-1, keepdims=True)
    p = jnp.exp2(s - m)
    l = jnp.sum(p, axis=-1, keepdims=True)
    _attn_finish(_dot(p.astype(BF16), v_ref[0]), l, [z_ref], o_ref, tq)


def _attn_ctx_call(proj, ks, vs, qw, nctx):
    bsz = proj.shape[0]
    tq = ROW_TILE
    zw = AGRP * HD
    return pl.pallas_call(
        functools.partial(_attn_ctx_kernel, tq=tq),
        grid=(bsz, AKV, nctx // tq),
        in_specs=[
            pl.BlockSpec((1, tq, zw), lambda b, g, i: (b, i, C_AQ // zw + g)),
            pl.BlockSpec((1, tq, zw), lambda b, g, i: (b, i, C_AZ // zw + g)),
            pl.BlockSpec((1, HD), lambda b, g, i: (0, 0)),
            pl.BlockSpec((1, nctx, HD), lambda b, g, i: (b, 0, g)),
            pl.BlockSpec((1, nctx, HD), lambda b, g, i: (b, 0, g)),
        ],
        out_specs=pl.BlockSpec((1, tq, zw), lambda b, g, i: (b, i, g)),
        out_shape=jax.ShapeDtypeStruct((bsz, nctx, AW), BF16),
        compiler_params=_cparams("parallel", "parallel", "parallel"),
        name="attention_ctx",
    )(proj, proj, qw, ks, vs)


def _attn_lat_kernel(q_ref, qn_ref, z_ref, cos_ref, sin_ref, cosn_ref, sinn_ref, qw_ref, k_ref, v_ref,
                     o_ref, q_sc, s_sc, *, tq, ck, nck):
    i = pl.program_id(2)

    def prep(qr, cr, sr):
        q = qr[0]
        return jnp.concatenate(
            [(_norm_rope(q[:, h * HD:(h + 1) * HD], qw_ref[...], cr[...], sr[...]) * Q_SCALE).astype(BF16)
             for h in range(AGRP)], axis=0)

    base = (i * nck) % 2 if nck % 2 else 0

    def slot(j):
        return (base + j) % 2

    @pl.when(i == 0)
    def _():
        q_sc[...] = prep(q_ref, cos_ref, sin_ref)
        s_sc[slot(0)] = _dot_nt(q_sc[...], k_ref[0, 0:ck, :])

    m = l = acc = None
    for j in range(nck):
        if j + 1 < nck:
            s_sc[slot(j + 1)] = _dot_nt(q_sc[...], k_ref[0, (j + 1) * ck:(j + 2) * ck, :])
        else:
            q_sc[...] = prep(qn_ref, cosn_ref, sinn_ref)
            s_sc[slot(j + 1)] = _dot_nt(q_sc[...], k_ref[0, 0:ck, :])
        s = s_sc[slot(j)]
        s_max = jnp.max(s, axis=-1, keepdims=True)
        m_new = s_max if j == 0 else jnp.maximum(m, s_max)
        p = jnp.exp2(s - m_new)
        pv = _dot(p.astype(BF16), v_ref[0, j * ck:(j + 1) * ck, :])
        p_sum = jnp.sum(p, axis=-1, keepdims=True)
        if j == 0:
            l, acc = p_sum, pv
        else:
            alpha = jnp.exp2(m - m_new)
            l = alpha * l + p_sum
            acc = alpha * acc + pv
        m = m_new
    _attn_finish(acc, l, [z_ref], o_ref, tq)


def _attn_lat_call(proj, ks, vs, cos_lat, sin_lat, qw, nctx):
    bsz, s, _ = proj.shape
    n_lat = s - nctx
    tq = ROW_TILE
    nct = nctx // tq
    ck = _pick_tile(s, 1536, 128)
    nck = s // ck
    zw = AGRP * HD
    cq = C_AQ // zw
    cz = C_AZ // zw
    nlt = n_lat // tq

    def nxt(i):
        return jnp.minimum(i + 1, nlt - 1)

    return pl.pallas_call(
        functools.partial(_attn_lat_kernel, tq=tq, ck=ck, nck=nck),
        grid=(bsz, AKV, nlt),
        in_specs=[
            pl.BlockSpec((1, tq, zw), lambda b, g, i: (b, nct + i, cq + g)),
            pl.BlockSpec((1, tq, zw), lambda b, g, i: (b, nct + nxt(i), cq + g)),
            pl.BlockSpec((1, tq, zw), lambda b, g, i: (b, nct + i, cz + g)),
            pl.BlockSpec((tq, HD), lambda b, g, i: (i, 0)),
            pl.BlockSpec((tq, HD), lambda b, g, i: (i, 0)),
            pl.BlockSpec((tq, HD), lambda b, g, i: (nxt(i), 0)),
            pl.BlockSpec((tq, HD), lambda b, g, i: (nxt(i), 0)),
            pl.BlockSpec((1, HD), lambda b, g, i: (0, 0)),
            pl.BlockSpec((1, s, HD), lambda b, g, i: (b, 0, g)),
            pl.BlockSpec((1, s, HD), lambda b, g, i: (b, 0, g)),
        ],
        out_specs=pl.BlockSpec((1, tq, zw), lambda b, g, i: (b, i, g)),
        out_shape=jax.ShapeDtypeStruct((bsz, n_lat, AW), BF16),
        scratch_shapes=[pltpu.VMEM((AGRP * tq, HD), BF16), pltpu.VMEM((2, AGRP * tq, ck), F32)],
        compiler_params=_cparams("parallel", "parallel", "arbitrary"),
        name="attention",
    )(proj, proj, proj, cos_lat, sin_lat, cos_lat, sin_lat, qw, ks, vs)


def _mlstm_prep_kernel(q_ref, k_ref, qp_ref, kp_ref, qn_ref, kn_ref, w_ref, b_ref, qo_ref, ko_ref,
                       *, tr, nct, nt):
    i = pl.program_id(1)
    first = jnp.logical_or(i == 0, i == nct)
    last = jnp.logical_or(i == nct - 1, i == nt - 1)
    rid = lax.broadcasted_iota(jnp.int32, (tr, MW), 0)
    w = w_ref[...]
    bias = b_ref[...]

    def conv(x, prev_row, next_row, off):
        prev_row = jnp.where(first, 0.0, prev_row)
        next_row = jnp.where(last, 0.0, next_row)
        xp = jnp.where(rid == 0, prev_row, pltpu.roll(x, 1, 0))
        xn = jnp.where(rid == tr - 1, next_row, pltpu.roll(x, tr - 1, 0))
        y = (w[0:1, off:off + MW] * xp + w[1:2, off:off + MW] * x + w[2:3, off:off + MW] * xn
             + bias[:, off:off + MW])
        return _silu(y)

    qo_ref[0] = conv(q_ref[0], qp_ref[0, 7:8, :], qn_ref[0, 0:1, :], 0).astype(BF16)
    ko_ref[0] = (conv(k_ref[0], kp_ref[0, 7:8, :], kn_ref[0, 0:1, :], MW) * (MD ** -0.5)).astype(BF16)


def _mlstm_prep_call(proj, conv_w, conv_b, nctx):
    bsz, s, _ = proj.shape
    tr = ROW_TILE
    nt = s // tr
    nct = nctx // tr
    r8 = tr // 8
    n8 = s // 8
    cq = C_MQ // MW
    ck = C_MK // MW

    def prev_map(c):
        return lambda b, i: (b, jnp.maximum(i * r8 - 1, 0), c)

    def next_map(c):
        return lambda b, i: (b, jnp.minimum((i + 1) * r8, n8 - 1), c)

    return pl.pallas_call(
        functools.partial(_mlstm_prep_kernel, tr=tr, nct=nct, nt=nt),
        grid=(bsz, nt),
        in_specs=[
            pl.BlockSpec((1, tr, MW), lambda b, i: (b, i, cq)),
            pl.BlockSpec((1, tr, MW), lambda b, i: (b, i, ck)),
            pl.BlockSpec((1, 8, MW), prev_map(cq)),
            pl.BlockSpec((1, 8, MW), prev_map(ck)),
            pl.BlockSpec((1, 8, MW), next_map(cq)),
            pl.BlockSpec((1, 8, MW), next_map(ck)),
            pl.BlockSpec((8, 2 * MW), lambda b, i: (0, 0)),
            pl.BlockSpec((1, 2 * MW), lambda b, i: (0, 0)),
        ],
        out_specs=[
            pl.BlockSpec((1, tr, MW), lambda b, i: (b, i, 0)),
            pl.BlockSpec((1, tr, MW), lambda b, i: (b, i, 0)),
        ],
        out_shape=[
            jax.ShapeDtypeStruct((bsz, s, MW), BF16),
            jax.ShapeDtypeStruct((bsz, s, MW), BF16),
        ],
        compiler_params=_cparams("parallel", "parallel"),
        name="mlstm_prep",
    )(proj, proj, proj, proj, proj, proj, conv_w, conv_b)


def _mlstm_bwd_chunk(step, nctx_ch, nch):
    return jnp.where(step < nctx_ch, nctx_ch - 1 - step, nch - 1 - (step - nctx_ch))


def _hi_mid(x):
    hi = x.astype(BF16)
    mid = (x - hi.astype(F32)).astype(BF16)
    return jnp.concatenate([hi, mid], axis=1)


def _mlstm_chunk(dirn, q, k, v, gr, o_ref, b, ct_sc, n_sc, m_sc, base):
    row = _iota((MCH, MCH), 0)
    col = _iota((MCH, MCH), 1)
    allowed = (col <= row) if dirn == 0 else (col >= row)
    eye = col == row
    tri = jnp.where(allowed, 1.0, 0.0).astype(BF16)
    lf_r = _log_sigmoid(gr)
    r3 = _split3(lf_r)
    b_r = _dot_nt(r3[0], tri) + _dot_nt(r3[1], tri) + _dot_nt(r3[2], tri)

    a8 = jnp.concatenate([gr[:MH] - b_r[MH:], gr[:MH] - b_r[MH:]], axis=0)
    lane = _iota((8, MCH), 1)
    cm8 = a8
    sh = 1
    while sh < MCH:
        if dirn == 0:
            cm8 = jnp.maximum(cm8, jnp.where(lane >= sh, pltpu.roll(cm8, sh, 1), -jnp.inf))
        else:
            cm8 = jnp.maximum(cm8, jnp.where(lane < MCH - sh, pltpu.roll(cm8, MCH - sh, 1), -jnp.inf))
        sh *= 2

    ones2 = jnp.ones((2 * MCH, MCH), BF16)
    lhs = []
    for h in range(MH):
        lhs += [_hi_mid(jnp.where(allowed, lf_r[MH + h:MH + h + 1, :], 0.0)),
                _hi_mid(jnp.where(eye, cm8[h:h + 1, :], 0.0)),
                _hi_mid(jnp.where(eye, gr[h:h + 1, :], 0.0))]
    rep = _dot(jnp.concatenate(lhs, axis=0), ones2)

    heads = []
    lhs2 = []
    for h in range(MH):
        st = base + h
        bc = rep[(3 * h) * MCH:(3 * h + 1) * MCH]
        cm = rep[(3 * h + 1) * MCH:(3 * h + 2) * MCH]
        lic = rep[(3 * h + 2) * MCH:(3 * h + 3) * MCH]
        br = b_r[MH + h:MH + h + 1, :]
        li_r = gr[h:h + 1, :]
        m_row = m_sc[st][:1, :]
        qh = q[:, h * MD:(h + 1) * MD]
        kh = k[:, h * MD:(h + 1) * MD]

        log_d = jnp.where(allowed, bc - br + li_r, -jnp.inf)
        m_t = bc + jnp.maximum(m_row, cm)
        s = _dot_nt(qh, kh) * jnp.exp(log_d - m_t)
        s_bf = s.astype(BF16)
        lhs2 += [s_bf, (qh.astype(F32) * n_sc[st][:1, :]).astype(BF16)]
        heads.append((st, bc, lic, m_row, m_t, qh, kh, s_bf))

    rep2 = _dot(jnp.concatenate(lhs2, axis=0), jnp.ones((MCH, MCH), BF16))

    for h, (st, bc, lic, m_row, m_t, qh, kh, s_bf) in enumerate(heads):
        vh = v[:, h * MD:(h + 1) * MD]
        s_sum = rep2[(2 * h) * MCH:(2 * h + 1) * MCH]
        qn = rep2[(2 * h + 1) * MCH:(2 * h + 2) * MCH]
        w_inter = jnp.exp(bc + m_row - m_t)
        num = _dot(s_bf, vh) + w_inter * _dot(qh, ct_sc[st].astype(BF16))
        den = s_sum + w_inter * qn
        o_ref[b, :, h * MD:(h + 1) * MD] = num / jnp.maximum(jnp.abs(den), jnp.exp(-m_t))

        b_last = bc[MCH - 1:MCH, :] if dirn == 0 else bc[0:1, :]
        log_w = b_last - bc + lic
        m_new = jnp.maximum(b_last + m_row, jnp.max(log_w, axis=0, keepdims=True))
        decay = jnp.exp(b_last + m_row - m_new)
        kw = kh.astype(F32) * jnp.exp(log_w - m_new)
        ct_sc[st] = decay * ct_sc[st] + _dot_tn(kw.astype(BF16), vh)
        n_sc[st] = jnp.broadcast_to(decay * n_sc[st][:1, :] + jnp.sum(kw, axis=0, keepdims=True), (8, MD))
        m_sc[st] = jnp.broadcast_to(m_new, (8, 128))


def _mlstm_kernel(qf_ref, kf_ref, vf_ref, grf_ref, qb_ref, kb_ref, vb_ref, grb_ref,
                  of_ref, ob_ref, ct_sc, n_sc, m_sc, *, bsz):
    @pl.when(pl.program_id(0) == 0)
    def _():
        ct_sc[...] = jnp.zeros_like(ct_sc)
        n_sc[...] = jnp.zeros_like(n_sc)
        m_sc[...] = jnp.zeros_like(m_sc)

    for b in range(bsz):
        _mlstm_chunk(0, qf_ref[b], kf_ref[b], vf_ref[b].astype(BF16), grf_ref[b, 0],
                     of_ref, b, ct_sc, n_sc, m_sc, b * MH)
        _mlstm_chunk(1, qb_ref[b], kb_ref[b], vb_ref[b].astype(BF16), grb_ref[b, 0],
                     ob_ref, b, ct_sc, n_sc, m_sc, (bsz + b) * MH)


def _mlstm_call(qc, kc, proj, g_row, nctx):
    bsz, s, _ = qc.shape
    nch = s // MCH
    nctx_ch = nctx // MCH
    cv = C_MV // MW

    def cb(st):
        return _mlstm_bwd_chunk(st, nctx_ch, nch)

    nst = 2 * bsz * MH
    return pl.pallas_call(
        functools.partial(_mlstm_kernel, bsz=bsz),
        grid=(nch,),
        in_specs=[
            pl.BlockSpec((bsz, MCH, MW), lambda st: (0, st, 0)),
            pl.BlockSpec((bsz, MCH, MW), lambda st: (0, st, 0)),
            pl.BlockSpec((bsz, MCH, MW), lambda st: (0, st, cv)),
            pl.BlockSpec((bsz, 1, 8, MCH), lambda st: (0, 0, 0, st)),
            pl.BlockSpec((bsz, MCH, MW), lambda st: (0, cb(st), 0)),
            pl.BlockSpec((bsz, MCH, MW), lambda st: (0, cb(st), 0)),
            pl.BlockSpec((bsz, MCH, MW), lambda st: (0, cb(st), cv)),
            pl.BlockSpec((bsz, 1, 8, MCH), lambda st: (0, 1, 0, cb(st))),
        ],
        out_specs=[
            pl.BlockSpec((bsz, MCH, MW), lambda st: (0, st, 0)),
            pl.BlockSpec((bsz, MCH, MW), lambda st: (0, cb(st), 0)),
        ],
        out_shape=[
            jax.ShapeDtypeStruct((bsz, s, MW), F32),
            jax.ShapeDtypeStruct((bsz, s, MW), F32),
        ],
        scratch_shapes=[
            pltpu.VMEM((nst, MD, MD), F32),
            pltpu.VMEM((nst, 8, MD), F32),
            pltpu.VMEM((nst, 8, 128), F32),
        ],
        compiler_params=_cparams("arbitrary"),
        name="mlstm_scan",
    )(qc, kc, proj, g_row, qc, kc, proj, g_row)


def _mixer_specs(tr, row_of):
    def tok(col_block):
        return pl.BlockSpec((1, tr, SSM_W), lambda b, i: (b, row_of(i), col_block))
    const = lambda shape: pl.BlockSpec(shape, lambda b, i: (0,) * len(shape))
    return [tok(0), tok(C_U // SSM_W), tok(C_Z // SSM_W), const((1, SSM_W)), const((SSM_W, SSM_W)),
            tok(0), tok(0), tok(C_MO // MW), tok(C_MZ // MW), const((1, MW))]


def _mixer_outputs(y_ref, u_ref, sz_ref, d_ref, wg_ref, hf_ref, hb_ref, mo_ref, mz_ref, mn_ref):
    g = jax.nn.gelu(y_ref[0] + d_ref[...] * u_ref[0], approximate=True)
    ya = (g * jax.nn.sigmoid(_dot(g.astype(BF16), wg_ref[...])) * _silu(sz_ref[0])).astype(BF16)
    hs = hf_ref[0] + hb_ref[0]
    og, zg, nw = mo_ref[0], mz_ref[0], mn_ref[...]
    yc = []
    for h in range(MH):
        sl = slice(h * MD, (h + 1) * MD)
        yc.append((jax.nn.sigmoid(og[:, sl]) * _rms(hs[:, sl], nw[:, sl]) * _silu(zg[:, sl])).astype(BF16))
    return ya, jnp.concatenate(yc, axis=1)


def _mix_dot(ya, yb, yc, w_ref):
    return (_dot(ya, w_ref[:SSM_W, :]) + _dot(yb, w_ref[SSM_W:SSM_W + AW, :])
            + _dot(yc, w_ref[SSM_W + AW:, :]))


def _outproj_mid_kernel(*refs, nct):
    mixer, (ybc_ref, ybl_ref, w_ref, x_ref, ctx_ref, g_ref, nw_ref, sh_ref, sc_ref,
            wgr_ref, bgr_ref, h_ref, xn_ref, gr_ref) = refs[:10], refs[10:]
    is_ctx = pl.program_id(1) < nct
    ya, yc = _mixer_outputs(*mixer)
    yb = jnp.where(is_ctx, ybc_ref[0], ybl_ref[0])
    mix = _mix_dot(ya, yb, yc, w_ref)
    h = jnp.where(is_ctx, ctx_ref[0], x_ref[0]) + g_ref[0] * mix
    h_ref[0] = h
    xn = (_rms(h, nw_ref[0]) * (1.0 + sc_ref[0]) + sh_ref[0]).astype(BF16)
    xn_ref[0] = xn
    for dirn in range(2):
        gr_ref[0, dirn] = _dot_nt(wgr_ref[dirn], xn) + bgr_ref[dirn][:, :1]


def _outproj_mid_call(yssm, h_f, h_b, yb_ctx, yb_lat, proj, d_row, w_glu, mnorm, w_out, x, ctx, mod, nw,
                      w_gr, b_gr, layer):
    bsz, s, _ = yssm.shape
    d = x.shape[-1]
    nctx = ctx.shape[1]
    tr = ROW_TILE
    nct = nctx // tr
    nxt = layer + 1
    lat = lambda b, i: (b, jnp.maximum(i - nct, 0), 0)
    cxt = lambda b, i: (b, jnp.minimum(i, nct - 1), 0)
    return pl.pallas_call(
        functools.partial(_outproj_mid_kernel, nct=nct),
        grid=(bsz, s // tr),
        in_specs=_mixer_specs(tr, lambda i: i) + [
            pl.BlockSpec((1, tr, AW), cxt),
            pl.BlockSpec((1, tr, AW), lat),
            pl.BlockSpec((MIX_W, d), lambda b, i: (0, 0)),
            pl.BlockSpec((1, tr, d), lat),
            pl.BlockSpec((1, tr, d), cxt),
            _mod_spec(d, layer, 2, nct, bsz),
            pl.BlockSpec((1, 1, d), lambda b, i: (nxt, 0, 0)),
            _mod_spec(d, nxt, 0, nct, bsz),
            _mod_spec(d, nxt, 1, nct, bsz),
            pl.BlockSpec((2, 8, d), lambda b, i: (0, 0, 0)),
            pl.BlockSpec((2, 8, 128), lambda b, i: (0, 0, 0)),
        ],
        out_specs=[
            pl.BlockSpec((1, tr, d), lambda b, i: (b, i, 0)),
            pl.BlockSpec((1, tr, d), lambda b, i: (b, i, 0)),
            pl.BlockSpec((1, 2, 8, tr), lambda b, i: (b, 0, 0, i)),
        ],
        out_shape=[
            jax.ShapeDtypeStruct((bsz, s, d), F32),
            jax.ShapeDtypeStruct((bsz, s, d), BF16),
            jax.ShapeDtypeStruct((bsz, 2, 8, s), F32),
        ],
        compiler_params=_cparams("parallel", "parallel"),
        name="out_proj_mid",
    )(yssm, proj, proj, d_row, w_glu, h_f, h_b, proj, proj, mnorm,
      yb_ctx, yb_lat, w_out, x, ctx, mod, nw, mod, mod, w_gr, b_gr)


def _outproj_last_kernel(*refs):
    mixer, (ybl_ref, w_ref, h_ref, g_ref, nw_ref, o_ref) = refs[:10], refs[10:]
    ya, yc = _mixer_outputs(*mixer)
    mix = _mix_dot(ya, ybl_ref[0], yc, w_ref)
    o_ref[0] = _rms(h_ref[0] + g_ref[0] * mix, nw_ref[...])


def _outproj_last_call(yssm, h_f, h_b, yb_lat, proj, d_row, w_glu, mnorm, w_out, h_prev, mod, final_w,
                       layer, nctx):
    bsz, s, d = h_prev.shape
    tr = ROW_TILE
    nct = nctx // tr
    n_lat = s - nctx
    return pl.pallas_call(
        _outproj_last_kernel,
        grid=(bsz, n_lat // tr),
        in_specs=_mixer_specs(tr, lambda i: i + nct) + [
            pl.BlockSpec((1, tr, AW), lambda b, i: (b, i, 0)),
            pl.BlockSpec((MIX_W, d), lambda b, i: (0, 0)),
            pl.BlockSpec((1, tr, d), lambda b, i: (b, i + nct, 0)),
            pl.BlockSpec((1, 1, d), lambda b, i: (layer * 8 + b, 0, 2)),
            pl.BlockSpec((1, d), lambda b, i: (0, 0)),
        ],
        out_specs=pl.BlockSpec((1, tr, d), lambda b, i: (b, i, 0)),
        out_shape=jax.ShapeDtypeStruct((bsz, n_lat, d), F32),
        compiler_params=_cparams("parallel", "parallel"),
        name="out_proj_last",
    )(yssm, proj, proj, d_row, w_glu, h_f, h_b, proj, proj, mnorm,
      yb_lat, w_out, h_prev, mod, final_w)


def _reorder_w_in(w):
    g0 = C_MO + MW
    return jnp.concatenate([w[:, :g0], w[:, g0 + 4 * MH:]], axis=1), w[:, g0:g0 + 4 * MH]


def kernel(x, c, ctx, c_ctx, norm_w, ada_w, ada_b, w_in, mlstm_gate_b, ssm_a_re, ssm_a_im, ssm_log_dt,
           ssm_b_re, ssm_b_im, ssm_c_re, ssm_c_im, ssm_d, ssm_w_glu, attn_q_norm, attn_k_norm,
           mlstm_conv_w, mlstm_conv_b, mlstm_norm_w, w_out, final_norm_w):
    bsz, n_lat, d = x.shape
    nctx = ctx.shape[1]
    s = nctx + n_lat
    depth = norm_w.shape[0]
    assert bsz < 8 and nctx % ROW_TILE == 0 and n_lat % ROW_TILE == 0 and depth == 2

    c_rows = jnp.zeros((8, d), F32).at[:bsz].set(c).at[bsz].set(c_ctx)
    mod = _ada_call(c_rows, ada_w, ada_b).reshape(depth * 8, 1, 3 * d)
    cos_all, sin_all, cos_lat, sin_lat = _rope_tables(n_lat, nctx)

    w_main, w_gr, b_gr = [], [], []
    for layer in range(depth):
        wm, wg = _reorder_w_in(w_in[layer])
        w_main.append(wm.astype(BF16))
        w_gr.append(wg.reshape(d, 2, 2 * MH).transpose(1, 2, 0).astype(BF16))
        gb = mlstm_gate_b[layer].astype(F32).reshape(2, 2 * MH)
        b_gr.append(jnp.broadcast_to(gb[:, :, None], (2, 2 * MH, 128)))

    norm_w3 = norm_w.astype(F32).reshape(depth, 1, d)
    xn, g_row = _prenorm_call(x, ctx, norm_w3, mod, w_gr[0], b_gr[0], 0)
    h_prev = None
    out = None
    for layer in range(depth):
        last = layer == depth - 1
        proj = _inproj_call(xn.reshape(bsz * s, d), w_main[layer]).reshape(bsz, s, PROJ_W)

        yssm = _s5_branch(proj, dict(a_re=ssm_a_re[layer], a_im=ssm_a_im[layer], log_dt=ssm_log_dt[layer],
                                     b_re=ssm_b_re[layer], b_im=ssm_b_im[layer], c_re=ssm_c_re[layer],
                                     c_im=ssm_c_im[layer]), nctx)

        qw = attn_q_norm[layer].reshape(1, HD).astype(F32)
        ks, vs = _attn_kv_prep_call(proj, cos_all, sin_all, attn_k_norm[layer].reshape(1, HD).astype(F32))
        yb_lat = _attn_lat_call(proj, ks, vs, cos_lat, sin_lat, qw, nctx)

        conv_w = jnp.zeros((8, 2 * MW), F32).at[:3].set(mlstm_conv_w[layer].astype(F32))
        qc, kc = _mlstm_prep_call(proj, conv_w, mlstm_conv_b[layer].astype(F32).reshape(1, 2 * MW), nctx)
        h_f, h_b = _mlstm_call(qc, kc, proj, g_row, nctx)

        d_row = ssm_d[layer].astype(F32).reshape(1, SSM_W)
        w_glu = ssm_w_glu[layer].astype(BF16)
        mnorm = mlstm_norm_w[layer].astype(F32).reshape(1, MW)
        w_o = w_out[layer].astype(BF16)
        if not last:
            yb_ctx = _attn_ctx_call(proj, ks, vs, qw, nctx)
            h_prev, xn, g_row = _outproj_mid_call(yssm, h_f, h_b, yb_ctx, yb_lat, proj, d_row, w_glu, mnorm,
                                                  w_o, x, ctx, mod, norm_w3, w_gr[layer + 1], b_gr[layer + 1],
                                                  layer)
        else:
            out = _outproj_last_call(yssm, h_f, h_b, yb_lat, proj, d_row, w_glu, mnorm, w_o, h_prev, mod,
                                     final_norm_w.reshape(1, d), layer, nctx)
    return out
```

```python
import functools
import math

import jax
import jax.numpy as jnp
from jax import lax
from jax.experimental import pallas as pl
from jax.experimental.pallas import tpu as pltpu

F32 = jnp.float32
BF16 = jnp.bfloat16
EPS = 1e-6

SSM_W = 512
SSM_P = 16
SSM_G = 32
SSM_N = 64
S5_T = 16
S5_SLAB_G = 8
S5_SLABS = SSM_G // S5_SLAB_G
S5_ST = S5_SLAB_G * SSM_N
AH = 8
AKV = 2
AGRP = AH // AKV
HD = 128
AW = AH * HD
KVW = AKV * HD
GRID_W = 64
ROPE_THETA = 10000.0
MH = 4
MD = 128
MW = MH * MD
MCH = 128
C_U, C_Z, C_AQ, C_AK, C_AV, C_AZ, C_MQ, C_MK, C_MV, C_MO, C_MZ = (
    0, 512, 1024, 2048, 2304, 2560, 3584, 4096, 4608, 5120, 5632)
PROJ_W = 6144
MIX_W = SSM_W + AW + MW

ROW_TILE = 256
VMEM_LIMIT = 48 * 1024 * 1024


def _cparams(*sem):
    return pltpu.CompilerParams(dimension_semantics=sem, vmem_limit_bytes=VMEM_LIMIT)


def _dot(a, b):
    return jnp.dot(a, b, preferred_element_type=F32)


def _dot_nt(a, b):
    return lax.dot_general(a, b, (((1,), (1,)), ((), ())), preferred_element_type=F32)


def _dot_tn(a, b):
    return lax.dot_general(a, b, (((0,), (0,)), ((), ())), preferred_element_type=F32)


def _split3(x):
    hi = x.astype(BF16)
    r1 = x - hi.astype(F32)
    mid = r1.astype(BF16)
    lo = (r1 - mid.astype(F32)).astype(BF16)
    return hi, mid, lo


def _silu(x):
    return x * jax.nn.sigmoid(x)


def _log_sigmoid(x):
    return jnp.minimum(x, 0.0) - jnp.log1p(jnp.exp(-jnp.abs(x)))


def _rms(x, w):
    return x * lax.rsqrt(jnp.mean(x * x, axis=-1, keepdims=True) + EPS) * w


def _pick_tile(n, cap, mult):
    best = mult
    for t in range(mult, min(n, cap) + 1, mult):
        if n % t == 0:
            best = t
    return best


def _ada_kernel(c_ref, w_ref, b_ref, o_ref):
    a = _silu(c_ref[...])
    w = w_ref[0]
    a_hi = a.astype(BF16)
    a_lo = (a - a_hi.astype(F32)).astype(BF16)
    w_hi = w.astype(BF16)
    w_lo = (w - w_hi.astype(F32)).astype(BF16)
    o_ref[0] = _dot(a_hi, w_hi) + _dot(a_hi, w_lo) + _dot(a_lo, w_hi) + b_ref[0]


def _ada_call(c_rows, ada_w, ada_b):
    depth, d, n3 = ada_w.shape
    tn = 512
    return pl.pallas_call(
        _ada_kernel,
        grid=(depth, n3 // tn),
        in_specs=[
            pl.BlockSpec((8, d), lambda l, j: (0, 0)),
            pl.BlockSpec((1, d, tn), lambda l, j: (l, 0, j)),
            pl.BlockSpec((1, 1, tn), lambda l, j: (l, 0, j)),
        ],
        out_specs=pl.BlockSpec((1, 8, tn), lambda l, j: (l, 0, j)),
        out_shape=jax.ShapeDtypeStruct((depth, 8, n3), F32),
        compiler_params=_cparams("parallel", "parallel"),
        name="ada_mod",
    )(c_rows, ada_w, ada_b.reshape(depth, 1, n3))


def _prenorm_kernel(x_ref, ctx_ref, nw_ref, sh_ref, sc_ref, wgr_ref, bgr_ref, o_ref, gr_ref, *, nct):
    h = jnp.where(pl.program_id(1) < nct, ctx_ref[0], x_ref[0])
    xn = (_rms(h, nw_ref[0]) * (1.0 + sc_ref[0]) + sh_ref[0]).astype(BF16)
    o_ref[0] = xn
    for dirn in range(2):
        gr_ref[0, dirn] = _dot_nt(wgr_ref[dirn], xn) + bgr_ref[dirn][:, :1]


def _mod_spec(d, layer, part, nct, nb):
    def idx(b, i):
        return (layer * 8 + jnp.where(i < nct, nb, b), 0, part)
    return pl.BlockSpec((1, 1, d), idx)


def _prenorm_call(x, ctx, nw, mod, w_gr, b_gr, layer):
    bsz, n, d = x.shape
    nctx = ctx.shape[1]
    tr = ROW_TILE
    nct = nctx // tr
    s = nctx + n
    return pl.pallas_call(
        functools.partial(_prenorm_kernel, nct=nct),
        grid=(bsz, s // tr),
        in_specs=[
            pl.BlockSpec((1, tr, d), lambda b, i: (b, jnp.maximum(i - nct, 0), 0)),
            pl.BlockSpec((1, tr, d), lambda b, i: (b, jnp.minimum(i, nct - 1), 0)),
            pl.BlockSpec((1, 1, d), lambda b, i: (layer, 0, 0)),
            _mod_spec(d, layer, 0, nct, bsz),
            _mod_spec(d, layer, 1, nct, bsz),
            pl.BlockSpec((2, 8, d), lambda b, i: (0, 0, 0)),
            pl.BlockSpec((2, 8, 128), lambda b, i: (0, 0, 0)),
        ],
        out_specs=[
            pl.BlockSpec((1, tr, d), lambda b, i: (b, i, 0)),
            pl.BlockSpec((1, 2, 8, tr), lambda b, i: (b, 0, 0, i)),
        ],
        out_shape=[
            jax.ShapeDtypeStruct((bsz, s, d), BF16),
            jax.ShapeDtypeStruct((bsz, 2, 8, s), F32),
        ],
        compiler_params=_cparams("parallel", "parallel"),
        name="prenorm",
    )(x, ctx, nw, mod, mod, w_gr, b_gr)


def _matmul_kernel(x_ref, w_ref, o_ref):
    o_ref[...] = _dot(x_ref[...], w_ref[...])


def _inproj_call(xn2d, w):
    m, d = xn2d.shape
    n = w.shape[1]
    tm = _pick_tile(m, 1056, 16)
    tn = 1024
    return pl.pallas_call(
        _matmul_kernel,
        grid=(n // tn, m // tm),
        in_specs=[
            pl.BlockSpec((tm, d), lambda j, i: (i, 0)),
            pl.BlockSpec((d, tn), lambda j, i: (0, j)),
        ],
        out_specs=pl.BlockSpec((tm, tn), lambda j, i: (i, j)),
        out_shape=jax.ShapeDtypeStruct((m, n), F32),
        compiler_params=_cparams("parallel", "parallel"),
        name="in_proj",
    )(xn2d, w)


def _s5_weights(a_re, a_im, log_dt, b_re, b_im, c_re, c_im):
    t_ = S5_T
    a_re = a_re.astype(F32)
    a_im = a_im.astype(F32)
    dt = jnp.exp(log_dt.astype(F32))[..., None]
    mag = jnp.exp(a_re * dt)
    lam_re = mag * jnp.cos(a_im * dt)
    lam_im = mag * jnp.sin(a_im * dt)
    inv_abs2 = 1.0 / (a_re * a_re + a_im * a_im)
    num_re, num_im = lam_re - 1.0, lam_im
    f_re = (num_re * a_re + num_im * a_im) * inv_abs2
    f_im = (num_im * a_re - num_re * a_im) * inv_abs2
    b_re = b_re.astype(F32)[None]
    b_im = b_im.astype(F32)[None]
    bb_re = f_re[:, :, None, :] * b_re - f_im[:, :, None, :] * b_im
    bb_im = f_re[:, :, None, :] * b_im + f_im[:, :, None, :] * b_re
    c_re = c_re.astype(F32)
    c_im = c_im.astype(F32)

    pr = [jnp.ones_like(lam_re)]
    pi = [jnp.zeros_like(lam_im)]
    for _ in range(t_):
        pr.append(pr[-1] * lam_re - pi[-1] * lam_im)
        pi.append(pr[-2] * lam_im + pi[-1] * lam_re)

    def powers(exps_f, exps_b):
        re = jnp.stack([jnp.stack([pr[e][0] for e in exps_f]), jnp.stack([pr[e][1] for e in exps_b])])
        im = jnp.stack([jnp.stack([pi[e][0] for e in exps_f]), jnp.stack([pi[e][1] for e in exps_b])])
        return re[:, :, :, None, :], im[:, :, :, None, :]

    def slab_rows(w):
        return w.reshape(2, t_, S5_SLABS, 128, 2 * SSM_N).astype(BF16)

    steps = list(range(t_))
    p_re, p_im = powers([t_ - 1 - t for t in steps], steps)
    win = slab_rows(jnp.concatenate([bb_re[:, None] * p_re - bb_im[:, None] * p_im,
                                     bb_re[:, None] * p_im + bb_im[:, None] * p_re], axis=-1))
    p_re, p_im = powers([t + 1 for t in steps], [t_ - t for t in steps])
    wout = slab_rows(jnp.concatenate([c_re[:, None] * p_re - c_im[:, None] * p_im,
                                      -(c_re[:, None] * p_im + c_im[:, None] * p_re)], axis=-1))

    p_re, p_im = powers(steps, steps)
    kr = c_re[:, None] * p_re - c_im[:, None] * p_im
    ki = c_re[:, None] * p_im + c_im[:, None] * p_re
    kk = jnp.einsum('dgqm,dtgpm->dgqtp', jnp.concatenate([bb_re, bb_im], axis=-1),
                    jnp.concatenate([kr, -ki], axis=-1), precision=lax.Precision.HIGH)
    k_f = kk[0].reshape(S5_SLABS, 128, t_ * SSM_P)
    k_b = kk[1][:, :, ::-1, :].reshape(S5_SLABS, 128, t_ * SSM_P)

    def lam_row(v):
        return v.reshape(2, S5_SLABS, 1, S5_ST)

    lam_t = jnp.concatenate([lam_row(pr[t_]), lam_row(pi[t_])], axis=-1)
    return win, wout, k_f, k_b, lam_t


def _iota(shape, dim):
    return lax.broadcasted_iota(jnp.int32, shape, dim)


def _s5_core_kernel(u_ref, win_ref, wout_ref, kf_ref, kb_ref, lam_ref, y_ref, lhs_sc, w_sc, s_sc, acc_sc,
                    *, nch, nctx_ch):
    t_ = S5_T
    n2 = 2 * SSM_N
    st2 = 2 * S5_ST
    lg_p = SSM_P.bit_length() - 1
    lg_n = SSM_N.bit_length() - 1
    lg_st = S5_ST.bit_length() - 1

    for t in range(t_):
        lhs_sc[:, t * 128:(t + 1) * 128] = u_ref[0, pl.ds(t, nch, stride=t_), :].astype(BF16)

    k1, c1 = _iota((n2, st2), 0), _iota((n2, st2), 1)
    e_in = jnp.where(((k1 >> lg_n) == (c1 >> lg_st)) & ((k1 & (SSM_N - 1)) == (c1 & (SSM_N - 1))),
                     1.0, 0.0).astype(BF16)
    r1, c1b = _iota((128, st2), 0), _iota((128, st2), 1)
    m_in = (r1 >> lg_p) == ((c1b & (S5_ST - 1)) >> lg_n)

    def expand_state_table(tab_ref, d):
        for t in range(t_):
            rows = slice(t * 128, (t + 1) * 128)
            w_sc[rows, :st2] = jnp.where(m_in, _dot(tab_ref[d, t, 0], e_in), 0.0).astype(BF16)

    for d in range(2):
        expand_state_table(win_ref, d)
        s_sc[d] = _dot(lhs_sc[...], w_sc[:, :st2])

    lam_f = lam_ref[0, 0]
    lam_b = lam_ref[1, 0]
    lrf, lif = lam_f[:, :S5_ST], lam_f[:, S5_ST:]
    lrb, lib = lam_b[:, :S5_ST], lam_b[:, S5_ST:]

    def step(k, carry):
        hrf, hif, hrb, hib = carry
        cb = jnp.where(k < nctx_ch, nctx_ch - 1 - k, nch - 1 - (k - nctx_ch))
        sf = s_sc[0, pl.ds(k, 1), :]
        sb = s_sc[1, pl.ds(cb, 1), :]
        s_sc[0, pl.ds(k, 1), :] = jnp.concatenate([hrf, hif], axis=-1)
        s_sc[1, pl.ds(cb, 1), :] = jnp.concatenate([hrb, hib], axis=-1)
        return (lrf * hrf - lif * hif + sf[:, :S5_ST], lrf * hif + lif * hrf + sf[:, S5_ST:],
                lrb * hrb - lib * hib + sb[:, :S5_ST], lrb * hib + lib * hrb + sb[:, S5_ST:])

    zero = jnp.zeros((1, S5_ST), F32)
    lax.fori_loop(0, nch, step, (zero, zero, zero, zero))

    k2, c2 = _iota((t_ * SSM_P, t_ * 128), 0), _iota((t_ * SSM_P, t_ * 128), 1)
    e_m = jnp.where(((k2 >> lg_p) == (c2 >> 7)) & ((k2 & (SSM_P - 1)) == (c2 & (SSM_P - 1))),
                    1.0, 0.0).astype(BF16)
    r2, c2b = _iota((128, t_ * 128), 0), _iota((128, t_ * 128), 1)
    m_m = (r2 >> lg_p) == ((c2b & 127) >> lg_p)
    k_f = kf_ref[0]
    k_b = kb_ref[0]
    lag_col = _iota((128, t_ * SSM_P), 1)
    for t in range(t_):
        rows = slice(t * 128, (t + 1) * 128)
        fwd = k_f if t == 0 else pltpu.roll(k_f, t * SSM_P, 1)
        bwd = k_b if t == t_ - 1 else pltpu.roll(k_b, (t + 1) * SSM_P, 1)
        mc = (jnp.where(lag_col >= t * SSM_P, fwd, 0.0)
              + jnp.where(lag_col < (t + 1) * SSM_P, bwd, 0.0)).astype(BF16)
        w_sc[rows, :] = jnp.where(m_m, _dot(mc, e_m), 0.0).astype(BF16)
    acc_sc[...] = _dot(lhs_sc[...], w_sc[...])

    for d in range(2):
        expand_state_table(wout_ref, d)
        acc_sc[...] += _dot_nt(s_sc[d].astype(BF16), w_sc[:, :st2])

    for t in range(t_):
        y_ref[0, pl.ds(t, nch, stride=t_), :] = acc_sc[:, t * 128:(t + 1) * 128]


def _s5_core_call(proj, win, wout, k_f, k_b, lam_t, nctx):
    bsz, s, _ = proj.shape
    t_ = S5_T
    nch = s // t_
    kdim = t_ * 128
    n2 = 2 * SSM_N
    st2 = 2 * S5_ST
    cu = C_U // 128
    return pl.pallas_call(
        functools.partial(_s5_core_kernel, nch=nch, nctx_ch=nctx // t_),
        grid=(S5_SLABS, bsz),
        in_specs=[
            pl.BlockSpec((1, s, 128), lambda sl, b: (b, 0, cu + sl)),
            pl.BlockSpec((2, t_, 1, 128, n2), lambda sl, b: (0, 0, sl, 0, 0)),
            pl.BlockSpec((2, t_, 1, 128, n2), lambda sl, b: (0, 0, sl, 0, 0)),
            pl.BlockSpec((1, 128, t_ * SSM_P), lambda sl, b: (sl, 0, 0)),
            pl.BlockSpec((1, 128, t_ * SSM_P), lambda sl, b: (sl, 0, 0)),
            pl.BlockSpec((2, 1, 1, st2), lambda sl, b: (0, sl, 0, 0)),
        ],
        out_specs=pl.BlockSpec((1, s, 128), lambda sl, b: (b, 0, sl)),
        out_shape=jax.ShapeDtypeStruct((bsz, s, SSM_W), F32),
        scratch_shapes=[
            pltpu.VMEM((nch, kdim), BF16),
            pltpu.VMEM((kdim, kdim), BF16),
            pltpu.VMEM((2, nch, st2), F32),
            pltpu.VMEM((nch, kdim), F32),
        ],
        compiler_params=pltpu.CompilerParams(dimension_semantics=("parallel", "parallel"),
                                             vmem_limit_bytes=56 * 1024 * 1024),
        name="s5_core",
    )(proj, win, wout, k_f, k_b, lam_t)


def _s5_branch(proj, p, nctx):
    win, wout, k_f, k_b, lam_t = _s5_weights(p['a_re'], p['a_im'], p['log_dt'], p['b_re'], p['b_im'],
                                             p['c_re'], p['c_im'])
    return _s5_core_call(proj, win, wout, k_f, k_b, lam_t, nctx)


def _rope_tables(n_lat, nctx):
    rows = n_lat // GRID_W
    row = jnp.broadcast_to(jnp.arange(rows, dtype=F32)[:, None], (rows, GRID_W)).reshape(-1)
    col = jnp.broadcast_to(jnp.arange(GRID_W, dtype=F32)[None, :], (rows, GRID_W)).reshape(-1)
    n_freq = HD // 4
    inv = ROPE_THETA ** (-jnp.arange(n_freq, dtype=F32) / n_freq)
    ang = jnp.concatenate([row[:, None] * inv, col[:, None] * inv], axis=-1)
    cs, sn = jnp.cos(ang), jnp.sin(ang)
    cos2 = jnp.concatenate([cs, cs], axis=-1)
    sin2 = jnp.concatenate([-sn, sn], axis=-1)
    cos_all = jnp.concatenate([jnp.ones((nctx, HD), F32), cos2], axis=0)
    sin_all = jnp.concatenate([jnp.zeros((nctx, HD), F32), sin2], axis=0)
    return cos_all, sin_all, cos2, sin2


def _norm_rope(x, w, cs, sn):
    y = _rms(x, w)
    return y * cs + pltpu.roll(y, HD // 2, 1) * sn


def _attn_kv_prep_kernel(k_ref, v_ref, cos_ref, sin_ref, kw_ref, ko_ref, vo_ref):
    cs = cos_ref[...]
    sn = sin_ref[...]
    k = k_ref[0]
    for h in range(AKV):
        ko_ref[0, :, h * HD:(h + 1) * HD] = _norm_rope(k[:, h * HD:(h + 1) * HD], kw_ref[...], cs, sn).astype(BF16)
    vo_ref[0] = v_ref[0].astype(BF16)


def _attn_kv_prep_call(proj, cos2, sin2, kw):
    bsz, s, _ = proj.shape
    tr = ROW_TILE
    return pl.pallas_call(
        _attn_kv_prep_kernel,
        grid=(bsz, s // tr),
        in_specs=[
            pl.BlockSpec((1, tr, KVW), lambda b, i: (b, i, C_AK // KVW)),
            pl.BlockSpec((1, tr, KVW), lambda b, i: (b, i, C_AV // KVW)),
            pl.BlockSpec((tr, HD), lambda b, i: (i, 0)),
            pl.BlockSpec((tr, HD), lambda b, i: (i, 0)),
            pl.BlockSpec((1, HD), lambda b, i: (0, 0)),
        ],
        out_specs=[
            pl.BlockSpec((1, tr, KVW), lambda b, i: (b, i, 0)),
            pl.BlockSpec((1, tr, KVW), lambda b, i: (b, i, 0)),
        ],
        out_shape=[
            jax.ShapeDtypeStruct((bsz, s, KVW), BF16),
            jax.ShapeDtypeStruct((bsz, s, KVW), BF16),
        ],
        compiler_params=_cparams("parallel", "parallel"),
        name="attn_kv_prep",
    )(proj, proj, cos2, sin2, kw)


Q_SCALE = HD ** -0.5 * math.log2(math.e)


def _attn_finish(acc, l, z_refs, o_ref, tq):
    o = acc / l
    ntile = len(z_refs)
    for h in range(AGRP):
        for t, z_ref in enumerate(z_refs):
            r0 = (h * ntile + t) * tq
            z = z_ref[0][:, h * HD:(h + 1) * HD]
            o_ref[0, t * tq:(t + 1) * tq, h * HD:(h + 1) * HD] = (o[r0:r0 + tq] * _silu(z)).astype(BF16)


def _attn_ctx_kernel(q_ref, z_ref, qw_ref, k_ref, v_ref, o_ref, *, tq):
    q = q_ref[0]
    q4 = jnp.concatenate([(_rms(q[:, h * HD:(h + 1) * HD], qw_ref[...]) * Q_SCALE).astype(BF16)
                          for h in range(AGRP)], axis=0)
    s = _dot_nt(q4, k_ref[0])
    m = jnp.max(s, axis=-1, keepdims=True)
    p = jnp.exp2(s - m)
    l = jnp.sum(p, axis=-1, keepdims=True)
    _attn_finish(_dot(p.astype(BF16), v_ref[0]), l, [z_ref], o_ref, tq)


def _attn_ctx_call(proj, ks, vs, qw, nctx):
    bsz = proj.shape[0]
    tq = ROW_TILE
    zw = AGRP * HD
    return pl.pallas_call(
        functools.partial(_attn_ctx_kernel, tq=tq),
        grid=(bsz, AKV, nctx // tq),
        in_specs=[
            pl.BlockSpec((1, tq, zw), lambda b, g, i: (b, i, C_AQ // zw + g)),
            pl.BlockSpec((1, tq, zw), lambda b, g, i: (b, i, C_AZ // zw + g)),
            pl.BlockSpec((1, HD), lambda b, g, i: (0, 0)),
            pl.BlockSpec((1, nctx, HD), lambda b, g, i: (b, 0, g)),
            pl.BlockSpec((1, nctx, HD), lambda b, g, i: (b, 0, g)),
        ],
        out_specs=pl.BlockSpec((1, tq, zw), lambda b, g, i: (b, i, g)),
        out_shape=jax.ShapeDtypeStruct((bsz, nctx, AW), BF16),
        compiler_params=_cparams("parallel", "parallel", "parallel"),
        name="attention_ctx",
    )(proj, proj, qw, ks, vs)


def _attn_chunks(s):
    n128 = s // 128
    cnt = max(2, 2 * round(n128 / 11))
    base, extra = divmod(n128, cnt)
    sizes = [(base + (1 if j < extra else 0)) * 128 for j in range(cnt)]
    starts = [sum(sizes[:j]) for j in range(cnt)]
    return list(zip(starts, sizes))


def _attn_lat_kernel(q_ref, qn_ref, z_ref, cos_ref, sin_ref, cosn_ref, sinn_ref, qw_ref, k_ref, v_ref,
                     o_ref, q_sc, s_sc, *, tq, chunks):
    i = pl.program_id(2)
    nck = len(chunks)

    def prep(qr, cr, sr):
        q = qr[0]
        return jnp.concatenate(
            [(_norm_rope(q[:, h * HD:(h + 1) * HD], qw_ref[...], cr[...], sr[...]) * Q_SCALE).astype(BF16)
             for h in range(AGRP)], axis=0)

    def scores(j):
        st, sz = chunks[j % nck]
        s_sc[j % 2, :, :sz] = _dot_nt(q_sc[...], k_ref[0, st:st + sz, :])

    @pl.when(i == 0)
    def _():
        q_sc[...] = prep(q_ref, cos_ref, sin_ref)
        scores(0)

    m = l = acc = None
    for j in range(nck):
        st, sz = chunks[j]
        if j + 1 == nck:
            q_sc[...] = prep(qn_ref, cosn_ref, sinn_ref)
        scores(j + 1)
        s = s_sc[j % 2, :, :sz]
        s_max = jnp.max(s, axis=-1, keepdims=True)
        m_new = s_max if j == 0 else jnp.maximum(m, s_max)
        p = jnp.exp2(s - m_new)
        pv = _dot(p.astype(BF16), v_ref[0, st:st + sz, :])
        p_sum = jnp.sum(p, axis=-1, keepdims=True)
        if j == 0:
            l, acc = p_sum, pv
        else:
            alpha = jnp.exp2(m - m_new)
            l = alpha * l + p_sum
            acc = alpha * acc + pv
        m = m_new
    _attn_finish(acc, l, [z_ref], o_ref, tq)


def _attn_lat_call(proj, ks, vs, cos_lat, sin_lat, qw, nctx):
    bsz, s, _ = proj.shape
    n_lat = s - nctx
    tq = ROW_TILE
    nct = nctx // tq
    chunks = _attn_chunks(s)
    assert len(chunks) % 2 == 0 and s % 128 == 0
    ck = max(sz for _, sz in chunks)
    zw = AGRP * HD
    cq = C_AQ // zw
    cz = C_AZ // zw
    nlt = n_lat // tq

    def nxt(i):
        return jnp.minimum(i + 1, nlt - 1)

    return pl.pallas_call(
        functools.partial(_attn_lat_kernel, tq=tq, chunks=chunks),
        grid=(bsz, AKV, nlt),
        in_specs=[
            pl.BlockSpec((1, tq, zw), lambda b, g, i: (b, nct + i, cq + g)),
            pl.BlockSpec((1, tq, zw), lambda b, g, i: (b, nct + nxt(i), cq + g)),
            pl.BlockSpec((1, tq, zw), lambda b, g, i: (b, nct + i, cz + g)),
            pl.BlockSpec((tq, HD), lambda b, g, i: (i, 0)),
            pl.BlockSpec((tq, HD), lambda b, g, i: (i, 0)),
            pl.BlockSpec((tq, HD), lambda b, g, i: (nxt(i), 0)),
            pl.BlockSpec((tq, HD), lambda b, g, i: (nxt(i), 0)),
            pl.BlockSpec((1, HD), lambda b, g, i: (0, 0)),
            pl.BlockSpec((1, s, HD), lambda b, g, i: (b, 0, g)),
            pl.BlockSpec((1, s, HD), lambda b, g, i: (b, 0, g)),
        ],
        out_specs=pl.BlockSpec((1, tq, zw), lambda b, g, i: (b, i, g)),
        out_shape=jax.ShapeDtypeStruct((bsz, n_lat, AW), BF16),
        scratch_shapes=[pltpu.VMEM((AGRP * tq, HD), BF16), pltpu.VMEM((2, AGRP * tq, ck), F32)],
        compiler_params=_cparams("parallel", "parallel", "arbitrary"),
        name="attention",
    )(proj, proj, proj, cos_lat, sin_lat, cos_lat, sin_lat, qw, ks, vs)


def _mlstm_prep_kernel(q_ref, k_ref, qp_ref, kp_ref, qn_ref, kn_ref, w_ref, b_ref, qo_ref, ko_ref,
                       *, tr, nct, nt):
    i = pl.program_id(1)
    first = jnp.logical_or(i == 0, i == nct)
    last = jnp.logical_or(i == nct - 1, i == nt - 1)
    rid = lax.broadcasted_iota(jnp.int32, (tr, MW), 0)
    w = w_ref[...]
    bias = b_ref[...]

    def conv(x, prev_row, next_row, off):
        prev_row = jnp.where(first, 0.0, prev_row)
        next_row = jnp.where(last, 0.0, next_row)
        xp = jnp.where(rid == 0, prev_row, pltpu.roll(x, 1, 0))
        xn = jnp.where(rid == tr - 1, next_row, pltpu.roll(x, tr - 1, 0))
        y = (w[0:1, off:off + MW] * xp + w[1:2, off:off + MW] * x + w[2:3, off:off + MW] * xn
             + bias[:, off:off + MW])
        return _silu(y)

    qo_ref[0] = conv(q_ref[0], qp_ref[0, 7:8, :], qn_ref[0, 0:1, :], 0).astype(BF16)
    ko_ref[0] = (conv(k_ref[0], kp_ref[0, 7:8, :], kn_ref[0, 0:1, :], MW) * (MD ** -0.5)).astype(BF16)


def _mlstm_prep_call(proj, conv_w, conv_b, nctx):
    bsz, s, _ = proj.shape
    tr = ROW_TILE
    nt = s // tr
    nct = nctx // tr
    r8 = tr // 8
    n8 = s // 8
    cq = C_MQ // MW
    ck = C_MK // MW

    def prev_map(c):
        return lambda b, i: (b, jnp.maximum(i * r8 - 1, 0), c)

    def next_map(c):
        return lambda b, i: (b, jnp.minimum((i + 1) * r8, n8 - 1), c)

    return pl.pallas_call(
        functools.partial(_mlstm_prep_kernel, tr=tr, nct=nct, nt=nt),
        grid=(bsz, nt),
        in_specs=[
            pl.BlockSpec((1, tr, MW), lambda b, i: (b, i, cq)),
            pl.BlockSpec((1, tr, MW), lambda b, i: (b, i, ck)),
            pl.BlockSpec((1, 8, MW), prev_map(cq)),
            pl.BlockSpec((1, 8, MW), prev_map(ck)),
            pl.BlockSpec((1, 8, MW), next_map(cq)),
            pl.BlockSpec((1, 8, MW), next_map(ck)),
            pl.BlockSpec((8, 2 * MW), lambda b, i: (0, 0)),
            pl.BlockSpec((1, 2 * MW), lambda b, i: (0, 0)),
        ],
        out_specs=[
            pl.BlockSpec((1, tr, MW), lambda b, i: (b, i, 0)),
            pl.BlockSpec((1, tr, MW), lambda b, i: (b, i, 0)),
        ],
        out_shape=[
            jax.ShapeDtypeStruct((bsz, s, MW), BF16),
            jax.ShapeDtypeStruct((bsz, s, MW), BF16),
        ],
        compiler_params=_cparams("parallel", "parallel"),
        name="mlstm_prep",
    )(proj, proj, proj, proj, proj, proj, conv_w, conv_b)


def _mlstm_bwd_chunk(step, nctx_ch, nch):
    return jnp.where(step < nctx_ch, nctx_ch - 1 - step, nch - 1 - (step - nctx_ch))


def _hi_mid(x):
    hi = x.astype(BF16)
    mid = (x - hi.astype(F32)).astype(BF16)
    return jnp.concatenate([hi, mid], axis=1)


def _mlstm_chunk(dirn, q, k, v, gr, o_ref, b, cn_sc, m_sc, base):
    row = _iota((MCH, MCH), 0)
    col = _iota((MCH, MCH), 1)
    allowed = (col <= row) if dirn == 0 else (col >= row)
    eye = col == row
    tri = jnp.where(allowed, 1.0, 0.0).astype(BF16)
    lf_r = _log_sigmoid(gr)
    r3 = _split3(lf_r)
    b_r = _dot_nt(r3[0], tri) + _dot_nt(r3[1], tri) + _dot_nt(r3[2], tri)

    a8 = jnp.concatenate([gr[:MH] - b_r[MH:], gr[:MH] - b_r[MH:]], axis=0)
    lane = _iota((8, MCH), 1)
    cm8 = a8
    sh = 1
    while sh < MCH:
        if dirn == 0:
            cm8 = jnp.maximum(cm8, jnp.where(lane >= sh, pltpu.roll(cm8, sh, 1), -jnp.inf))
        else:
            cm8 = jnp.maximum(cm8, jnp.where(lane < MCH - sh, pltpu.roll(cm8, MCH - sh, 1), -jnp.inf))
        sh *= 2

    ones2 = jnp.ones((2 * MCH, MCH), BF16)
    lhs = []
    for h in range(MH):
        lhs += [_hi_mid(jnp.where(allowed, lf_r[MH + h:MH + h + 1, :], 0.0)),
                _hi_mid(jnp.where(eye, cm8[h:h + 1, :], 0.0)),
                _hi_mid(jnp.where(eye, gr[h:h + 1, :], 0.0))]
    rep = _dot(jnp.concatenate(lhs, axis=0), ones2)

    ones_v = jnp.ones((MCH, MD), BF16)
    for h in range(MH):
        st = base + h
        bc = rep[(3 * h) * MCH:(3 * h + 1) * MCH]
        cm = rep[(3 * h + 1) * MCH:(3 * h + 2) * MCH]
        lic = rep[(3 * h + 2) * MCH:(3 * h + 3) * MCH]
        br = b_r[MH + h:MH + h + 1, :]
        li_r = gr[h:h + 1, :]
        m_row = m_sc[st][:1, :]
        qh = q[:, h * MD:(h + 1) * MD]
        kh = k[:, h * MD:(h + 1) * MD]
        v_ext = jnp.concatenate([v[:, h * MD:(h + 1) * MD], ones_v], axis=1)

        log_d = jnp.where(allowed, bc - br + li_r, -jnp.inf)
        m_t = bc + jnp.maximum(m_row, cm)
        s = _dot_nt(qh, kh) * jnp.exp(log_d - m_t)
        intra = _dot(s.astype(BF16), v_ext)
        inter = _dot(qh, cn_sc[st].astype(BF16))
        w_inter = jnp.exp(bc + m_row - m_t)
        num = intra[:, :MD] + w_inter * inter[:, :MD]
        den = intra[:, MD:] + w_inter * inter[:, MD:]
        o_ref[b, :, h * MD:(h + 1) * MD] = num / jnp.maximum(jnp.abs(den), jnp.exp(-m_t))

        b_last = bc[MCH - 1:MCH, :] if dirn == 0 else bc[0:1, :]
        log_w = b_last - bc + lic
        m_new = jnp.maximum(b_last + m_row, jnp.max(log_w, axis=0, keepdims=True))
        decay = jnp.exp(b_last + m_row - m_new)
        kw = kh.astype(F32) * jnp.exp(log_w - m_new)
        cn_sc[st] = jnp.concatenate([decay, decay], axis=1) * cn_sc[st] + _dot_tn(kw.astype(BF16), v_ext)
        m_sc[st] = jnp.broadcast_to(m_new, (8, 128))


def _mlstm_kernel(qf_ref, kf_ref, vf_ref, grf_ref, qb_ref, kb_ref, vb_ref, grb_ref,
                  of_ref, ob_ref, cn_sc, m_sc, *, bsz):
    @pl.when(pl.program_id(0) == 0)
    def _():
        cn_sc[...] = jnp.zeros_like(cn_sc)
        m_sc[...] = jnp.zeros_like(m_sc)

    for b in range(bsz):
        _mlstm_chunk(0, qf_ref[b], kf_ref[b], vf_ref[b].astype(BF16), grf_ref[b, 0],
                     of_ref, b, cn_sc, m_sc, b * MH)
        _mlstm_chunk(1, qb_ref[b], kb_ref[b], vb_ref[b].astype(BF16), grb_ref[b, 0],
                     ob_ref, b, cn_sc, m_sc, (bsz + b) * MH)


def _mlstm_call(qc, kc, proj, g_row, nctx):
    bsz, s, _ = qc.shape
    nch = s // MCH
    nctx_ch = nctx // MCH
    cv = C_MV // MW

    def cb(st):
        return _mlstm_bwd_chunk(st, nctx_ch, nch)

    nst = 2 * bsz * MH
    return pl.pallas_call(
        functools.partial(_mlstm_kernel, bsz=bsz),
        grid=(nch,),
        in_specs=[
            pl.BlockSpec((bsz, MCH, MW), lambda st: (0, st, 0)),
            pl.BlockSpec((bsz, MCH, MW), lambda st: (0, st, 0)),
            pl.BlockSpec((bsz, MCH, MW), lambda st: (0, st, cv)),
            pl.BlockSpec((bsz, 1, 8, MCH), lambda st: (0, 0, 0, st)),
            pl.BlockSpec((bsz, MCH, MW), lambda st: (0, cb(st), 0)),
            pl.BlockSpec((bsz, MCH, MW), lambda st: (0, cb(st), 0)),
            pl.BlockSpec((bsz, MCH, MW), lambda st: (0, cb(st), cv)),
            pl.BlockSpec((bsz, 1, 8, MCH), lambda st: (0, 1, 0, cb(st))),
        ],
        out_specs=[
            pl.BlockSpec((bsz, MCH, MW), lambda st: (0, st, 0)),
            pl.BlockSpec((bsz, MCH, MW), lambda st: (0, cb(st), 0)),
        ],
        out_shape=[
            jax.ShapeDtypeStruct((bsz, s, MW), F32),
            jax.ShapeDtypeStruct((bsz, s, MW), F32),
        ],
        scratch_shapes=[
            pltpu.VMEM((nst, MD, 2 * MD), F32),
            pltpu.VMEM((nst, 8, 128), F32),
        ],
        compiler_params=_cparams("arbitrary"),
        name="mlstm_scan",
    )(qc, kc, proj, g_row, qc, kc, proj, g_row)


def _mixer_specs(tr, row_of):
    def tok(col_block):
        return pl.BlockSpec((1, tr, SSM_W), lambda b, i: (b, row_of(i), col_block))
    const = lambda shape: pl.BlockSpec(shape, lambda b, i: (0,) * len(shape))
    return [tok(0), tok(C_U // SSM_W), tok(C_Z // SSM_W), const((1, SSM_W)), const((SSM_W, SSM_W)),
            tok(0), tok(0), tok(C_MO // MW), tok(C_MZ // MW), const((1, MW))]


def _mixer_outputs(y_ref, u_ref, sz_ref, d_ref, wg_ref, hf_ref, hb_ref, mo_ref, mz_ref, mn_ref):
    g = jax.nn.gelu(y_ref[0] + d_ref[...] * u_ref[0], approximate=True)
    ya = (g * jax.nn.sigmoid(_dot(g.astype(BF16), wg_ref[...])) * _silu(sz_ref[0])).astype(BF16)
    hs = hf_ref[0] + hb_ref[0]
    og, zg, nw = mo_ref[0], mz_ref[0], mn_ref[...]
    yc = []
    for h in range(MH):
        sl = slice(h * MD, (h + 1) * MD)
        yc.append((jax.nn.sigmoid(og[:, sl]) * _rms(hs[:, sl], nw[:, sl]) * _silu(zg[:, sl])).astype(BF16))
    return ya, jnp.concatenate(yc, axis=1)


def _mix_dot(ya, yb, yc, w_ref):
    return (_dot(ya, w_ref[:SSM_W, :]) + _dot(yb, w_ref[SSM_W:SSM_W + AW, :])
            + _dot(yc, w_ref[SSM_W + AW:, :]))


def _outproj_mid_kernel(*refs, nct):
    mixer, (ybc_ref, ybl_ref, w_ref, x_ref, ctx_ref, g_ref, nw_ref, sh_ref, sc_ref,
            wgr_ref, bgr_ref, h_ref, xn_ref, gr_ref) = refs[:10], refs[10:]
    is_ctx = pl.program_id(1) < nct
    ya, yc = _mixer_outputs(*mixer)
    yb = jnp.where(is_ctx, ybc_ref[0], ybl_ref[0])
    mix = _mix_dot(ya, yb, yc, w_ref)
    h = jnp.where(is_ctx, ctx_ref[0], x_ref[0]) + g_ref[0] * mix
    h_ref[0] = h
    xn = (_rms(h, nw_ref[0]) * (1.0 + sc_ref[0]) + sh_ref[0]).astype(BF16)
    xn_ref[0] = xn
    for dirn in range(2):
        gr_ref[0, dirn] = _dot_nt(wgr_ref[dirn], xn) + bgr_ref[dirn][:, :1]


def _outproj_mid_call(yssm, h_f, h_b, yb_ctx, yb_lat, proj, d_row, w_glu, mnorm, w_out, x, ctx, mod, nw,
                      w_gr, b_gr, layer):
    bsz, s, _ = yssm.shape
    d = x.shape[-1]
    nctx = ctx.shape[1]
    tr = ROW_TILE
    nct = nctx // tr
    nxt = layer + 1
    lat = lambda b, i: (b, jnp.maximum(i - nct, 0), 0)
    cxt = lambda b, i: (b, jnp.minimum(i, nct - 1), 0)
    return pl.pallas_call(
        functools.partial(_outproj_mid_kernel, nct=nct),
        grid=(bsz, s // tr),
        in_specs=_mixer_specs(tr, lambda i: i) + [
            pl.BlockSpec((1, tr, AW), cxt),
            pl.BlockSpec((1, tr, AW), lat),
            pl.BlockSpec((MIX_W, d), lambda b, i: (0, 0)),
            pl.BlockSpec((1, tr, d), lat),
            pl.BlockSpec((1, tr, d), cxt),
            _mod_spec(d, layer, 2, nct, bsz),
            pl.BlockSpec((1, 1, d), lambda b, i: (nxt, 0, 0)),
            _mod_spec(d, nxt, 0, nct, bsz),
            _mod_spec(d, nxt, 1, nct, bsz),
            pl.BlockSpec((2, 8, d), lambda b, i: (0, 0, 0)),
            pl.BlockSpec((2, 8, 128), lambda b, i: (0, 0, 0)),
        ],
        out_specs=[
            pl.BlockSpec((1, tr, d), lambda b, i: (b, i, 0)),
            pl.BlockSpec((1, tr, d), lambda b, i: (b, i, 0)),
            pl.BlockSpec((1, 2, 8, tr), lambda b, i: (b, 0, 0, i)),
        ],
        out_shape=[
            jax.ShapeDtypeStruct((bsz, s, d), F32),
            jax.ShapeDtypeStruct((bsz, s, d), BF16),
            jax.ShapeDtypeStruct((bsz, 2, 8, s), F32),
        ],
        compiler_params=_cparams("parallel", "parallel"),
        name="out_proj_mid",
    )(yssm, proj, proj, d_row, w_glu, h_f, h_b, proj, proj, mnorm,
      yb_ctx, yb_lat, w_out, x, ctx, mod, nw, mod, mod, w_gr, b_gr)


def _outproj_last_kernel(*refs):
    mixer, (ybl_ref, w_ref, h_ref, g_ref, nw_ref, o_ref) = refs[:10], refs[10:]
    ya, yc = _mixer_outputs(*mixer)
    mix = _mix_dot(ya, ybl_ref[0], yc, w_ref)
    o_ref[0] = _rms(h_ref[0] + g_ref[0] * mix, nw_ref[...])


def _outproj_last_call(yssm, h_f, h_b, yb_lat, proj, d_row, w_glu, mnorm, w_out, h_prev, mod, final_w,
                       layer, nctx):
    bsz, s, d = h_prev.shape
    tr = ROW_TILE
    nct = nctx // tr
    n_lat = s - nctx
    return pl.pallas_call(
        _outproj_last_kernel,
        grid=(bsz, n_lat // tr),
        in_specs=_mixer_specs(tr, lambda i: i + nct) + [
            pl.BlockSpec((1, tr, AW), lambda b, i: (b, i, 0)),
            pl.BlockSpec((MIX_W, d), lambda b, i: (0, 0)),
            pl.BlockSpec((1, tr, d), lambda b, i: (b, i + nct, 0)),
            pl.BlockSpec((1, 1, d), lambda b, i: (layer * 8 + b, 0, 2)),
            pl.BlockSpec((1, d), lambda b, i: (0, 0)),
        ],
        out_specs=pl.BlockSpec((1, tr, d), lambda b, i: (b, i, 0)),
        out_shape=jax.ShapeDtypeStruct((bsz, n_lat, d), F32),
        compiler_params=_cparams("parallel", "parallel"),
        name="out_proj_last",
    )(yssm, proj, proj, d_row, w_glu, h_f, h_b, proj, proj, mnorm,
      yb_lat, w_out, h_prev, mod, final_w)


def _reorder_w_in(w):
    g0 = C_MO + MW
    return jnp.concatenate([w[:, :g0], w[:, g0 + 4 * MH:]], axis=1), w[:, g0:g0 + 4 * MH]


def kernel(x, c, ctx, c_ctx, norm_w, ada_w, ada_b, w_in, mlstm_gate_b, ssm_a_re, ssm_a_im, ssm_log_dt,
           ssm_b_re, ssm_b_im, ssm_c_re, ssm_c_im, ssm_d, ssm_w_glu, attn_q_norm, attn_k_norm,
           mlstm_conv_w, mlstm_conv_b, mlstm_norm_w, w_out, final_norm_w):
    bsz, n_lat, d = x.shape
    nctx = ctx.shape[1]
    s = nctx + n_lat
    depth = norm_w.shape[0]
    assert bsz < 8 and nctx % ROW_TILE == 0 and n_lat % ROW_TILE == 0 and depth == 2

    c_rows = jnp.zeros((8, d), F32).at[:bsz].set(c).at[bsz].set(c_ctx)
    mod = _ada_call(c_rows, ada_w, ada_b).reshape(depth * 8, 1, 3 * d)
    cos_all, sin_all, cos_lat, sin_lat = _rope_tables(n_lat, nctx)

    w_main, w_gr, b_gr = [], [], []
    for layer in range(depth):
        wm, wg = _reorder_w_in(w_in[layer])
        w_main.append(wm.astype(BF16))
        w_gr.append(wg.reshape(d, 2, 2 * MH).transpose(1, 2, 0).astype(BF16))
        gb = mlstm_gate_b[layer].astype(F32).reshape(2, 2 * MH)
        b_gr.append(jnp.broadcast_to(gb[:, :, None], (2, 2 * MH, 128)))

    norm_w3 = norm_w.astype(F32).reshape(depth, 1, d)
    xn, g_row = _prenorm_call(x, ctx, norm_w3, mod, w_gr[0], b_gr[0], 0)
    h_prev = None
    out = None
    for layer in range(depth):
        last = layer == depth - 1
        proj = _inproj_call(xn.reshape(bsz * s, d), w_main[layer]).reshape(bsz, s, PROJ_W)

        yssm = _s5_branch(proj, dict(a_re=ssm_a_re[layer], a_im=ssm_a_im[layer], log_dt=ssm_log_dt[layer],
                                     b_re=ssm_b_re[layer], b_im=ssm_b_im[layer], c_re=ssm_c_re[layer],
                                     c_im=ssm_c_im[layer]), nctx)

        qw = attn_q_norm[layer].reshape(1, HD).astype(F32)
        ks, vs = _attn_kv_prep_call(proj, cos_all, sin_all, attn_k_norm[layer].reshape(1, HD).astype(F32))
        yb_lat = _attn_lat_call(proj, ks, vs, cos_lat, sin_lat, qw, nctx)

        conv_w = jnp.zeros((8, 2 * MW), F32).at[:3].set(mlstm_conv_w[layer].astype(F32))
        qc, kc = _mlstm_prep_call(proj, conv_w, mlstm_conv_b[layer].astype(F32).reshape(1, 2 * MW), nctx)
        h_f, h_b = _mlstm_call(qc, kc, proj, g_row, nctx)

        d_row = ssm_d[layer].astype(F32).reshape(1, SSM_W)
        w_glu = ssm_w_glu[layer].astype(BF16)
        mnorm = mlstm_norm_w[layer].astype(F32).reshape(1, MW)
        w_o = w_out[layer].astype(BF16)
        if not last:
            yb_ctx = _attn_ctx_call(proj, ks, vs, qw, nctx)
            h_prev, xn, g_row = _outproj_mid_call(yssm, h_f, h_b, yb_ctx, yb_lat, proj, d_row, w_glu, mnorm,
                                                  w_o, x, ctx, mod, norm_w3, w_gr[layer + 1], b_gr[layer + 1],
                                                  layer)
        else:
            out = _outproj_last_call(yssm, h_f, h_b, yb_lat, proj, d_row, w_glu, mnorm, w_o, h_prev, mod,
                                     final_norm_w.reshape(1, d), layer, nctx)
    return out
```

```python
import functools
import math

import jax
import jax.numpy as jnp
from jax import lax
from jax.experimental import pallas as pl
from jax.experimental.pallas import tpu as pltpu

F32 = jnp.float32
BF16 = jnp.bfloat16
EPS = 1e-6

SSM_W = 512
SSM_P = 16
SSM_G = 32
SSM_N = 64
S5_T = 16
S5_SLAB_G = 8
S5_SLABS = SSM_G // S5_SLAB_G
S5_ST = S5_SLAB_G * SSM_N
AH = 8
AKV = 2
AGRP = AH // AKV
HD = 128
AW = AH * HD
KVW = AKV * HD
GRID_W = 64
ROPE_THETA = 10000.0
MH = 4
MD = 128
MW = MH * MD
MCH = 128
C_U, C_Z, C_AQ, C_AK, C_AV, C_AZ, C_MQ, C_MK, C_MV, C_MO, C_MZ = (
    0, 512, 1024, 2048, 2304, 2560, 3584, 4096, 4608, 5120, 5632)
PROJ_W = 6144
MIX_W = SSM_W + AW + MW

ROW_TILE = 256
VMEM_LIMIT = 48 * 1024 * 1024


def _cparams(*sem):
    return pltpu.CompilerParams(dimension_semantics=sem, vmem_limit_bytes=VMEM_LIMIT)


def _dot(a, b):
    return jnp.dot(a, b, preferred_element_type=F32)


def _dot_nt(a, b):
    return lax.dot_general(a, b, (((1,), (1,)), ((), ())), preferred_element_type=F32)


def _dot_tn(a, b):
    return lax.dot_general(a, b, (((0,), (0,)), ((), ())), preferred_element_type=F32)


def _split3(x):
    hi = x.astype(BF16)
    r1 = x - hi.astype(F32)
    mid = r1.astype(BF16)
    lo = (r1 - mid.astype(F32)).astype(BF16)
    return hi, mid, lo


def _silu(x):
    return x * jax.nn.sigmoid(x)


def _log_sigmoid(x):
    return jnp.minimum(x, 0.0) - jnp.log1p(jnp.exp(-jnp.abs(x)))


def _rms(x, w):
    return x * lax.rsqrt(jnp.mean(x * x, axis=-1, keepdims=True) + EPS) * w


def _pick_tile(n, cap, mult):
    best = mult
    for t in range(mult, min(n, cap) + 1, mult):
        if n % t == 0:
            best = t
    return best


def _ada_kernel(c_ref, w_ref, b_ref, o_ref):
    a = _silu(c_ref[...])
    w = w_ref[0]
    a_hi = a.astype(BF16)
    a_lo = (a - a_hi.astype(F32)).astype(BF16)
    w_hi = w.astype(BF16)
    w_lo = (w - w_hi.astype(F32)).astype(BF16)
    o_ref[0] = _dot(a_hi, w_hi) + _dot(a_hi, w_lo) + _dot(a_lo, w_hi) + b_ref[0]


def _ada_call(c_rows, ada_w, ada_b):
    depth, d, n3 = ada_w.shape
    tn = 512
    return pl.pallas_call(
        _ada_kernel,
        grid=(depth, n3 // tn),
        in_specs=[
            pl.BlockSpec((8, d), lambda l, j: (0, 0)),
            pl.BlockSpec((1, d, tn), lambda l, j: (l, 0, j)),
            pl.BlockSpec((1, 1, tn), lambda l, j: (l, 0, j)),
        ],
        out_specs=pl.BlockSpec((1, 8, tn), lambda l, j: (l, 0, j)),
        out_shape=jax.ShapeDtypeStruct((depth, 8, n3), F32),
        compiler_params=_cparams("parallel", "parallel"),
        name="ada_mod",
    )(c_rows, ada_w, ada_b.reshape(depth, 1, n3))


def _prenorm_kernel(x_ref, ctx_ref, nw_ref, sh_ref, sc_ref, wgr_ref, bgr_ref, o_ref, gr_ref, *, nct):
    h = jnp.where(pl.program_id(1) < nct, ctx_ref[0], x_ref[0])
    xn = (_rms(h, nw_ref[0]) * (1.0 + sc_ref[0]) + sh_ref[0]).astype(BF16)
    o_ref[0] = xn
    for dirn in range(2):
        gr_ref[0, dirn] = _dot_nt(wgr_ref[dirn], xn) + bgr_ref[dirn][:, :1]


def _mod_spec(d, layer, part, nct, nb):
    def idx(b, i):
        return (layer * 8 + jnp.where(i < nct, nb, b), 0, part)
    return pl.BlockSpec((1, 1, d), idx)


def _prenorm_call(x, ctx, nw, mod, w_gr, b_gr, layer):
    bsz, n, d = x.shape
    nctx = ctx.shape[1]
    tr = ROW_TILE
    nct = nctx // tr
    s = nctx + n
    return pl.pallas_call(
        functools.partial(_prenorm_kernel, nct=nct),
        grid=(bsz, s // tr),
        in_specs=[
            pl.BlockSpec((1, tr, d), lambda b, i: (b, jnp.maximum(i - nct, 0), 0)),
            pl.BlockSpec((1, tr, d), lambda b, i: (b, jnp.minimum(i, nct - 1), 0)),
            pl.BlockSpec((1, 1, d), lambda b, i: (layer, 0, 0)),
            _mod_spec(d, layer, 0, nct, bsz),
            _mod_spec(d, layer, 1, nct, bsz),
            pl.BlockSpec((2, 8, d), lambda b, i: (0, 0, 0)),
            pl.BlockSpec((2, 8, 128), lambda b, i: (0, 0, 0)),
        ],
        out_specs=[
            pl.BlockSpec((1, tr, d), lambda b, i: (b, i, 0)),
            pl.BlockSpec((1, 2, 8, tr), lambda b, i: (b, 0, 0, i)),
        ],
        out_shape=[
            jax.ShapeDtypeStruct((bsz, s, d), BF16),
            jax.ShapeDtypeStruct((bsz, 2, 8, s), F32),
        ],
        compiler_params=_cparams("parallel", "parallel"),
        name="prenorm",
    )(x, ctx, nw, mod, mod, w_gr, b_gr)


def _matmul_kernel(x_ref, w_ref, o_ref):
    o_ref[...] = _dot(x_ref[...], w_ref[...])


def _inproj_call(xn2d, w):
    m, d = xn2d.shape
    n = w.shape[1]
    tm = _pick_tile(m, 1056, 16)
    tn = 1024
    return pl.pallas_call(
        _matmul_kernel,
        grid=(n // tn, m // tm),
        in_specs=[
            pl.BlockSpec((tm, d), lambda j, i: (i, 0)),
            pl.BlockSpec((d, tn), lambda j, i: (0, j)),
        ],
        out_specs=pl.BlockSpec((tm, tn), lambda j, i: (i, j)),
        out_shape=jax.ShapeDtypeStruct((m, n), F32),
        compiler_params=_cparams("parallel", "parallel"),
        name="in_proj",
    )(xn2d, w)


def _s5_weights(a_re, a_im, log_dt, b_re, b_im, c_re, c_im):
    t_ = S5_T
    a_re = a_re.astype(F32)
    a_im = a_im.astype(F32)
    dt = jnp.exp(log_dt.astype(F32))[..., None]
    mag = jnp.exp(a_re * dt)
    lam_re = mag * jnp.cos(a_im * dt)
    lam_im = mag * jnp.sin(a_im * dt)
    inv_abs2 = 1.0 / (a_re * a_re + a_im * a_im)
    num_re, num_im = lam_re - 1.0, lam_im
    f_re = (num_re * a_re + num_im * a_im) * inv_abs2
    f_im = (num_im * a_re - num_re * a_im) * inv_abs2
    b_re = b_re.astype(F32)[None]
    b_im = b_im.astype(F32)[None]
    bb_re = f_re[:, :, None, :] * b_re - f_im[:, :, None, :] * b_im
    bb_im = f_re[:, :, None, :] * b_im + f_im[:, :, None, :] * b_re
    c_re = c_re.astype(F32)
    c_im = c_im.astype(F32)

    pr = [jnp.ones_like(lam_re)]
    pi = [jnp.zeros_like(lam_im)]
    for _ in range(t_):
        pr.append(pr[-1] * lam_re - pi[-1] * lam_im)
        pi.append(pr[-2] * lam_im + pi[-1] * lam_re)

    def powers(exps_f, exps_b):
        re = jnp.stack([jnp.stack([pr[e][0] for e in exps_f]), jnp.stack([pr[e][1] for e in exps_b])])
        im = jnp.stack([jnp.stack([pi[e][0] for e in exps_f]), jnp.stack([pi[e][1] for e in exps_b])])
        return re[:, :, :, None, :], im[:, :, :, None, :]

    def slab_rows(w):
        return w.reshape(2, t_, S5_SLABS, 128, 2 * SSM_N).astype(BF16)

    steps = list(range(t_))
    p_re, p_im = powers([t_ - 1 - t for t in steps], steps)
    win = slab_rows(jnp.concatenate([bb_re[:, None] * p_re - bb_im[:, None] * p_im,
                                     bb_re[:, None] * p_im + bb_im[:, None] * p_re], axis=-1))
    p_re, p_im = powers([t + 1 for t in steps], [t_ - t for t in steps])
    wout = slab_rows(jnp.concatenate([c_re[:, None] * p_re - c_im[:, None] * p_im,
                                      -(c_re[:, None] * p_im + c_im[:, None] * p_re)], axis=-1))

    p_re, p_im = powers(steps, steps)
    kr = c_re[:, None] * p_re - c_im[:, None] * p_im
    ki = c_re[:, None] * p_im + c_im[:, None] * p_re
    kk = jnp.einsum('dgqm,dtgpm->dgqtp', jnp.concatenate([bb_re, bb_im], axis=-1),
                    jnp.concatenate([kr, -ki], axis=-1), precision=lax.Precision.HIGH)
    k_f = kk[0].reshape(S5_SLABS, 128, t_ * SSM_P)
    k_b = kk[1][:, :, ::-1, :].reshape(S5_SLABS, 128, t_ * SSM_P)

    def lam_row(v):
        return v.reshape(2, S5_SLABS, 1, S5_ST)

    lam_t = jnp.concatenate([lam_row(pr[t_]), lam_row(pi[t_])], axis=-1)
    return win, wout, k_f, k_b, lam_t


def _iota(shape, dim):
    return lax.broadcasted_iota(jnp.int32, shape, dim)


def _s5_core_kernel(u_ref, win_ref, wout_ref, kf_ref, kb_ref, lam_ref, y_ref, lhs_sc, w_sc, s_sc, acc_sc,
                    *, nch, nctx_ch):
    t_ = S5_T
    n2 = 2 * SSM_N
    st2 = 2 * S5_ST
    lg_p = SSM_P.bit_length() - 1
    lg_n = SSM_N.bit_length() - 1
    lg_st = S5_ST.bit_length() - 1

    for t in range(t_):
        lhs_sc[:, t * 128:(t + 1) * 128] = u_ref[0, pl.ds(t, nch, stride=t_), :].astype(BF16)

    k1, c1 = _iota((n2, st2), 0), _iota((n2, st2), 1)
    e_in = jnp.where(((k1 >> lg_n) == (c1 >> lg_st)) & ((k1 & (SSM_N - 1)) == (c1 & (SSM_N - 1))),
                     1.0, 0.0).astype(BF16)
    r1, c1b = _iota((128, st2), 0), _iota((128, st2), 1)
    m_in = (r1 >> lg_p) == ((c1b & (S5_ST - 1)) >> lg_n)

    def expand_state_table(tab_ref, d):
        for t in range(t_):
            rows = slice(t * 128, (t + 1) * 128)
            w_sc[rows, :st2] = jnp.where(m_in, _dot(tab_ref[d, t, 0], e_in), 0.0).astype(BF16)

    for d in range(2):
        expand_state_table(win_ref, d)
        s_sc[d] = _dot(lhs_sc[...], w_sc[:, :st2])

    lam_f = lam_ref[0, 0]
    lam_b = lam_ref[1, 0]
    lrf, lif = lam_f[:, :S5_ST], lam_f[:, S5_ST:]
    lrb, lib = lam_b[:, :S5_ST], lam_b[:, S5_ST:]

    def step(k, carry):
        hrf, hif, hrb, hib = carry
        cb = jnp.where(k < nctx_ch, nctx_ch - 1 - k, nch - 1 - (k - nctx_ch))
        sf = s_sc[0, pl.ds(k, 1), :]
        sb = s_sc[1, pl.ds(cb, 1), :]
        s_sc[0, pl.ds(k, 1), :] = jnp.concatenate([hrf, hif], axis=-1)
        s_sc[1, pl.ds(cb, 1), :] = jnp.concatenate([hrb, hib], axis=-1)
        return (lrf * hrf - lif * hif + sf[:, :S5_ST], lrf * hif + lif * hrf + sf[:, S5_ST:],
                lrb * hrb - lib * hib + sb[:, :S5_ST], lrb * hib + lib * hrb + sb[:, S5_ST:])

    zero = jnp.zeros((1, S5_ST), F32)
    lax.fori_loop(0, nch, step, (zero, zero, zero, zero))

    k2, c2 = _iota((t_ * SSM_P, t_ * 128), 0), _iota((t_ * SSM_P, t_ * 128), 1)
    e_m = jnp.where(((k2 >> lg_p) == (c2 >> 7)) & ((k2 & (SSM_P - 1)) == (c2 & (SSM_P - 1))),
                    1.0, 0.0).astype(BF16)
    r2, c2b = _iota((128, t_ * 128), 0), _iota((128, t_ * 128), 1)
    m_m = (r2 >> lg_p) == ((c2b & 127) >> lg_p)
    k_f = kf_ref[0]
    k_b = kb_ref[0]
    lag_col = _iota((128, t_ * SSM_P), 1)
    for t in range(t_):
        rows = slice(t * 128, (t + 1) * 128)
        fwd = k_f if t == 0 else pltpu.roll(k_f, t * SSM_P, 1)
        bwd = k_b if t == t_ - 1 else pltpu.roll(k_b, (t + 1) * SSM_P, 1)
        mc = (jnp.where(lag_col >= t * SSM_P, fwd, 0.0)
              + jnp.where(lag_col < (t + 1) * SSM_P, bwd, 0.0)).astype(BF16)
        w_sc[rows, :] = jnp.where(m_m, _dot(mc, e_m), 0.0).astype(BF16)
    acc_sc[...] = _dot(lhs_sc[...], w_sc[...])

    for d in range(2):
        expand_state_table(wout_ref, d)
        acc_sc[...] += _dot_nt(s_sc[d].astype(BF16), w_sc[:, :st2])

    for t in range(t_):
        y_ref[0, pl.ds(t, nch, stride=t_), :] = acc_sc[:, t * 128:(t + 1) * 128]


def _s5_core_call(proj, win, wout, k_f, k_b, lam_t, nctx):
    bsz, s, _ = proj.shape
    t_ = S5_T
    nch = s // t_
    kdim = t_ * 128
    n2 = 2 * SSM_N
    st2 = 2 * S5_ST
    cu = C_U // 128
    return pl.pallas_call(
        functools.partial(_s5_core_kernel, nch=nch, nctx_ch=nctx // t_),
        grid=(S5_SLABS, bsz),
        in_specs=[
            pl.BlockSpec((1, s, 128), lambda sl, b: (b, 0, cu + sl)),
            pl.BlockSpec((2, t_, 1, 128, n2), lambda sl, b: (0, 0, sl, 0, 0)),
            pl.BlockSpec((2, t_, 1, 128, n2), lambda sl, b: (0, 0, sl, 0, 0)),
            pl.BlockSpec((1, 128, t_ * SSM_P), lambda sl, b: (sl, 0, 0)),
            pl.BlockSpec((1, 128, t_ * SSM_P), lambda sl, b: (sl, 0, 0)),
            pl.BlockSpec((2, 1, 1, st2), lambda sl, b: (0, sl, 0, 0)),
        ],
        out_specs=pl.BlockSpec((1, s, 128), lambda sl, b: (b, 0, sl)),
        out_shape=jax.ShapeDtypeStruct((bsz, s, SSM_W), F32),
        scratch_shapes=[
            pltpu.VMEM((nch, kdim), BF16),
            pltpu.VMEM((kdim, kdim), BF16),
            pltpu.VMEM((2, nch, st2), F32),
            pltpu.VMEM((nch, kdim), F32),
        ],
        compiler_params=pltpu.CompilerParams(dimension_semantics=("parallel", "parallel"),
                                             vmem_limit_bytes=56 * 1024 * 1024),
        name="s5_core",
    )(proj, win, wout, k_f, k_b, lam_t)


def _s5_branch(proj, p, nctx):
    win, wout, k_f, k_b, lam_t = _s5_weights(p['a_re'], p['a_im'], p['log_dt'], p['b_re'], p['b_im'],
                                             p['c_re'], p['c_im'])
    return _s5_core_call(proj, win, wout, k_f, k_b, lam_t, nctx)


def _rope_tables(n_lat, nctx):
    rows = n_lat // GRID_W
    row = jnp.broadcast_to(jnp.arange(rows, dtype=F32)[:, None], (rows, GRID_W)).reshape(-1)
    col = jnp.broadcast_to(jnp.arange(GRID_W, dtype=F32)[None, :], (rows, GRID_W)).reshape(-1)
    n_freq = HD // 4
    inv = ROPE_THETA ** (-jnp.arange(n_freq, dtype=F32) / n_freq)
    ang = jnp.concatenate([row[:, None] * inv, col[:, None] * inv], axis=-1)
    cs, sn = jnp.cos(ang), jnp.sin(ang)
    cos2 = jnp.concatenate([cs, cs], axis=-1)
    sin2 = jnp.concatenate([-sn, sn], axis=-1)
    cos_all = jnp.concatenate([jnp.ones((nctx, HD), F32), cos2], axis=0)
    sin_all = jnp.concatenate([jnp.zeros((nctx, HD), F32), sin2], axis=0)
    return cos_all, sin_all, cos2, sin2


def _norm_rope(x, w, cs, sn):
    y = _rms(x, w)
    return y * cs + pltpu.roll(y, HD // 2, 1) * sn


def _attn_kv_prep_kernel(k_ref, v_ref, cos_ref, sin_ref, kw_ref, ko_ref, vo_ref):
    cs = cos_ref[...]
    sn = sin_ref[...]
    k = k_ref[0]
    for h in range(AKV):
        ko_ref[0, :, h * HD:(h + 1) * HD] = _norm_rope(k[:, h * HD:(h + 1) * HD], kw_ref[...], cs, sn).astype(BF16)
    vo_ref[0] = v_ref[0].astype(BF16)


def _attn_kv_prep_call(proj, cos2, sin2, kw):
    bsz, s, _ = proj.shape
    tr = ROW_TILE
    return pl.pallas_call(
        _attn_kv_prep_kernel,
        grid=(bsz, s // tr),
        in_specs=[
            pl.BlockSpec((1, tr, KVW), lambda b, i: (b, i, C_AK // KVW)),
            pl.BlockSpec((1, tr, KVW), lambda b, i: (b, i, C_AV // KVW)),
            pl.BlockSpec((tr, HD), lambda b, i: (i, 0)),
            pl.BlockSpec((tr, HD), lambda b, i: (i, 0)),
            pl.BlockSpec((1, HD), lambda b, i: (0, 0)),
        ],
        out_specs=[
            pl.BlockSpec((1, tr, KVW), lambda b, i: (b, i, 0)),
            pl.BlockSpec((1, tr, KVW), lambda b, i: (b, i, 0)),
        ],
        out_shape=[
            jax.ShapeDtypeStruct((bsz, s, KVW), BF16),
            jax.ShapeDtypeStruct((bsz, s, KVW), BF16),
        ],
        compiler_params=_cparams("parallel", "parallel"),
        name="attn_kv_prep",
    )(proj, proj, cos2, sin2, kw)


Q_SCALE = HD ** -0.5 * math.log2(math.e)


def _attn_finish(acc, l, z_refs, o_ref, tq):
    o = acc / l
    ntile = len(z_refs)
    for h in range(AGRP):
        for t, z_ref in enumerate(z_refs):
            r0 = (h * ntile + t) * tq
            z = z_ref[0][:, h * HD:(h + 1) * HD]
            o_ref[0, t * tq:(t + 1) * tq, h * HD:(h + 1) * HD] = (o[r0:r0 + tq] * _silu(z)).astype(BF16)


def _attn_ctx_kernel(q_ref, z_ref, qw_ref, k_ref, v_ref, o_ref, *, tq):
    q = q_ref[0]
    q4 = jnp.concatenate([(_rms(q[:, h * HD:(h + 1) * HD], qw_ref[...]) * Q_SCALE).astype(BF16)
                          for h in range(AGRP)], axis=0)
    s = _dot_nt(q4, k_ref[0])
    m = jnp.max(s, axis=-1, keepdims=True)
    p = jnp.exp2(s - m)
    l = jnp.sum(p, axis=-1, keepdims=True)
    _attn_finish(_dot(p.astype(BF16), v_ref[0]), l, [z_ref], o_ref, tq)


def _attn_ctx_call(proj, ks, vs, qw, nctx):
    bsz = proj.shape[0]
    tq = ROW_TILE
    zw = AGRP * HD
    return pl.pallas_call(
        functools.partial(_attn_ctx_kernel, tq=tq),
        grid=(bsz, AKV, nctx // tq),
        in_specs=[
            pl.BlockSpec((1, tq, zw), lambda b, g, i: (b, i, C_AQ // zw + g)),
            pl.BlockSpec((1, tq, zw), lambda b, g, i: (b, i, C_AZ // zw + g)),
            pl.BlockSpec((1, HD), lambda b, g, i: (0, 0)),
            pl.BlockSpec((1, nctx, HD), lambda b, g, i: (b, 0, g)),
            pl.BlockSpec((1, nctx, HD), lambda b, g, i: (b, 0, g)),
        ],
        out_specs=pl.BlockSpec((1, tq, zw), lambda b, g, i: (b, i, g)),
        out_shape=jax.ShapeDtypeStruct((bsz, nctx, AW), BF16),
        compiler_params=_cparams("parallel", "parallel", "parallel"),
        name="attention_ctx",
    )(proj, proj, qw, ks, vs)


def _attn_chunks(s):
    n128 = s // 128
    cnt = max(2, 2 * round(n128 / 22))
    base, extra = divmod(n128, cnt)
    sizes = [(base + (1 if j < extra else 0)) * 128 for j in range(cnt)]
    starts = [sum(sizes[:j]) for j in range(cnt)]
    return list(zip(starts, sizes))


def _attn_lat_kernel(q_ref, qn_ref, z_ref, cos_ref, sin_ref, cosn_ref, sinn_ref, qw_ref, k_ref, v_ref,
                     o_ref, q_sc, s_sc, p_sc, acc_sc, l_sc, *, tq, chunks):
    i = pl.program_id(2)
    nck = len(chunks)
    st_l, sz_l = chunks[-1]

    def prep(qr, cr, sr):
        q = qr[0]
        return jnp.concatenate(
            [(_norm_rope(q[:, h * HD:(h + 1) * HD], qw_ref[...], cr[...], sr[...]) * Q_SCALE).astype(BF16)
             for h in range(AGRP)], axis=0)

    def scores(j):
        st, sz = chunks[j % nck]
        s_sc[j % 2, :, :sz] = _dot_nt(q_sc[...], k_ref[0, st:st + sz, :])

    @pl.when(i == 0)
    def _():
        q_sc[...] = prep(q_ref, cos_ref, sin_ref)
        scores(0)
        p_sc[...] = jnp.zeros_like(p_sc)
        acc_sc[...] = jnp.zeros_like(acc_sc)
        l_sc[...] = jnp.ones_like(l_sc)

    acc_prev = acc_sc[...] + _dot(p_sc[...], v_ref[0, st_l:st_l + sz_l, :])
    _attn_finish(acc_prev, l_sc[...], [z_ref], o_ref, tq)

    m = l = acc = None
    for j in range(nck):
        st, sz = chunks[j]
        if j + 1 == nck:
            q_sc[...] = prep(qn_ref, cosn_ref, sinn_ref)
        scores(j + 1)
        s = s_sc[j % 2, :, :sz]
        s_max = jnp.max(s, axis=-1, keepdims=True)
        m_new = s_max if j == 0 else jnp.maximum(m, s_max)
        p = jnp.exp2(s - m_new)
        p_sum = jnp.sum(p, axis=-1, keepdims=True)
        if j == 0:
            l, acc = p_sum, _dot(p.astype(BF16), v_ref[0, st:st + sz, :])
        else:
            alpha = jnp.exp2(m - m_new)
            l = alpha * l + p_sum
            if j + 1 < nck:
                acc = alpha * acc + _dot(p.astype(BF16), v_ref[0, st:st + sz, :])
            else:
                p_sc[...] = p.astype(BF16)
                acc_sc[...] = alpha * acc
                l_sc[...] = l
        m = m_new


def _attn_lat_call(proj, ks, vs, cos_lat, sin_lat, qw, nctx):
    bsz, s, _ = proj.shape
    n_lat = s - nctx
    tq = ROW_TILE
    nct = nctx // tq
    chunks = _attn_chunks(s)
    assert len(chunks) % 2 == 0 and s % 128 == 0
    ck = max(sz for _, sz in chunks)
    zw = AGRP * HD
    cq = C_AQ // zw
    cz = C_AZ // zw
    nlt = n_lat // tq

    def cur(i):
        return jnp.minimum(i, nlt - 1)

    def nxt(i):
        return jnp.minimum(i + 1, nlt - 1)

    def prv(i):
        return jnp.maximum(i - 1, 0)

    rows = AGRP * tq
    return pl.pallas_call(
        functools.partial(_attn_lat_kernel, tq=tq, chunks=chunks),
        grid=(bsz, AKV, nlt + 1),
        in_specs=[
            pl.BlockSpec((1, tq, zw), lambda b, g, i: (b, nct + cur(i), cq + g)),
            pl.BlockSpec((1, tq, zw), lambda b, g, i: (b, nct + nxt(i), cq + g)),
            pl.BlockSpec((1, tq, zw), lambda b, g, i: (b, nct + prv(i), cz + g)),
            pl.BlockSpec((tq, HD), lambda b, g, i: (cur(i), 0)),
            pl.BlockSpec((tq, HD), lambda b, g, i: (cur(i), 0)),
            pl.BlockSpec((tq, HD), lambda b, g, i: (nxt(i), 0)),
            pl.BlockSpec((tq, HD), lambda b, g, i: (nxt(i), 0)),
            pl.BlockSpec((1, HD), lambda b, g, i: (0, 0)),
            pl.BlockSpec((1, s, HD), lambda b, g, i: (b, 0, g)),
            pl.BlockSpec((1, s, HD), lambda b, g, i: (b, 0, g)),
        ],
        out_specs=pl.BlockSpec((1, tq, zw), lambda b, g, i: (b, prv(i), g)),
        out_shape=jax.ShapeDtypeStruct((bsz, n_lat, AW), BF16),
        scratch_shapes=[pltpu.VMEM((rows, HD), BF16), pltpu.VMEM((2, rows, ck), F32),
                        pltpu.VMEM((rows, chunks[-1][1]), BF16), pltpu.VMEM((rows, HD), F32),
                        pltpu.VMEM((rows, 1), F32)],
        compiler_params=_cparams("parallel", "parallel", "arbitrary"),
        name="attention",
    )(proj, proj, proj, cos_lat, sin_lat, cos_lat, sin_lat, qw, ks, vs)


def _mlstm_prep_kernel(q_ref, k_ref, qp_ref, kp_ref, qn_ref, kn_ref, w_ref, b_ref, qo_ref, ko_ref,
                       *, tr, nct, nt):
    i = pl.program_id(1)
    first = jnp.logical_or(i == 0, i == nct)
    last = jnp.logical_or(i == nct - 1, i == nt - 1)
    rid = lax.broadcasted_iota(jnp.int32, (tr, MW), 0)
    w = w_ref[...]
    bias = b_ref[...]

    def conv(x, prev_row, next_row, off):
        prev_row = jnp.where(first, 0.0, prev_row)
        next_row = jnp.where(last, 0.0, next_row)
        xp = jnp.where(rid == 0, prev_row, pltpu.roll(x, 1, 0))
        xn = jnp.where(rid == tr - 1, next_row, pltpu.roll(x, tr - 1, 0))
        y = (w[0:1, off:off + MW] * xp + w[1:2, off:off + MW] * x + w[2:3, off:off + MW] * xn
             + bias[:, off:off + MW])
        return _silu(y)

    qo_ref[0] = conv(q_ref[0], qp_ref[0, 7:8, :], qn_ref[0, 0:1, :], 0).astype(BF16)
    ko_ref[0] = (conv(k_ref[0], kp_ref[0, 7:8, :], kn_ref[0, 0:1, :], MW) * (MD ** -0.5)).astype(BF16)


def _mlstm_prep_call(proj, conv_w, conv_b, nctx):
    bsz, s, _ = proj.shape
    tr = ROW_TILE
    nt = s // tr
    nct = nctx // tr
    r8 = tr // 8
    n8 = s // 8
    cq = C_MQ // MW
    ck = C_MK // MW

    def prev_map(c):
        return lambda b, i: (b, jnp.maximum(i * r8 - 1, 0), c)

    def next_map(c):
        return lambda b, i: (b, jnp.minimum((i + 1) * r8, n8 - 1), c)

    return pl.pallas_call(
        functools.partial(_mlstm_prep_kernel, tr=tr, nct=nct, nt=nt),
        grid=(bsz, nt),
        in_specs=[
            pl.BlockSpec((1, tr, MW), lambda b, i: (b, i, cq)),
            pl.BlockSpec((1, tr, MW), lambda b, i: (b, i, ck)),
            pl.BlockSpec((1, 8, MW), prev_map(cq)),
            pl.BlockSpec((1, 8, MW), prev_map(ck)),
            pl.BlockSpec((1, 8, MW), next_map(cq)),
            pl.BlockSpec((1, 8, MW), next_map(ck)),
            pl.BlockSpec((8, 2 * MW), lambda b, i: (0, 0)),
            pl.BlockSpec((1, 2 * MW), lambda b, i: (0, 0)),
        ],
        out_specs=[
            pl.BlockSpec((1, tr, MW), lambda b, i: (b, i, 0)),
            pl.BlockSpec((1, tr, MW), lambda b, i: (b, i, 0)),
        ],
        out_shape=[
            jax.ShapeDtypeStruct((bsz, s, MW), BF16),
            jax.ShapeDtypeStruct((bsz, s, MW), BF16),
        ],
        compiler_params=_cparams("parallel", "parallel"),
        name="mlstm_prep",
    )(proj, proj, proj, proj, proj, proj, conv_w, conv_b)


def _mlstm_bwd_chunk(step, nctx_ch, nch):
    return jnp.where(step < nctx_ch, nctx_ch - 1 - step, nch - 1 - (step - nctx_ch))


def _hi_mid(x):
    hi = x.astype(BF16)
    mid = (x - hi.astype(F32)).astype(BF16)
    return jnp.concatenate([hi, mid], axis=1)


def _mlstm_chunk(dirn, q, k, v, gr, o_ref, b, ct_sc, n_sc, m_sc, base):
    row = _iota((MCH, MCH), 0)
    col = _iota((MCH, MCH), 1)
    allowed = (col <= row) if dirn == 0 else (col >= row)
    eye = col == row
    tri = jnp.where(allowed, 1.0, 0.0).astype(BF16)
    lf_r = _log_sigmoid(gr)
    r3 = _split3(lf_r)
    b_r = _dot_nt(r3[0], tri) + _dot_nt(r3[1], tri) + _dot_nt(r3[2], tri)

    a8 = jnp.concatenate([gr[:MH] - b_r[MH:], gr[:MH] - b_r[MH:]], axis=0)
    lane = _iota((8, MCH), 1)
    cm8 = a8
    sh = 1
    while sh < MCH:
        if dirn == 0:
            cm8 = jnp.maximum(cm8, jnp.where(lane >= sh, pltpu.roll(cm8, sh, 1), -jnp.inf))
        else:
            cm8 = jnp.maximum(cm8, jnp.where(lane < MCH - sh, pltpu.roll(cm8, MCH - sh, 1), -jnp.inf))
        sh *= 2

    ones2 = jnp.ones((2 * MCH, MCH), BF16)
    lhs = []
    for h in range(MH):
        lhs += [_hi_mid(jnp.where(allowed, lf_r[MH + h:MH + h + 1, :], 0.0)),
                _hi_mid(jnp.where(eye, cm8[h:h + 1, :], 0.0)),
                _hi_mid(jnp.where(eye, gr[h:h + 1, :], 0.0))]
    rep = _dot(jnp.concatenate(lhs, axis=0), ones2)

    heads = []
    lhs2 = []
    for h in range(MH):
        st = base + h
        bc = rep[(3 * h) * MCH:(3 * h + 1) * MCH]
        cm = rep[(3 * h + 1) * MCH:(3 * h + 2) * MCH]
        lic = rep[(3 * h + 2) * MCH:(3 * h + 3) * MCH]
        br = b_r[MH + h:MH + h + 1, :]
        li_r = gr[h:h + 1, :]
        m_row = m_sc[st][:1, :]
        qh = q[:, h * MD:(h + 1) * MD]
        kh = k[:, h * MD:(h + 1) * MD]

        log_d = jnp.where(allowed, bc - br + li_r, -jnp.inf)
        m_t = bc + jnp.maximum(m_row, cm)
        s = _dot_nt(qh, kh) * jnp.exp(log_d - m_t)
        s_bf = s.astype(BF16)
        lhs2 += [s_bf, (qh.astype(F32) * n_sc[st][:1, :]).astype(BF16)]
        heads.append((st, bc, lic, m_row, m_t, qh, kh, s_bf))

    rep2 = _dot(jnp.concatenate(lhs2, axis=0), jnp.ones((MCH, MCH), BF16))

    for h, (st, bc, lic, m_row, m_t, qh, kh, s_bf) in enumerate(heads):
        vh = v[:, h * MD:(h + 1) * MD]
        s_sum = rep2[(2 * h) * MCH:(2 * h + 1) * MCH]
        qn = rep2[(2 * h + 1) * MCH:(2 * h + 2) * MCH]
        w_inter = jnp.exp(bc + m_row - m_t)
        num = _dot(s_bf, vh) + w_inter * _dot(qh, ct_sc[st].astype(BF16))
        den = s_sum + w_inter * qn
        o_ref[b, :, h * MD:(h + 1) * MD] = num / jnp.maximum(jnp.abs(den), jnp.exp(-m_t))

        b_last = bc[MCH - 1:MCH, :] if dirn == 0 else bc[0:1, :]
        log_w = b_last - bc + lic
        m_new = jnp.maximum(b_last + m_row, jnp.max(log_w, axis=0, keepdims=True))
        decay = jnp.exp(b_last + m_row - m_new)
        kw = kh.astype(F32) * jnp.exp(log_w - m_new)
        ct_sc[st] = decay * ct_sc[st] + _dot_tn(kw.astype(BF16), vh)
        n_sc[st] = jnp.broadcast_to(decay * n_sc[st][:1, :] + jnp.sum(kw, axis=0, keepdims=True), (8, MD))
        m_sc[st] = jnp.broadcast_to(m_new, (8, 128))


def _mlstm_kernel(qf_ref, kf_ref, vf_ref, grf_ref, qb_ref, kb_ref, vb_ref, grb_ref,
                  of_ref, ob_ref, ct_sc, n_sc, m_sc, *, bsz):
    @pl.when(pl.program_id(0) == 0)
    def _():
        ct_sc[...] = jnp.zeros_like(ct_sc)
        n_sc[...] = jnp.zeros_like(n_sc)
        m_sc[...] = jnp.zeros_like(m_sc)

    for b in range(bsz):
        _mlstm_chunk(0, qf_ref[b], kf_ref[b], vf_ref[b].astype(BF16), grf_ref[b, 0],
                     of_ref, b, ct_sc, n_sc, m_sc, b * MH)
        _mlstm_chunk(1, qb_ref[b], kb_ref[b], vb_ref[b].astype(BF16), grb_ref[b, 0],
                     ob_ref, b, ct_sc, n_sc, m_sc, (bsz + b) * MH)


def _mlstm_call(qc, kc, proj, g_row, nctx):
    bsz, s, _ = qc.shape
    nch = s // MCH
    nctx_ch = nctx // MCH
    cv = C_MV // MW

    def cb(st):
        return _mlstm_bwd_chunk(st, nctx_ch, nch)

    nst = 2 * bsz * MH
    return pl.pallas_call(
        functools.partial(_mlstm_kernel, bsz=bsz),
        grid=(nch,),
        in_specs=[
            pl.BlockSpec((bsz, MCH, MW), lambda st: (0, st, 0)),
            pl.BlockSpec((bsz, MCH, MW), lambda st: (0, st, 0)),
            pl.BlockSpec((bsz, MCH, MW), lambda st: (0, st, cv)),
            pl.BlockSpec((bsz, 1, 8, MCH), lambda st: (0, 0, 0, st)),
            pl.BlockSpec((bsz, MCH, MW), lambda st: (0, cb(st), 0)),
            pl.BlockSpec((bsz, MCH, MW), lambda st: (0, cb(st), 0)),
            pl.BlockSpec((bsz, MCH, MW), lambda st: (0, cb(st), cv)),
            pl.BlockSpec((bsz, 1, 8, MCH), lambda st: (0, 1, 0, cb(st))),
        ],
        out_specs=[
            pl.BlockSpec((bsz, MCH, MW), lambda st: (0, st, 0)),
            pl.BlockSpec((bsz, MCH, MW), lambda st: (0, cb(st), 0)),
        ],
        out_shape=[
            jax.ShapeDtypeStruct((bsz, s, MW), F32),
            jax.ShapeDtypeStruct((bsz, s, MW), F32),
        ],
        scratch_shapes=[
            pltpu.VMEM((nst, MD, MD), F32),
            pltpu.VMEM((nst, 8, MD), F32),
            pltpu.VMEM((nst, 8, 128), F32),
        ],
        compiler_params=_cparams("arbitrary"),
        name="mlstm_scan",
    )(qc, kc, proj, g_row, qc, kc, proj, g_row)


def _mixer_specs(tr, row_of):
    def tok(col_block):
        return pl.BlockSpec((1, tr, SSM_W), lambda b, i: (b, row_of(i), col_block))
    const = lambda shape: pl.BlockSpec(shape, lambda b, i: (0,) * len(shape))
    return [tok(0), tok(C_U // SSM_W), tok(C_Z // SSM_W), const((1, SSM_W)), const((SSM_W, SSM_W)),
            tok(0), tok(0), tok(C_MO // MW), tok(C_MZ // MW), const((1, MW))]


def _mixer_outputs(y_ref, u_ref, sz_ref, d_ref, wg_ref, hf_ref, hb_ref, mo_ref, mz_ref, mn_ref):
    g = jax.nn.gelu(y_ref[0] + d_ref[...] * u_ref[0], approximate=True)
    ya = (g * jax.nn.sigmoid(_dot(g.astype(BF16), wg_ref[...])) * _silu(sz_ref[0])).astype(BF16)
    hs = hf_ref[0] + hb_ref[0]
    og, zg, nw = mo_ref[0], mz_ref[0], mn_ref[...]
    yc = []
    for h in range(MH):
        sl = slice(h * MD, (h + 1) * MD)
        yc.append((jax.nn.sigmoid(og[:, sl]) * _rms(hs[:, sl], nw[:, sl]) * _silu(zg[:, sl])).astype(BF16))
    return ya, jnp.concatenate(yc, axis=1)


def _mix_dot(ya, yb, yc, w_ref):
    return (_dot(ya, w_ref[:SSM_W, :]) + _dot(yb, w_ref[SSM_W:SSM_W + AW, :])
            + _dot(yc, w_ref[SSM_W + AW:, :]))


def _outproj_mid_kernel(*refs, nct):
    mixer, (ybc_ref, ybl_ref, w_ref, x_ref, ctx_ref, g_ref, nw_ref, sh_ref, sc_ref,
            wgr_ref, bgr_ref, h_ref, xn_ref, gr_ref) = refs[:10], refs[10:]
    is_ctx = pl.program_id(1) < nct
    ya, yc = _mixer_outputs(*mixer)
    yb = jnp.where(is_ctx, ybc_ref[0], ybl_ref[0])
    mix = _mix_dot(ya, yb, yc, w_ref)
    h = jnp.where(is_ctx, ctx_ref[0], x_ref[0]) + g_ref[0] * mix
    h_ref[0] = h
    xn = (_rms(h, nw_ref[0]) * (1.0 + sc_ref[0]) + sh_ref[0]).astype(BF16)
    xn_ref[0] = xn
    for dirn in range(2):
        gr_ref[0, dirn] = _dot_nt(wgr_ref[dirn], xn) + bgr_ref[dirn][:, :1]


def _outproj_mid_call(yssm, h_f, h_b, yb_ctx, yb_lat, proj, d_row, w_glu, mnorm, w_out, x, ctx, mod, nw,
                      w_gr, b_gr, layer):
    bsz, s, _ = yssm.shape
    d = x.shape[-1]
    nctx = ctx.shape[1]
    tr = ROW_TILE
    nct = nctx // tr
    nxt = layer + 1
    lat = lambda b, i: (b, jnp.maximum(i - nct, 0), 0)
    cxt = lambda b, i: (b, jnp.minimum(i, nct - 1), 0)
    return pl.pallas_call(
        functools.partial(_outproj_mid_kernel, nct=nct),
        grid=(bsz, s // tr),
        in_specs=_mixer_specs(tr, lambda i: i) + [
            pl.BlockSpec((1, tr, AW), cxt),
            pl.BlockSpec((1, tr, AW), lat),
            pl.BlockSpec((MIX_W, d), lambda b, i: (0, 0)),
            pl.BlockSpec((1, tr, d), lat),
            pl.BlockSpec((1, tr, d), cxt),
            _mod_spec(d, layer, 2, nct, bsz),
            pl.BlockSpec((1, 1, d), lambda b, i: (nxt, 0, 0)),
            _mod_spec(d, nxt, 0, nct, bsz),
            _mod_spec(d, nxt, 1, nct, bsz),
            pl.BlockSpec((2, 8, d), lambda b, i: (0, 0, 0)),
            pl.BlockSpec((2, 8, 128), lambda b, i: (0, 0, 0)),
        ],
        out_specs=[
            pl.BlockSpec((1, tr, d), lambda b, i: (b, i, 0)),
            pl.BlockSpec((1, tr, d), lambda b, i: (b, i, 0)),
            pl.BlockSpec((1, 2, 8, tr), lambda b, i: (b, 0, 0, i)),
        ],
        out_shape=[
            jax.ShapeDtypeStruct((bsz, s, d), F32),
            jax.ShapeDtypeStruct((bsz, s, d), BF16),
            jax.ShapeDtypeStruct((bsz, 2, 8, s), F32),
        ],
        compiler_params=_cparams("parallel", "parallel"),
        name="out_proj_mid",
    )(yssm, proj, proj, d_row, w_glu, h_f, h_b, proj, proj, mnorm,
      yb_ctx, yb_lat, w_out, x, ctx, mod, nw, mod, mod, w_gr, b_gr)


def _outproj_last_kernel(*refs):
    mixer, (ybl_ref, w_ref, h_ref, g_ref, nw_ref, o_ref) = refs[:10], refs[10:]
    ya, yc = _mixer_outputs(*mixer)
    mix = _mix_dot(ya, ybl_ref[0], yc, w_ref)
    o_ref[0] = _rms(h_ref[0] + g_ref[0] * mix, nw_ref[...])


def _outproj_last_call(yssm, h_f, h_b, yb_lat, proj, d_row, w_glu, mnorm, w_out, h_prev, mod, final_w,
                       layer, nctx):
    bsz, s, d = h_prev.shape
    tr = ROW_TILE
    nct = nctx // tr
    n_lat = s - nctx
    return pl.pallas_call(
        _outproj_last_kernel,
        grid=(bsz, n_lat // tr),
        in_specs=_mixer_specs(tr, lambda i: i + nct) + [
            pl.BlockSpec((1, tr, AW), lambda b, i: (b, i, 0)),
            pl.BlockSpec((MIX_W, d), lambda b, i: (0, 0)),
            pl.BlockSpec((1, tr, d), lambda b, i: (b, i + nct, 0)),
            pl.BlockSpec((1, 1, d), lambda b, i: (layer * 8 + b, 0, 2)),
            pl.BlockSpec((1, d), lambda b, i: (0, 0)),
        ],
        out_specs=pl.BlockSpec((1, tr, d), lambda b, i: (b, i, 0)),
        out_shape=jax.ShapeDtypeStruct((bsz, n_lat, d), F32),
        compiler_params=_cparams("parallel", "parallel"),
        name="out_proj_last",
    )(yssm, proj, proj, d_row, w_glu, h_f, h_b, proj, proj, mnorm,
      yb_lat, w_out, h_prev, mod, final_w)


def _reorder_w_in(w):
    g0 = C_MO + MW
    return jnp.concatenate([w[:, :g0], w[:, g0 + 4 * MH:]], axis=1), w[:, g0:g0 + 4 * MH]


def kernel(x, c, ctx, c_ctx, norm_w, ada_w, ada_b, w_in, mlstm_gate_b, ssm_a_re, ssm_a_im, ssm_log_dt,
           ssm_b_re, ssm_b_im, ssm_c_re, ssm_c_im, ssm_d, ssm_w_glu, attn_q_norm, attn_k_norm,
           mlstm_conv_w, mlstm_conv_b, mlstm_norm_w, w_out, final_norm_w):
    bsz, n_lat, d = x.shape
    nctx = ctx.shape[1]
    s = nctx + n_lat
    depth = norm_w.shape[0]
    assert bsz < 8 and nctx % ROW_TILE == 0 and n_lat % ROW_TILE == 0 and depth == 2

    c_rows = jnp.zeros((8, d), F32).at[:bsz].set(c).at[bsz].set(c_ctx)
    mod = _ada_call(c_rows, ada_w, ada_b).reshape(depth * 8, 1, 3 * d)
    cos_all, sin_all, cos_lat, sin_lat = _rope_tables(n_lat, nctx)

    w_main, w_gr, b_gr = [], [], []
    for layer in range(depth):
        wm, wg = _reorder_w_in(w_in[layer])
        w_main.append(wm.astype(BF16))
        w_gr.append(wg.reshape(d, 2, 2 * MH).transpose(1, 2, 0).astype(BF16))
        gb = mlstm_gate_b[layer].astype(F32).reshape(2, 2 * MH)
        b_gr.append(jnp.broadcast_to(gb[:, :, None], (2, 2 * MH, 128)))

    norm_w3 = norm_w.astype(F32).reshape(depth, 1, d)
    xn, g_row = _prenorm_call(x, ctx, norm_w3, mod, w_gr[0], b_gr[0], 0)
    h_prev = None
    out = None
    for layer in range(depth):
        last = layer == depth - 1
        proj = _inproj_call(xn.reshape(bsz * s, d), w_main[layer]).reshape(bsz, s, PROJ_W)

        yssm = _s5_branch(proj, dict(a_re=ssm_a_re[layer], a_im=ssm_a_im[layer], log_dt=ssm_log_dt[layer],
                                     b_re=ssm_b_re[layer], b_im=ssm_b_im[layer], c_re=ssm_c_re[layer],
                                     c_im=ssm_c_im[layer]), nctx)

        qw = attn_q_norm[layer].reshape(1, HD).astype(F32)
        ks, vs = _attn_kv_prep_call(proj, cos_all, sin_all, attn_k_norm[layer].reshape(1, HD).astype(F32))
        yb_lat = _attn_lat_call(proj, ks, vs, cos_lat, sin_lat, qw, nctx)

        conv_w = jnp.zeros((8, 2 * MW), F32).at[:3].set(mlstm_conv_w[layer].astype(F32))
        qc, kc = _mlstm_prep_call(proj, conv_w, mlstm_conv_b[layer].astype(F32).reshape(1, 2 * MW), nctx)
        h_f, h_b = _mlstm_call(qc, kc, proj, g_row, nctx)

        d_row = ssm_d[layer].astype(F32).reshape(1, SSM_W)
        w_glu = ssm_w_glu[layer].astype(BF16)
        mnorm = mlstm_norm_w[layer].astype(F32).reshape(1, MW)
        w_o = w_out[layer].astype(BF16)
        if not last:
            yb_ctx = _attn_ctx_call(proj, ks, vs, qw, nctx)
            h_prev, xn, g_row = _outproj_mid_call(yssm, h_f, h_b, yb_ctx, yb_lat, proj, d_row, w_glu, mnorm,
                                                  w_o, x, ctx, mod, norm_w3, w_gr[layer + 1], b_gr[layer + 1],
                                                  layer)
        else:
            out = _outproj_last_call(yssm, h_f, h_b, yb_lat, proj, d_row, w_glu, mnorm, w_o, h_prev, mod,
                                     final_norm_w.reshape(1, d), layer, nctx)
    return out
```

```python
import functools
import math

import jax
import jax.numpy as jnp
from jax import lax
from jax.experimental import pallas as pl
from jax.experimental.pallas import tpu as pltpu

F32 = jnp.float32
BF16 = jnp.bfloat16
EPS = 1e-6

SSM_W = 512
SSM_P = 16
SSM_G = 32
SSM_N = 64
S5_T = 16
S5_SLAB_G = 8
S5_SLABS = SSM_G // S5_SLAB_G
S5_ST = S5_SLAB_G * SSM_N
AH = 8
AKV = 2
AGRP = AH // AKV
HD = 128
AW = AH * HD
KVW = AKV * HD
GRID_W = 64
ROPE_THETA = 10000.0
MH = 4
MD = 128
MW = MH * MD
MCH = 128
C_U, C_Z, C_AQ, C_AK, C_AV, C_AZ, C_MQ, C_MK, C_MV, C_MO, C_MZ = (
    0, 512, 1024, 2048, 2304, 2560, 3584, 4096, 4608, 5120, 5632)
PROJ_W = 6144
MIX_W = SSM_W + AW + MW

ROW_TILE = 256
VMEM_LIMIT = 48 * 1024 * 1024


def _cparams(*sem):
    return pltpu.CompilerParams(dimension_semantics=sem, vmem_limit_bytes=VMEM_LIMIT)


def _dot(a, b):
    return jnp.dot(a, b, preferred_element_type=F32)


def _dot_nt(a, b):
    return lax.dot_general(a, b, (((1,), (1,)), ((), ())), preferred_element_type=F32)


def _dot_tn(a, b):
    return lax.dot_general(a, b, (((0,), (0,)), ((), ())), preferred_element_type=F32)


def _split3(x):
    hi = x.astype(BF16)
    r1 = x - hi.astype(F32)
    mid = r1.astype(BF16)
    lo = (r1 - mid.astype(F32)).astype(BF16)
    return hi, mid, lo


def _silu(x):
    return x * jax.nn.sigmoid(x)


def _log_sigmoid(x):
    return jnp.minimum(x, 0.0) - jnp.log1p(jnp.exp(-jnp.abs(x)))


def _rms(x, w):
    return x * lax.rsqrt(jnp.mean(x * x, axis=-1, keepdims=True) + EPS) * w


def _pick_tile(n, cap, mult):
    best = mult
    for t in range(mult, min(n, cap) + 1, mult):
        if n % t == 0:
            best = t
    return best


def _ada_kernel(c_ref, w_ref, b_ref, o_ref):
    a = _silu(c_ref[...])
    w = w_ref[0]
    a_hi = a.astype(BF16)
    a_lo = (a - a_hi.astype(F32)).astype(BF16)
    w_hi = w.astype(BF16)
    w_lo = (w - w_hi.astype(F32)).astype(BF16)
    o_ref[0] = _dot(a_hi, w_hi) + _dot(a_hi, w_lo) + _dot(a_lo, w_hi) + b_ref[0]


def _ada_call(c_rows, ada_w, ada_b):
    depth, d, n3 = ada_w.shape
    tn = 512
    return pl.pallas_call(
        _ada_kernel,
        grid=(depth, n3 // tn),
        in_specs=[
            pl.BlockSpec((8, d), lambda l, j: (0, 0)),
            pl.BlockSpec((1, d, tn), lambda l, j: (l, 0, j)),
            pl.BlockSpec((1, 1, tn), lambda l, j: (l, 0, j)),
        ],
        out_specs=pl.BlockSpec((1, 8, tn), lambda l, j: (l, 0, j)),
        out_shape=jax.ShapeDtypeStruct((depth, 8, n3), F32),
        compiler_params=_cparams("parallel", "parallel"),
        name="ada_mod",
    )(c_rows, ada_w, ada_b.reshape(depth, 1, n3))


def _prenorm_kernel(x_ref, ctx_ref, nw_ref, sh_ref, sc_ref, wgr_ref, bgr_ref, o_ref, gr_ref, *, nct):
    h = jnp.where(pl.program_id(1) < nct, ctx_ref[0], x_ref[0])
    xn = (_rms(h, nw_ref[0]) * (1.0 + sc_ref[0]) + sh_ref[0]).astype(BF16)
    o_ref[0] = xn
    for dirn in range(2):
        gr_ref[0, dirn] = _dot_nt(wgr_ref[dirn], xn) + bgr_ref[dirn][:, :1]


def _mod_spec(d, layer, part, nct, nb):
    def idx(b, i):
        return (layer * 8 + jnp.where(i < nct, nb, b), 0, part)
    return pl.BlockSpec((1, 1, d), idx)


def _prenorm_call(x, ctx, nw, mod, w_gr, b_gr, layer):
    bsz, n, d = x.shape
    nctx = ctx.shape[1]
    tr = ROW_TILE
    nct = nctx // tr
    s = nctx + n
    return pl.pallas_call(
        functools.partial(_prenorm_kernel, nct=nct),
        grid=(bsz, s // tr),
        in_specs=[
            pl.BlockSpec((1, tr, d), lambda b, i: (b, jnp.maximum(i - nct, 0), 0)),
            pl.BlockSpec((1, tr, d), lambda b, i: (b, jnp.minimum(i, nct - 1), 0)),
            pl.BlockSpec((1, 1, d), lambda b, i: (layer, 0, 0)),
            _mod_spec(d, layer, 0, nct, bsz),
            _mod_spec(d, layer, 1, nct, bsz),
            pl.BlockSpec((2, 8, d), lambda b, i: (0, 0, 0)),
            pl.BlockSpec((2, 8, 128), lambda b, i: (0, 0, 0)),
        ],
        out_specs=[
            pl.BlockSpec((1, tr, d), lambda b, i: (b, i, 0)),
            pl.BlockSpec((1, 2, 8, tr), lambda b, i: (b, 0, 0, i)),
        ],
        out_shape=[
            jax.ShapeDtypeStruct((bsz, s, d), BF16),
            jax.ShapeDtypeStruct((bsz, 2, 8, s), F32),
        ],
        compiler_params=_cparams("parallel", "parallel"),
        name="prenorm",
    )(x, ctx, nw, mod, mod, w_gr, b_gr)


def _matmul_kernel(x_ref, w_ref, o_ref):
    o_ref[...] = _dot(x_ref[...], w_ref[...])


def _inproj_call(xn2d, w):
    m, d = xn2d.shape
    n = w.shape[1]
    tm = _pick_tile(m, 1056, 16)
    tn = 1536
    return pl.pallas_call(
        _matmul_kernel,
        grid=(n // tn, m // tm),
        in_specs=[
            pl.BlockSpec((tm, d), lambda j, i: (i, 0)),
            pl.BlockSpec((d, tn), lambda j, i: (0, j)),
        ],
        out_specs=pl.BlockSpec((tm, tn), lambda j, i: (i, j)),
        out_shape=jax.ShapeDtypeStruct((m, n), F32),
        compiler_params=_cparams("parallel", "parallel"),
        name="in_proj",
    )(xn2d, w)


def _s5_weights(a_re, a_im, log_dt, b_re, b_im, c_re, c_im):
    t_ = S5_T
    a_re = a_re.astype(F32)
    a_im = a_im.astype(F32)
    dt = jnp.exp(log_dt.astype(F32))[..., None]
    mag = jnp.exp(a_re * dt)
    lam_re = mag * jnp.cos(a_im * dt)
    lam_im = mag * jnp.sin(a_im * dt)
    inv_abs2 = 1.0 / (a_re * a_re + a_im * a_im)
    num_re, num_im = lam_re - 1.0, lam_im
    f_re = (num_re * a_re + num_im * a_im) * inv_abs2
    f_im = (num_im * a_re - num_re * a_im) * inv_abs2
    b_re = b_re.astype(F32)[None]
    b_im = b_im.astype(F32)[None]
    bb_re = f_re[:, :, None, :] * b_re - f_im[:, :, None, :] * b_im
    bb_im = f_re[:, :, None, :] * b_im + f_im[:, :, None, :] * b_re
    c_re = c_re.astype(F32)
    c_im = c_im.astype(F32)

    pr = [jnp.ones_like(lam_re)]
    pi = [jnp.zeros_like(lam_im)]
    for _ in range(t_):
        pr.append(pr[-1] * lam_re - pi[-1] * lam_im)
        pi.append(pr[-2] * lam_im + pi[-1] * lam_re)

    def powers(exps_f, exps_b):
        re = jnp.stack([jnp.stack([pr[e][0] for e in exps_f]), jnp.stack([pr[e][1] for e in exps_b])])
        im = jnp.stack([jnp.stack([pi[e][0] for e in exps_f]), jnp.stack([pi[e][1] for e in exps_b])])
        return re[:, :, :, None, :], im[:, :, :, None, :]

    def slab_rows(w):
        return w.reshape(2, t_, S5_SLABS, 128, 2 * SSM_N).astype(BF16)

    steps = list(range(t_))
    p_re, p_im = powers([t_ - 1 - t for t in steps], steps)
    win = slab_rows(jnp.concatenate([bb_re[:, None] * p_re - bb_im[:, None] * p_im,
                                     bb_re[:, None] * p_im + bb_im[:, None] * p_re], axis=-1))
    p_re, p_im = powers([t + 1 for t in steps], [t_ - t for t in steps])
    wout = slab_rows(jnp.concatenate([c_re[:, None] * p_re - c_im[:, None] * p_im,
                                      -(c_re[:, None] * p_im + c_im[:, None] * p_re)], axis=-1))

    p_re, p_im = powers(steps, steps)
    kr = c_re[:, None] * p_re - c_im[:, None] * p_im
    ki = c_re[:, None] * p_im + c_im[:, None] * p_re
    kk = jnp.einsum('dgqm,dtgpm->dgqtp', jnp.concatenate([bb_re, bb_im], axis=-1),
                    jnp.concatenate([kr, -ki], axis=-1), precision=lax.Precision.HIGH)
    k_f = kk[0].reshape(S5_SLABS, 128, t_ * SSM_P)
    k_b = kk[1][:, :, ::-1, :].reshape(S5_SLABS, 128, t_ * SSM_P)

    def lam_row(v):
        return v.reshape(2, S5_SLABS, 1, S5_ST)

    lam_t = jnp.concatenate([lam_row(pr[t_]), lam_row(pi[t_])], axis=-1)
    return win, wout, k_f, k_b, lam_t


def _iota(shape, dim):
    return lax.broadcasted_iota(jnp.int32, shape, dim)


def _s5_core_kernel(u_ref, win_ref, wout_ref, kf_ref, kb_ref, lam_ref, y_ref, lhs_sc, w_sc, s_sc, acc_sc,
                    *, nch, nctx_ch):
    t_ = S5_T
    n2 = 2 * SSM_N
    st2 = 2 * S5_ST
    lg_p = SSM_P.bit_length() - 1
    lg_n = SSM_N.bit_length() - 1
    lg_st = S5_ST.bit_length() - 1

    for t in range(t_):
        lhs_sc[:, t * 128:(t + 1) * 128] = u_ref[0, pl.ds(t, nch, stride=t_), :].astype(BF16)

    k1, c1 = _iota((n2, st2), 0), _iota((n2, st2), 1)
    e_in = jnp.where(((k1 >> lg_n) == (c1 >> lg_st)) & ((k1 & (SSM_N - 1)) == (c1 & (SSM_N - 1))),
                     1.0, 0.0).astype(BF16)
    r1, c1b = _iota((128, st2), 0), _iota((128, st2), 1)
    m_in = (r1 >> lg_p) == ((c1b & (S5_ST - 1)) >> lg_n)

    def expand_state_table(tab_ref, d):
        for t in range(t_):
            rows = slice(t * 128, (t + 1) * 128)
            w_sc[rows, :st2] = jnp.where(m_in, _dot(tab_ref[d, t, 0], e_in), 0.0).astype(BF16)

    for d in range(2):
        expand_state_table(win_ref, d)
        s_sc[d] = _dot(lhs_sc[...], w_sc[:, :st2])

    lam_f = lam_ref[0, 0]
    lam_b = lam_ref[1, 0]
    lrf, lif = lam_f[:, :S5_ST], lam_f[:, S5_ST:]
    lrb, lib = lam_b[:, :S5_ST], lam_b[:, S5_ST:]

    def step(k, carry):
        hrf, hif, hrb, hib = carry
        cb = jnp.where(k < nctx_ch, nctx_ch - 1 - k, nch - 1 - (k - nctx_ch))
        sf = s_sc[0, pl.ds(k, 1), :]
        sb = s_sc[1, pl.ds(cb, 1), :]
        s_sc[0, pl.ds(k, 1), :] = jnp.concatenate([hrf, hif], axis=-1)
        s_sc[1, pl.ds(cb, 1), :] = jnp.concatenate([hrb, hib], axis=-1)
        return (lrf * hrf - lif * hif + sf[:, :S5_ST], lrf * hif + lif * hrf + sf[:, S5_ST:],
                lrb * hrb - lib * hib + sb[:, :S5_ST], lrb * hib + lib * hrb + sb[:, S5_ST:])

    zero = jnp.zeros((1, S5_ST), F32)
    lax.fori_loop(0, nch, step, (zero, zero, zero, zero))

    k2, c2 = _iota((t_ * SSM_P, t_ * 128), 0), _iota((t_ * SSM_P, t_ * 128), 1)
    e_m = jnp.where(((k2 >> lg_p) == (c2 >> 7)) & ((k2 & (SSM_P - 1)) == (c2 & (SSM_P - 1))),
                    1.0, 0.0).astype(BF16)
    r2, c2b = _iota((128, t_ * 128), 0), _iota((128, t_ * 128), 1)
    m_m = (r2 >> lg_p) == ((c2b & 127) >> lg_p)
    k_f = kf_ref[0]
    k_b = kb_ref[0]
    lag_col = _iota((128, t_ * SSM_P), 1)
    for t in range(t_):
        rows = slice(t * 128, (t + 1) * 128)
        fwd = k_f if t == 0 else pltpu.roll(k_f, t * SSM_P, 1)
        bwd = k_b if t == t_ - 1 else pltpu.roll(k_b, (t + 1) * SSM_P, 1)
        mc = (jnp.where(lag_col >= t * SSM_P, fwd, 0.0)
              + jnp.where(lag_col < (t + 1) * SSM_P, bwd, 0.0)).astype(BF16)
        w_sc[rows, :] = jnp.where(m_m, _dot(mc, e_m), 0.0).astype(BF16)
    acc_sc[...] = _dot(lhs_sc[...], w_sc[...])

    for d in range(2):
        expand_state_table(wout_ref, d)
        acc_sc[...] += _dot_nt(s_sc[d].astype(BF16), w_sc[:, :st2])

    for t in range(t_):
        y_ref[0, pl.ds(t, nch, stride=t_), :] = acc_sc[:, t * 128:(t + 1) * 128]


def _s5_core_call(proj, win, wout, k_f, k_b, lam_t, nctx):
    bsz, s, _ = proj.shape
    t_ = S5_T
    nch = s // t_
    kdim = t_ * 128
    n2 = 2 * SSM_N
    st2 = 2 * S5_ST
    cu = C_U // 128
    return pl.pallas_call(
        functools.partial(_s5_core_kernel, nch=nch, nctx_ch=nctx // t_),
        grid=(S5_SLABS, bsz),
        in_specs=[
            pl.BlockSpec((1, s, 128), lambda sl, b: (b, 0, cu + sl)),
            pl.BlockSpec((2, t_, 1, 128, n2), lambda sl, b: (0, 0, sl, 0, 0)),
            pl.BlockSpec((2, t_, 1, 128, n2), lambda sl, b: (0, 0, sl, 0, 0)),
            pl.BlockSpec((1, 128, t_ * SSM_P), lambda sl, b: (sl, 0, 0)),
            pl.BlockSpec((1, 128, t_ * SSM_P), lambda sl, b: (sl, 0, 0)),
            pl.BlockSpec((2, 1, 1, st2), lambda sl, b: (0, sl, 0, 0)),
        ],
        out_specs=pl.BlockSpec((1, s, 128), lambda sl, b: (b, 0, sl)),
        out_shape=jax.ShapeDtypeStruct((bsz, s, SSM_W), F32),
        scratch_shapes=[
            pltpu.VMEM((nch, kdim), BF16),
            pltpu.VMEM((kdim, kdim), BF16),
            pltpu.VMEM((2, nch, st2), F32),
            pltpu.VMEM((nch, kdim), F32),
        ],
        compiler_params=pltpu.CompilerParams(dimension_semantics=("parallel", "parallel"),
                                             vmem_limit_bytes=56 * 1024 * 1024),
        name="s5_core",
    )(proj, win, wout, k_f, k_b, lam_t)


def _s5_branch(proj, p, nctx):
    win, wout, k_f, k_b, lam_t = _s5_weights(p['a_re'], p['a_im'], p['log_dt'], p['b_re'], p['b_im'],
                                             p['c_re'], p['c_im'])
    return _s5_core_call(proj, win, wout, k_f, k_b, lam_t, nctx)


def _rope_tables(n_lat, nctx):
    rows = n_lat // GRID_W
    row = jnp.broadcast_to(jnp.arange(rows, dtype=F32)[:, None], (rows, GRID_W)).reshape(-1)
    col = jnp.broadcast_to(jnp.arange(GRID_W, dtype=F32)[None, :], (rows, GRID_W)).reshape(-1)
    n_freq = HD // 4
    inv = ROPE_THETA ** (-jnp.arange(n_freq, dtype=F32) / n_freq)
    ang = jnp.concatenate([row[:, None] * inv, col[:, None] * inv], axis=-1)
    cs, sn = jnp.cos(ang), jnp.sin(ang)
    cos2 = jnp.concatenate([cs, cs], axis=-1)
    sin2 = jnp.concatenate([-sn, sn], axis=-1)
    cos_all = jnp.concatenate([jnp.ones((nctx, HD), F32), cos2], axis=0)
    sin_all = jnp.concatenate([jnp.zeros((nctx, HD), F32), sin2], axis=0)
    return cos_all, sin_all, cos2, sin2


def _norm_rope(x, w, cs, sn):
    y = _rms(x, w)
    return y * cs + pltpu.roll(y, HD // 2, 1) * sn


def _attn_kv_prep_kernel(k_ref, v_ref, cos_ref, sin_ref, kw_ref, ko_ref, vo_ref):
    cs = cos_ref[...]
    sn = sin_ref[...]
    k = k_ref[0]
    for h in range(AKV):
        ko_ref[0, :, h * HD:(h + 1) * HD] = _norm_rope(k[:, h * HD:(h + 1) * HD], kw_ref[...], cs, sn).astype(BF16)
    vo_ref[0] = v_ref[0].astype(BF16)


def _attn_kv_prep_call(proj, cos2, sin2, kw):
    bsz, s, _ = proj.shape
    tr = _pick_tile(s, 3 * ROW_TILE, ROW_TILE)
    return pl.pallas_call(
        _attn_kv_prep_kernel,
        grid=(bsz, s // tr),
        in_specs=[
            pl.BlockSpec((1, tr, KVW), lambda b, i: (b, i, C_AK // KVW)),
            pl.BlockSpec((1, tr, KVW), lambda b, i: (b, i, C_AV // KVW)),
            pl.BlockSpec((tr, HD), lambda b, i: (i, 0)),
            pl.BlockSpec((tr, HD), lambda b, i: (i, 0)),
            pl.BlockSpec((1, HD), lambda b, i: (0, 0)),
        ],
        out_specs=[
            pl.BlockSpec((1, tr, KVW), lambda b, i: (b, i, 0)),
            pl.BlockSpec((1, tr, KVW), lambda b, i: (b, i, 0)),
        ],
        out_shape=[
            jax.ShapeDtypeStruct((bsz, s, KVW), BF16),
            jax.ShapeDtypeStruct((bsz, s, KVW), BF16),
        ],
        compiler_params=_cparams("parallel", "parallel"),
        name="attn_kv_prep",
    )(proj, proj, cos2, sin2, kw)


Q_SCALE = HD ** -0.5 * math.log2(math.e)


def _attn_finish(acc, l, z_refs, o_ref, tq):
    o = acc / l
    ntile = len(z_refs)
    for h in range(AGRP):
        for t, z_ref in enumerate(z_refs):
            r0 = (h * ntile + t) * tq
            z = z_ref[0][:, h * HD:(h + 1) * HD]
            o_ref[0, t * tq:(t + 1) * tq, h * HD:(h + 1) * HD] = (o[r0:r0 + tq] * _silu(z)).astype(BF16)


def _attn_ctx_kernel(q_ref, z_ref, qw_ref, k_ref, v_ref, o_ref, *, tq):
    q = q_ref[0]
    q4 = jnp.concatenate([(_rms(q[:, h * HD:(h + 1) * HD], qw_ref[...]) * Q_SCALE).astype(BF16)
                          for h in range(AGRP)], axis=0)
    s = _dot_nt(q4, k_ref[0])
    m = jnp.max(s, axis=-1, keepdims=True)
    p = jnp.exp2(s - m)
    l = jnp.sum(p, axis=-1, keepdims=True)
    _attn_finish(_dot(p.astype(BF16), v_ref[0]), l, [z_ref], o_ref, tq)


def _attn_ctx_call(proj, ks, vs, qw, nctx):
    bsz = proj.shape[0]
    tq = ROW_TILE
    zw = AGRP * HD
    return pl.pallas_call(
        functools.partial(_attn_ctx_kernel, tq=tq),
        grid=(bsz, AKV, nctx // tq),
        in_specs=[
            pl.BlockSpec((1, tq, zw), lambda b, g, i: (b, i, C_AQ // zw + g)),
            pl.BlockSpec((1, tq, zw), lambda b, g, i: (b, i, C_AZ // zw + g)),
            pl.BlockSpec((1, HD), lambda b, g, i: (0, 0)),
            pl.BlockSpec((1, nctx, HD), lambda b, g, i: (b, 0, g)),
            pl.BlockSpec((1, nctx, HD), lambda b, g, i: (b, 0, g)),
        ],
        out_specs=pl.BlockSpec((1, tq, zw), lambda b, g, i: (b, i, g)),
        out_shape=jax.ShapeDtypeStruct((bsz, nctx, AW), BF16),
        compiler_params=_cparams("parallel", "parallel", "parallel"),
        name="attention_ctx",
    )(proj, proj, qw, ks, vs)


def _attn_lat_kernel(q_ref, qn_ref, z_ref, cos_ref, sin_ref, cosn_ref, sinn_ref, qw_ref, k_ref, v_ref,
                     o_ref, q_sc, s_sc, *, tq, ck, nck):
    i = pl.program_id(2)

    def prep(qr, cr, sr):
        q = qr[0]
        return jnp.concatenate(
            [(_norm_rope(q[:, h * HD:(h + 1) * HD], qw_ref[...], cr[...], sr[...]) * Q_SCALE).astype(BF16)
             for h in range(AGRP)], axis=0)

    base = (i * nck) % 2 if nck % 2 else 0

    def slot(j):
        return (base + j) % 2

    @pl.when(i == 0)
    def _():
        q_sc[...] = prep(q_ref, cos_ref, sin_ref)
        s_sc[slot(0)] = _dot_nt(q_sc[...], k_ref[0, 0:ck, :])

    m = l = acc = None
    for j in range(nck):
        if j + 1 < nck:
            s_sc[slot(j + 1)] = _dot_nt(q_sc[...], k_ref[0, (j + 1) * ck:(j + 2) * ck, :])
        else:
            q_sc[...] = prep(qn_ref, cosn_ref, sinn_ref)
            s_sc[slot(j + 1)] = _dot_nt(q_sc[...], k_ref[0, 0:ck, :])
        s = s_sc[slot(j)]
        s_max = jnp.max(s, axis=-1, keepdims=True)
        m_new = s_max if j == 0 else jnp.maximum(m, s_max)
        p = jnp.exp2(s - m_new)
        pv = _dot(p.astype(BF16), v_ref[0, j * ck:(j + 1) * ck, :])
        p_sum = jnp.sum(p, axis=-1, keepdims=True)
        if j == 0:
            l, acc = p_sum, pv
        else:
            alpha = jnp.exp2(m - m_new)
            l = alpha * l + p_sum
            acc = alpha * acc + pv
        m = m_new
    _attn_finish(acc, l, [z_ref], o_ref, tq)


def _attn_lat_call(proj, ks, vs, cos_lat, sin_lat, qw, nctx):
    bsz, s, _ = proj.shape
    n_lat = s - nctx
    tq = ROW_TILE
    nct = nctx // tq
    ck = _pick_tile(s, 1536, 128)
    nck = s // ck
    zw = AGRP * HD
    cq = C_AQ // zw
    cz = C_AZ // zw
    nlt = n_lat // tq

    def nxt(i):
        return jnp.minimum(i + 1, nlt - 1)

    return pl.pallas_call(
        functools.partial(_attn_lat_kernel, tq=tq, ck=ck, nck=nck),
        grid=(bsz, AKV, nlt),
        in_specs=[
            pl.BlockSpec((1, tq, zw), lambda b, g, i: (b, nct + i, cq + g)),
            pl.BlockSpec((1, tq, zw), lambda b, g, i: (b, nct + nxt(i), cq + g)),
            pl.BlockSpec((1, tq, zw), lambda b, g, i: (b, nct + i, cz + g)),
            pl.BlockSpec((tq, HD), lambda b, g, i: (i, 0)),
            pl.BlockSpec((tq, HD), lambda b, g, i: (i, 0)),
            pl.BlockSpec((tq, HD), lambda b, g, i: (nxt(i), 0)),
            pl.BlockSpec((tq, HD), lambda b, g, i: (nxt(i), 0)),
            pl.BlockSpec((1, HD), lambda b, g, i: (0, 0)),
            pl.BlockSpec((1, s, HD), lambda b, g, i: (b, 0, g)),
            pl.BlockSpec((1, s, HD), lambda b, g, i: (b, 0, g)),
        ],
        out_specs=pl.BlockSpec((1, tq, zw), lambda b, g, i: (b, i, g)),
        out_shape=jax.ShapeDtypeStruct((bsz, n_lat, AW), BF16),
        scratch_shapes=[pltpu.VMEM((AGRP * tq, HD), BF16), pltpu.VMEM((2, AGRP * tq, ck), F32)],
        compiler_params=_cparams("parallel", "parallel", "arbitrary"),
        name="attention",
    )(proj, proj, proj, cos_lat, sin_lat, cos_lat, sin_lat, qw, ks, vs)


def _mlstm_prep_kernel(q_ref, k_ref, qp_ref, kp_ref, qn_ref, kn_ref, w_ref, b_ref, qo_ref, ko_ref,
                       *, tr, nct, nt):
    i = pl.program_id(1)
    first = jnp.logical_or(i == 0, i == nct)
    last = jnp.logical_or(i == nct - 1, i == nt - 1)
    rid = lax.broadcasted_iota(jnp.int32, (tr, MW), 0)
    w = w_ref[...]
    bias = b_ref[...]

    def conv(x, prev_row, next_row, off):
        prev_row = jnp.where(first, 0.0, prev_row)
        next_row = jnp.where(last, 0.0, next_row)
        xp = jnp.where(rid == 0, prev_row, pltpu.roll(x, 1, 0))
        xn = jnp.where(rid == tr - 1, next_row, pltpu.roll(x, tr - 1, 0))
        y = (w[0:1, off:off + MW] * xp + w[1:2, off:off + MW] * x + w[2:3, off:off + MW] * xn
             + bias[:, off:off + MW])
        return _silu(y)

    qo_ref[0] = conv(q_ref[0], qp_ref[0, 7:8, :], qn_ref[0, 0:1, :], 0).astype(BF16)
    ko_ref[0] = (conv(k_ref[0], kp_ref[0, 7:8, :], kn_ref[0, 0:1, :], MW) * (MD ** -0.5)).astype(BF16)


def _mlstm_prep_call(proj, conv_w, conv_b, nctx):
    bsz, s, _ = proj.shape
    tr = ROW_TILE
    nt = s // tr
    nct = nctx // tr
    r8 = tr // 8
    n8 = s // 8
    cq = C_MQ // MW
    ck = C_MK // MW

    def prev_map(c):
        return lambda b, i: (b, jnp.maximum(i * r8 - 1, 0), c)

    def next_map(c):
        return lambda b, i: (b, jnp.minimum((i + 1) * r8, n8 - 1), c)

    return pl.pallas_call(
        functools.partial(_mlstm_prep_kernel, tr=tr, nct=nct, nt=nt),
        grid=(bsz, nt),
        in_specs=[
            pl.BlockSpec((1, tr, MW), lambda b, i: (b, i, cq)),
            pl.BlockSpec((1, tr, MW), lambda b, i: (b, i, ck)),
            pl.BlockSpec((1, 8, MW), prev_map(cq)),
            pl.BlockSpec((1, 8, MW), prev_map(ck)),
            pl.BlockSpec((1, 8, MW), next_map(cq)),
            pl.BlockSpec((1, 8, MW), next_map(ck)),
            pl.BlockSpec((8, 2 * MW), lambda b, i: (0, 0)),
            pl.BlockSpec((1, 2 * MW), lambda b, i: (0, 0)),
        ],
        out_specs=[
            pl.BlockSpec((1, tr, MW), lambda b, i: (b, i, 0)),
            pl.BlockSpec((1, tr, MW), lambda b, i: (b, i, 0)),
        ],
        out_shape=[
            jax.ShapeDtypeStruct((bsz, s, MW), BF16),
            jax.ShapeDtypeStruct((bsz, s, MW), BF16),
        ],
        compiler_params=_cparams("parallel", "parallel"),
        name="mlstm_prep",
    )(proj, proj, proj, proj, proj, proj, conv_w, conv_b)


def _mlstm_bwd_chunk(step, nctx_ch, nch):
    return jnp.where(step < nctx_ch, nctx_ch - 1 - step, nch - 1 - (step - nctx_ch))


def _hi_mid(x):
    hi = x.astype(BF16)
    mid = (x - hi.astype(F32)).astype(BF16)
    return jnp.concatenate([hi, mid], axis=1)


def _mlstm_chunk(dirn, q, k, v, gr, o_ref, b, ct_sc, n_sc, m_sc, base):
    row = _iota((MCH, MCH), 0)
    col = _iota((MCH, MCH), 1)
    allowed = (col <= row) if dirn == 0 else (col >= row)
    eye = col == row
    tri = jnp.where(allowed, 1.0, 0.0).astype(BF16)
    lf_r = _log_sigmoid(gr)
    r3 = _split3(lf_r)
    b_r = _dot_nt(r3[0], tri) + _dot_nt(r3[1], tri) + _dot_nt(r3[2], tri)

    a8 = jnp.concatenate([gr[:MH] - b_r[MH:], gr[:MH] - b_r[MH:]], axis=0)
    lane = _iota((8, MCH), 1)
    cm8 = a8
    sh = 1
    while sh < MCH:
        if dirn == 0:
            cm8 = jnp.maximum(cm8, jnp.where(lane >= sh, pltpu.roll(cm8, sh, 1), -jnp.inf))
        else:
            cm8 = jnp.maximum(cm8, jnp.where(lane < MCH - sh, pltpu.roll(cm8, MCH - sh, 1), -jnp.inf))
        sh *= 2

    ones2 = jnp.ones((2 * MCH, MCH), BF16)
    lhs = []
    for h in range(MH):
        lhs += [_hi_mid(jnp.where(allowed, lf_r[MH + h:MH + h + 1, :], 0.0)),
                _hi_mid(jnp.where(eye, cm8[h:h + 1, :], 0.0)),
                _hi_mid(jnp.where(eye, gr[h:h + 1, :], 0.0))]
    rep = _dot(jnp.concatenate(lhs, axis=0), ones2)

    heads = []
    lhs2 = []
    for h in range(MH):
        st = base + h
        bc = rep[(3 * h) * MCH:(3 * h + 1) * MCH]
        cm = rep[(3 * h + 1) * MCH:(3 * h + 2) * MCH]
        lic = rep[(3 * h + 2) * MCH:(3 * h + 3) * MCH]
        br = b_r[MH + h:MH + h + 1, :]
        li_r = gr[h:h + 1, :]
        m_row = m_sc[st][:1, :]
        qh = q[:, h * MD:(h + 1) * MD]
        kh = k[:, h * MD:(h + 1) * MD]

        log_d = jnp.where(allowed, bc - br + li_r, -jnp.inf)
        m_t = bc + jnp.maximum(m_row, cm)
        s = _dot_nt(qh, kh) * jnp.exp(log_d - m_t)
        s_bf = s.astype(BF16)
        lhs2 += [s_bf, (qh.astype(F32) * n_sc[st][:1, :]).astype(BF16)]
        heads.append((st, bc, lic, m_row, m_t, qh, kh, s_bf))

    rep2 = _dot(jnp.concatenate(lhs2, axis=0), jnp.ones((MCH, MCH), BF16))

    for h, (st, bc, lic, m_row, m_t, qh, kh, s_bf) in enumerate(heads):
        vh = v[:, h * MD:(h + 1) * MD]
        s_sum = rep2[(2 * h) * MCH:(2 * h + 1) * MCH]
        qn = rep2[(2 * h + 1) * MCH:(2 * h + 2) * MCH]
        w_inter = jnp.exp(bc + m_row - m_t)
        num = _dot(s_bf, vh) + w_inter * _dot(qh, ct_sc[st].astype(BF16))
        den = s_sum + w_inter * qn
        o_ref[b, :, h * MD:(h + 1) * MD] = num / jnp.maximum(jnp.abs(den), jnp.exp(-m_t))

        b_last = bc[MCH - 1:MCH, :] if dirn == 0 else bc[0:1, :]
        log_w = b_last - bc + lic
        m_new = jnp.maximum(b_last + m_row, jnp.max(log_w, axis=0, keepdims=True))
        decay = jnp.exp(b_last + m_row - m_new)
        kw = kh.astype(F32) * jnp.exp(log_w - m_new)
        ct_sc[st] = decay * ct_sc[st] + _dot_tn(kw.astype(BF16), vh)
        n_sc[st] = jnp.broadcast_to(decay * n_sc[st][:1, :] + jnp.sum(kw, axis=0, keepdims=True), (8, MD))
        m_sc[st] = jnp.broadcast_to(m_new, (8, 128))


def _mlstm_kernel(qf_ref, kf_ref, vf_ref, grf_ref, qb_ref, kb_ref, vb_ref, grb_ref,
                  of_ref, ob_ref, ct_sc, n_sc, m_sc, *, bsz):
    @pl.when(pl.program_id(0) == 0)
    def _():
        ct_sc[...] = jnp.zeros_like(ct_sc)
        n_sc[...] = jnp.zeros_like(n_sc)
        m_sc[...] = jnp.zeros_like(m_sc)

    for b in range(bsz):
        _mlstm_chunk(0, qf_ref[b], kf_ref[b], vf_ref[b].astype(BF16), grf_ref[b, 0],
                     of_ref, b, ct_sc, n_sc, m_sc, b * MH)
        _mlstm_chunk(1, qb_ref[b], kb_ref[b], vb_ref[b].astype(BF16), grb_ref[b, 0],
                     ob_ref, b, ct_sc, n_sc, m_sc, (bsz + b) * MH)


def _mlstm_call(qc, kc, proj, g_row, nctx):
    bsz, s, _ = qc.shape
    nch = s // MCH
    nctx_ch = nctx // MCH
    cv = C_MV // MW

    def cb(st):
        return _mlstm_bwd_chunk(st, nctx_ch, nch)

    nst = 2 * bsz * MH
    return pl.pallas_call(
        functools.partial(_mlstm_kernel, bsz=bsz),
        grid=(nch,),
        in_specs=[
            pl.BlockSpec((bsz, MCH, MW), lambda st: (0, st, 0)),
            pl.BlockSpec((bsz, MCH, MW), lambda st: (0, st, 0)),
            pl.BlockSpec((bsz, MCH, MW), lambda st: (0, st, cv)),
            pl.BlockSpec((bsz, 1, 8, MCH), lambda st: (0, 0, 0, st)),
            pl.BlockSpec((bsz, MCH, MW), lambda st: (0, cb(st), 0)),
            pl.BlockSpec((bsz, MCH, MW), lambda st: (0, cb(st), 0)),
            pl.BlockSpec((bsz, MCH, MW), lambda st: (0, cb(st), cv)),
            pl.BlockSpec((bsz, 1, 8, MCH), lambda st: (0, 1, 0, cb(st))),
        ],
        out_specs=[
            pl.BlockSpec((bsz, MCH, MW), lambda st: (0, st, 0)),
            pl.BlockSpec((bsz, MCH, MW), lambda st: (0, cb(st), 0)),
        ],
        out_shape=[
            jax.ShapeDtypeStruct((bsz, s, MW), F32),
            jax.ShapeDtypeStruct((bsz, s, MW), F32),
        ],
        scratch_shapes=[
            pltpu.VMEM((nst, MD, MD), F32),
            pltpu.VMEM((nst, 8, MD), F32),
            pltpu.VMEM((nst, 8, 128), F32),
        ],
        compiler_params=_cparams("arbitrary"),
        name="mlstm_scan",
    )(qc, kc, proj, g_row, qc, kc, proj, g_row)


def _mixer_specs(tr, row_of):
    def tok(col_block):
        return pl.BlockSpec((1, tr, SSM_W), lambda b, i: (b, row_of(i), col_block))
    const = lambda shape: pl.BlockSpec(shape, lambda b, i: (0,) * len(shape))
    return [tok(0), tok(C_U // SSM_W), tok(C_Z // SSM_W), const((1, SSM_W)), const((SSM_W, SSM_W)),
            tok(0), tok(0), tok(C_MO // MW), tok(C_MZ // MW), const((1, MW))]


def _mixer_outputs(y_ref, u_ref, sz_ref, d_ref, wg_ref, hf_ref, hb_ref, mo_ref, mz_ref, mn_ref):
    g = jax.nn.gelu(y_ref[0] + d_ref[...] * u_ref[0], approximate=True)
    ya = (g * jax.nn.sigmoid(_dot(g.astype(BF16), wg_ref[...])) * _silu(sz_ref[0])).astype(BF16)
    hs = hf_ref[0] + hb_ref[0]
    og, zg, nw = mo_ref[0], mz_ref[0], mn_ref[...]
    yc = []
    for h in range(MH):
        sl = slice(h * MD, (h + 1) * MD)
        yc.append((jax.nn.sigmoid(og[:, sl]) * _rms(hs[:, sl], nw[:, sl]) * _silu(zg[:, sl])).astype(BF16))
    return ya, jnp.concatenate(yc, axis=1)


def _mix_dot(ya, yb, yc, w_ref):
    return (_dot(ya, w_ref[:SSM_W, :]) + _dot(yb, w_ref[SSM_W:SSM_W + AW, :])
            + _dot(yc, w_ref[SSM_W + AW:, :]))


def _outproj_mid_kernel(*refs, nct):
    mixer, (ybc_ref, ybl_ref, w_ref, x_ref, ctx_ref, g_ref, nw_ref, sh_ref, sc_ref,
            wgr_ref, bgr_ref, h_ref, xn_ref, gr_ref) = refs[:10], refs[10:]
    is_ctx = pl.program_id(1) < nct
    ya, yc = _mixer_outputs(*mixer)
    yb = jnp.where(is_ctx, ybc_ref[0], ybl_ref[0])
    mix = _mix_dot(ya, yb, yc, w_ref)
    h = jnp.where(is_ctx, ctx_ref[0], x_ref[0]) + g_ref[0] * mix
    h_ref[0] = h
    xn = (_rms(h, nw_ref[0]) * (1.0 + sc_ref[0]) + sh_ref[0]).astype(BF16)
    xn_ref[0] = xn
    for dirn in range(2):
        gr_ref[0, dirn] = _dot_nt(wgr_ref[dirn], xn) + bgr_ref[dirn][:, :1]


def _outproj_mid_call(yssm, h_f, h_b, yb_ctx, yb_lat, proj, d_row, w_glu, mnorm, w_out, x, ctx, mod, nw,
                      w_gr, b_gr, layer):
    bsz, s, _ = yssm.shape
    d = x.shape[-1]
    nctx = ctx.shape[1]
    tr = ROW_TILE
    nct = nctx // tr
    nxt = layer + 1
    lat = lambda b, i: (b, jnp.maximum(i - nct, 0), 0)
    cxt = lambda b, i: (b, jnp.minimum(i, nct - 1), 0)
    return pl.pallas_call(
        functools.partial(_outproj_mid_kernel, nct=nct),
        grid=(bsz, s // tr),
        in_specs=_mixer_specs(tr, lambda i: i) + [
            pl.BlockSpec((1, tr, AW), cxt),
            pl.BlockSpec((1, tr, AW), lat),
            pl.BlockSpec((MIX_W, d), lambda b, i: (0, 0)),
            pl.BlockSpec((1, tr, d), lat),
            pl.BlockSpec((1, tr, d), cxt),
            _mod_spec(d, layer, 2, nct, bsz),
            pl.BlockSpec((1, 1, d), lambda b, i: (nxt, 0, 0)),
            _mod_spec(d, nxt, 0, nct, bsz),
            _mod_spec(d, nxt, 1, nct, bsz),
            pl.BlockSpec((2, 8, d), lambda b, i: (0, 0, 0)),
            pl.BlockSpec((2, 8, 128), lambda b, i: (0, 0, 0)),
        ],
        out_specs=[
            pl.BlockSpec((1, tr, d), lambda b, i: (b, i, 0)),
            pl.BlockSpec((1, tr, d), lambda b, i: (b, i, 0)),
            pl.BlockSpec((1, 2, 8, tr), lambda b, i: (b, 0, 0, i)),
        ],
        out_shape=[
            jax.ShapeDtypeStruct((bsz, s, d), F32),
            jax.ShapeDtypeStruct((bsz, s, d), BF16),
            jax.ShapeDtypeStruct((bsz, 2, 8, s), F32),
        ],
        compiler_params=_cparams("parallel", "parallel"),
        name="out_proj_mid",
    )(yssm, proj, proj, d_row, w_glu, h_f, h_b, proj, proj, mnorm,
      yb_ctx, yb_lat, w_out, x, ctx, mod, nw, mod, mod, w_gr, b_gr)


def _outproj_last_kernel(*refs):
    mixer, (ybl_ref, w_ref, h_ref, g_ref, nw_ref, o_ref) = refs[:10], refs[10:]
    ya, yc = _mixer_outputs(*mixer)
    mix = _mix_dot(ya, ybl_ref[0], yc, w_ref)
    o_ref[0] = _rms(h_ref[0] + g_ref[0] * mix, nw_ref[...])


def _outproj_last_call(yssm, h_f, h_b, yb_lat, proj, d_row, w_glu, mnorm, w_out, h_prev, mod, final_w,
                       layer, nctx):
    bsz, s, d = h_prev.shape
    tr = ROW_TILE
    nct = nctx // tr
    n_lat = s - nctx
    return pl.pallas_call(
        _outproj_last_kernel,
        grid=(bsz, n_lat // tr),
        in_specs=_mixer_specs(tr, lambda i: i + nct) + [
            pl.BlockSpec((1, tr, AW), lambda b, i: (b, i, 0)),
            pl.BlockSpec((MIX_W, d), lambda b, i: (0, 0)),
            pl.BlockSpec((1, tr, d), lambda b, i: (b, i + nct, 0)),
            pl.BlockSpec((1, 1, d), lambda b, i: (layer * 8 + b, 0, 2)),
            pl.BlockSpec((1, d), lambda b, i: (0, 0)),
        ],
        out_specs=pl.BlockSpec((1, tr, d), lambda b, i: (b, i, 0)),
        out_shape=jax.ShapeDtypeStruct((bsz, n_lat, d), F32),
        compiler_params=_cparams("parallel", "parallel"),
        name="out_proj_last",
    )(yssm, proj, proj, d_row, w_glu, h_f, h_b, proj, proj, mnorm,
      yb_lat, w_out, h_prev, mod, final_w)


def _reorder_w_in(w):
    g0 = C_MO + MW
    return jnp.concatenate([w[:, :g0], w[:, g0 + 4 * MH:]], axis=1), w[:, g0:g0 + 4 * MH]


def kernel(x, c, ctx, c_ctx, norm_w, ada_w, ada_b, w_in, mlstm_gate_b, ssm_a_re, ssm_a_im, ssm_log_dt,
           ssm_b_re, ssm_b_im, ssm_c_re, ssm_c_im, ssm_d, ssm_w_glu, attn_q_norm, attn_k_norm,
           mlstm_conv_w, mlstm_conv_b, mlstm_norm_w, w_out, final_norm_w):
    bsz, n_lat, d = x.shape
    nctx = ctx.shape[1]
    s = nctx + n_lat
    depth = norm_w.shape[0]
    assert bsz < 8 and nctx % ROW_TILE == 0 and n_lat % ROW_TILE == 0 and depth == 2

    c_rows = jnp.zeros((8, d), F32).at[:bsz].set(c).at[bsz].set(c_ctx)
    mod = _ada_call(c_rows, ada_w, ada_b).reshape(depth * 8, 1, 3 * d)
    cos_all, sin_all, cos_lat, sin_lat = _rope_tables(n_lat, nctx)

    w_main, w_gr, b_gr = [], [], []
    for layer in range(depth):
        wm, wg = _reorder_w_in(w_in[layer])
        w_main.append(wm.astype(BF16))
        w_gr.append(wg.reshape(d, 2, 2 * MH).transpose(1, 2, 0).astype(BF16))
        gb = mlstm_gate_b[layer].astype(F32).reshape(2, 2 * MH)
        b_gr.append(jnp.broadcast_to(gb[:, :, None], (2, 2 * MH, 128)))

    norm_w3 = norm_w.astype(F32).reshape(depth, 1, d)
    xn, g_row = _prenorm_call(x, ctx, norm_w3, mod, w_gr[0], b_gr[0], 0)
    h_prev = None
    out = None
    for layer in range(depth):
        last = layer == depth - 1
        proj = _inproj_call(xn.reshape(bsz * s, d), w_main[layer]).reshape(bsz, s, PROJ_W)

        yssm = _s5_branch(proj, dict(a_re=ssm_a_re[layer], a_im=ssm_a_im[layer], log_dt=ssm_log_dt[layer],
                                     b_re=ssm_b_re[layer], b_im=ssm_b_im[layer], c_re=ssm_c_re[layer],
                                     c_im=ssm_c_im[layer]), nctx)

        qw = attn_q_norm[layer].reshape(1, HD).astype(F32)
        ks, vs = _attn_kv_prep_call(proj, cos_all, sin_all, attn_k_norm[layer].reshape(1, HD).astype(F32))
        yb_lat = _attn_lat_call(proj, ks, vs, cos_lat, sin_lat, qw, nctx)

        conv_w = jnp.zeros((8, 2 * MW), F32).at[:3].set(mlstm_conv_w[layer].astype(F32))
        qc, kc = _mlstm_prep_call(proj, conv_w, mlstm_conv_b[layer].astype(F32).reshape(1, 2 * MW), nctx)
        h_f, h_b = _mlstm_call(qc, kc, proj, g_row, nctx)

        d_row = ssm_d[layer].astype(F32).reshape(1, SSM_W)
        w_glu = ssm_w_glu[layer].astype(BF16)
        mnorm = mlstm_norm_w[layer].astype(F32).reshape(1, MW)
        w_o = w_out[layer].astype(BF16)
        if not last:
            yb_ctx = _attn_ctx_call(proj, ks, vs, qw, nctx)
            h_prev, xn, g_row = _outproj_mid_call(yssm, h_f, h_b, yb_ctx, yb_lat, proj, d_row, w_glu, mnorm,
                                                  w_o, x, ctx, mod, norm_w3, w_gr[layer + 1], b_gr[layer + 1],
                                                  layer)
        else:
            out = _outproj_last_call(yssm, h_f, h_b, yb_lat, proj, d_row, w_glu, mnorm, w_o, h_prev, mod,
                                     final_norm_w.reshape(1, d), layer, nctx)
    return out
```

```python
import functools
import math

import jax
import jax.numpy as jnp
from jax import lax
from jax.experimental import pallas as pl
from jax.experimental.pallas import tpu as pltpu

F32 = jnp.float32
BF16 = jnp.bfloat16
EPS = 1e-6

SSM_W = 512
SSM_P = 16
SSM_G = 32
SSM_N = 64
S5_T = 16
S5_SLAB_G = 8
S5_SLABS = SSM_G // S5_SLAB_G
S5_ST = S5_SLAB_G * SSM_N
AH = 8
AKV = 2
AGRP = AH // AKV
HD = 128
AW = AH * HD
KVW = AKV * HD
GRID_W = 64
ROPE_THETA = 10000.0
MH = 4
MD = 128
MW = MH * MD
MCH = 128
C_U, C_Z, C_AQ, C_AK, C_AV, C_AZ, C_MQ, C_MK, C_MV, C_MO, C_MZ = (
    0, 512, 1024, 2048, 2304, 2560, 3584, 4096, 4608, 5120, 5632)
PROJ_W = 6144
MIX_W = SSM_W + AW + MW

ROW_TILE = 256
VMEM_LIMIT = 48 * 1024 * 1024


def _cparams(*sem):
    return pltpu.CompilerParams(dimension_semantics=sem, vmem_limit_bytes=VMEM_LIMIT)


def _dot(a, b):
    return jnp.dot(a, b, preferred_element_type=F32)


def _dot_nt(a, b):
    return lax.dot_general(a, b, (((1,), (1,)), ((), ())), preferred_element_type=F32)


def _dot_tn(a, b):
    return lax.dot_general(a, b, (((0,), (0,)), ((), ())), preferred_element_type=F32)


def _split3(x):
    hi = x.astype(BF16)
    r1 = x - hi.astype(F32)
    mid = r1.astype(BF16)
    lo = (r1 - mid.astype(F32)).astype(BF16)
    return hi, mid, lo


def _silu(x):
    return x * jax.nn.sigmoid(x)


def _log_sigmoid(x):
    return jnp.minimum(x, 0.0) - jnp.log1p(jnp.exp(-jnp.abs(x)))


def _rms(x, w):
    return x * lax.rsqrt(jnp.mean(x * x, axis=-1, keepdims=True) + EPS) * w


def _pick_tile(n, cap, mult):
    best = mult
    for t in range(mult, min(n, cap) + 1, mult):
        if n % t == 0:
            best = t
    return best


def _ada_kernel(c_ref, w_ref, b_ref, o_ref):
    a = _silu(c_ref[...])
    w = w_ref[0]
    a_hi = a.astype(BF16)
    a_lo = (a - a_hi.astype(F32)).astype(BF16)
    w_hi = w.astype(BF16)
    w_lo = (w - w_hi.astype(F32)).astype(BF16)
    o_ref[0] = _dot(a_hi, w_hi) + _dot(a_hi, w_lo) + _dot(a_lo, w_hi) + b_ref[0]


def _ada_call(c_rows, ada_w, ada_b):
    depth, d, n3 = ada_w.shape
    tn = 1024
    return pl.pallas_call(
        _ada_kernel,
        grid=(depth, n3 // tn),
        in_specs=[
            pl.BlockSpec((8, d), lambda l, j: (0, 0)),
            pl.BlockSpec((1, d, tn), lambda l, j: (l, 0, j)),
            pl.BlockSpec((1, 1, tn), lambda l, j: (l, 0, j)),
        ],
        out_specs=pl.BlockSpec((1, 8, tn), lambda l, j: (l, 0, j)),
        out_shape=jax.ShapeDtypeStruct((depth, 8, n3), F32),
        compiler_params=_cparams("parallel", "parallel"),
        name="ada_mod",
    )(c_rows, ada_w, ada_b.reshape(depth, 1, n3))


def _prenorm_kernel(x_ref, ctx_ref, nw_ref, sh_ref, sc_ref, wgr_ref, bgr_ref, o_ref, gr_ref, *, nct):
    h = jnp.where(pl.program_id(1) < nct, ctx_ref[0], x_ref[0])
    xn = (_rms(h, nw_ref[0]) * (1.0 + sc_ref[0]) + sh_ref[0]).astype(BF16)
    o_ref[0] = xn
    for dirn in range(2):
        gr_ref[0, dirn] = _dot_nt(wgr_ref[dirn], xn) + bgr_ref[dirn][:, :1]


def _mod_spec(d, layer, part, nct, nb):
    def idx(b, i):
        return (layer * 8 + jnp.where(i < nct, nb, b), 0, part)
    return pl.BlockSpec((1, 1, d), idx)


def _prenorm_call(x, ctx, nw, mod, w_gr, b_gr, layer):
    bsz, n, d = x.shape
    nctx = ctx.shape[1]
    tr = ROW_TILE
    nct = nctx // tr
    s = nctx + n
    return pl.pallas_call(
        functools.partial(_prenorm_kernel, nct=nct),
        grid=(bsz, s // tr),
        in_specs=[
            pl.BlockSpec((1, tr, d), lambda b, i: (b, jnp.maximum(i - nct, 0), 0)),
            pl.BlockSpec((1, tr, d), lambda b, i: (b, jnp.minimum(i, nct - 1), 0)),
            pl.BlockSpec((1, 1, d), lambda b, i: (layer, 0, 0)),
            _mod_spec(d, layer, 0, nct, bsz),
            _mod_spec(d, layer, 1, nct, bsz),
            pl.BlockSpec((2, 8, d), lambda b, i: (0, 0, 0)),
            pl.BlockSpec((2, 8, 128), lambda b, i: (0, 0, 0)),
        ],
        out_specs=[
            pl.BlockSpec((1, tr, d), lambda b, i: (b, i, 0)),
            pl.BlockSpec((1, 2, 8, tr), lambda b, i: (b, 0, 0, i)),
        ],
        out_shape=[
            jax.ShapeDtypeStruct((bsz, s, d), BF16),
            jax.ShapeDtypeStruct((bsz, 2, 8, s), F32),
        ],
        compiler_params=_cparams("parallel", "parallel"),
        name="prenorm",
    )(x, ctx, nw, mod, mod, w_gr, b_gr)


def _matmul_kernel(x_ref, w_ref, o_ref):
    o_ref[...] = _dot(x_ref[...], w_ref[...])


def _inproj_call(xn2d, w):
    m, d = xn2d.shape
    n = w.shape[1]
    tm = _pick_tile(m, 1056, 16)
    tn = 1536
    return pl.pallas_call(
        _matmul_kernel,
        grid=(n // tn, m // tm),
        in_specs=[
            pl.BlockSpec((tm, d), lambda j, i: (i, 0)),
            pl.BlockSpec((d, tn), lambda j, i: (0, j)),
        ],
        out_specs=pl.BlockSpec((tm, tn), lambda j, i: (i, j)),
        out_shape=jax.ShapeDtypeStruct((m, n), F32),
        compiler_params=_cparams("parallel", "parallel"),
        name="in_proj",
    )(xn2d, w)


def _s5_weights(a_re, a_im, log_dt, b_re, b_im, c_re, c_im):
    t_ = S5_T
    a_re = a_re.astype(F32)
    a_im = a_im.astype(F32)
    dt = jnp.exp(log_dt.astype(F32))[..., None]
    mag = jnp.exp(a_re * dt)
    lam_re = mag * jnp.cos(a_im * dt)
    lam_im = mag * jnp.sin(a_im * dt)
    inv_abs2 = 1.0 / (a_re * a_re + a_im * a_im)
    num_re, num_im = lam_re - 1.0, lam_im
    f_re = (num_re * a_re + num_im * a_im) * inv_abs2
    f_im = (num_im * a_re - num_re * a_im) * inv_abs2
    b_re = b_re.astype(F32)[None]
    b_im = b_im.astype(F32)[None]
    bb_re = f_re[:, :, None, :] * b_re - f_im[:, :, None, :] * b_im
    bb_im = f_re[:, :, None, :] * b_im + f_im[:, :, None, :] * b_re
    c_re = c_re.astype(F32)
    c_im = c_im.astype(F32)

    pr = [jnp.ones_like(lam_re)]
    pi = [jnp.zeros_like(lam_im)]
    for _ in range(t_):
        pr.append(pr[-1] * lam_re - pi[-1] * lam_im)
        pi.append(pr[-2] * lam_im + pi[-1] * lam_re)

    def powers(exps_f, exps_b):
        re = jnp.stack([jnp.stack([pr[e][0] for e in exps_f]), jnp.stack([pr[e][1] for e in exps_b])])
        im = jnp.stack([jnp.stack([pi[e][0] for e in exps_f]), jnp.stack([pi[e][1] for e in exps_b])])
        return re[:, :, :, None, :], im[:, :, :, None, :]

    def slab_rows(w):
        return w.reshape(2, t_, S5_SLABS, 128, 2 * SSM_N).astype(BF16)

    steps = list(range(t_))
    p_re, p_im = powers([t_ - 1 - t for t in steps], steps)
    win = slab_rows(jnp.concatenate([bb_re[:, None] * p_re - bb_im[:, None] * p_im,
                                     bb_re[:, None] * p_im + bb_im[:, None] * p_re], axis=-1))
    p_re, p_im = powers([t + 1 for t in steps], [t_ - t for t in steps])
    wout = slab_rows(jnp.concatenate([c_re[:, None] * p_re - c_im[:, None] * p_im,
                                      -(c_re[:, None] * p_im + c_im[:, None] * p_re)], axis=-1))

    p_re, p_im = powers(steps, steps)
    kr = c_re[:, None] * p_re - c_im[:, None] * p_im
    ki = c_re[:, None] * p_im + c_im[:, None] * p_re
    kk = jnp.einsum('dgqm,dtgpm->dgqtp', jnp.concatenate([bb_re, bb_im], axis=-1),
                    jnp.concatenate([kr, -ki], axis=-1), precision=lax.Precision.HIGH)
    k_f = kk[0].reshape(S5_SLABS, 128, t_ * SSM_P)
    k_b = kk[1][:, :, ::-1, :].reshape(S5_SLABS, 128, t_ * SSM_P)

    def lam_row(v):
        return v.reshape(2, S5_SLABS, 1, S5_ST)

    lam_t = jnp.concatenate([lam_row(pr[t_]), lam_row(pi[t_])], axis=-1)
    return win, wout, k_f, k_b, lam_t


def _iota(shape, dim):
    return lax.broadcasted_iota(jnp.int32, shape, dim)


def _s5_core_kernel(u_ref, win_ref, wout_ref, kf_ref, kb_ref, lam_ref, y_ref, lhs_sc, w_sc, s_sc, acc_sc,
                    *, nch, nctx_ch):
    t_ = S5_T
    n2 = 2 * SSM_N
    st2 = 2 * S5_ST
    lg_p = SSM_P.bit_length() - 1
    lg_n = SSM_N.bit_length() - 1
    lg_st = S5_ST.bit_length() - 1

    for t in range(t_):
        lhs_sc[:, t * 128:(t + 1) * 128] = u_ref[0, pl.ds(t, nch, stride=t_), :].astype(BF16)

    k1, c1 = _iota((n2, st2), 0), _iota((n2, st2), 1)
    e_in = jnp.where(((k1 >> lg_n) == (c1 >> lg_st)) & ((k1 & (SSM_N - 1)) == (c1 & (SSM_N - 1))),
                     1.0, 0.0).astype(BF16)
    r1, c1b = _iota((128, st2), 0), _iota((128, st2), 1)
    m_in = (r1 >> lg_p) == ((c1b & (S5_ST - 1)) >> lg_n)

    def expand_state_table(tab_ref, d):
        for t in range(t_):
            rows = slice(t * 128, (t + 1) * 128)
            w_sc[rows, :st2] = jnp.where(m_in, _dot(tab_ref[d, t, 0], e_in), 0.0).astype(BF16)

    for d in range(2):
        expand_state_table(win_ref, d)
        s_sc[d] = _dot(lhs_sc[...], w_sc[:, :st2])

    lam_f = lam_ref[0, 0]
    lam_b = lam_ref[1, 0]
    lrf, lif = lam_f[:, :S5_ST], lam_f[:, S5_ST:]
    lrb, lib = lam_b[:, :S5_ST], lam_b[:, S5_ST:]

    def step(k, carry):
        hrf, hif, hrb, hib = carry
        cb = jnp.where(k < nctx_ch, nctx_ch - 1 - k, nch - 1 - (k - nctx_ch))
        sf = s_sc[0, pl.ds(k, 1), :]
        sb = s_sc[1, pl.ds(cb, 1), :]
        s_sc[0, pl.ds(k, 1), :] = jnp.concatenate([hrf, hif], axis=-1)
        s_sc[1, pl.ds(cb, 1), :] = jnp.concatenate([hrb, hib], axis=-1)
        return (lrf * hrf - lif * hif + sf[:, :S5_ST], lrf * hif + lif * hrf + sf[:, S5_ST:],
                lrb * hrb - lib * hib + sb[:, :S5_ST], lrb * hib + lib * hrb + sb[:, S5_ST:])

    zero = jnp.zeros((1, S5_ST), F32)
    lax.fori_loop(0, nch, step, (zero, zero, zero, zero))

    k2, c2 = _iota((t_ * SSM_P, t_ * 128), 0), _iota((t_ * SSM_P, t_ * 128), 1)
    e_m = jnp.where(((k2 >> lg_p) == (c2 >> 7)) & ((k2 & (SSM_P - 1)) == (c2 & (SSM_P - 1))),
                    1.0, 0.0).astype(BF16)
    r2, c2b = _iota((128, t_ * 128), 0), _iota((128, t_ * 128), 1)
    m_m = (r2 >> lg_p) == ((c2b & 127) >> lg_p)
    k_f = kf_ref[0]
    k_b = kb_ref[0]
    lag_col = _iota((128, t_ * SSM_P), 1)
    for t in range(t_):
        rows = slice(t * 128, (t + 1) * 128)
        fwd = k_f if t == 0 else pltpu.roll(k_f, t * SSM_P, 1)
        bwd = k_b if t == t_ - 1 else pltpu.roll(k_b, (t + 1) * SSM_P, 1)
        mc = (jnp.where(lag_col >= t * SSM_P, fwd, 0.0)
              + jnp.where(lag_col < (t + 1) * SSM_P, bwd, 0.0)).astype(BF16)
        w_sc[rows, :] = jnp.where(m_m, _dot(mc, e_m), 0.0).astype(BF16)
    acc_sc[...] = _dot(lhs_sc[...], w_sc[...])

    for d in range(2):
        expand_state_table(wout_ref, d)
        acc_sc[...] += _dot_nt(s_sc[d].astype(BF16), w_sc[:, :st2])

    for t in range(t_):
        y_ref[0, pl.ds(t, nch, stride=t_), :] = acc_sc[:, t * 128:(t + 1) * 128]


def _s5_core_call(proj, win, wout, k_f, k_b, lam_t, nctx):
    bsz, s, _ = proj.shape
    t_ = S5_T
    nch = s // t_
    kdim = t_ * 128
    n2 = 2 * SSM_N
    st2 = 2 * S5_ST
    cu = C_U // 128
    return pl.pallas_call(
        functools.partial(_s5_core_kernel, nch=nch, nctx_ch=nctx // t_),
        grid=(S5_SLABS, bsz),
        in_specs=[
            pl.BlockSpec((1, s, 128), lambda sl, b: (b, 0, cu + sl)),
            pl.BlockSpec((2, t_, 1, 128, n2), lambda sl, b: (0, 0, sl, 0, 0)),
            pl.BlockSpec((2, t_, 1, 128, n2), lambda sl, b: (0, 0, sl, 0, 0)),
            pl.BlockSpec((1, 128, t_ * SSM_P), lambda sl, b: (sl, 0, 0)),
            pl.BlockSpec((1, 128, t_ * SSM_P), lambda sl, b: (sl, 0, 0)),
            pl.BlockSpec((2, 1, 1, st2), lambda sl, b: (0, sl, 0, 0)),
        ],
        out_specs=pl.BlockSpec((1, s, 128), lambda sl, b: (b, 0, sl)),
        out_shape=jax.ShapeDtypeStruct((bsz, s, SSM_W), F32),
        scratch_shapes=[
            pltpu.VMEM((nch, kdim), BF16),
            pltpu.VMEM((kdim, kdim), BF16),
            pltpu.VMEM((2, nch, st2), F32),
            pltpu.VMEM((nch, kdim), F32),
        ],
        compiler_params=pltpu.CompilerParams(dimension_semantics=("parallel", "parallel"),
                                             vmem_limit_bytes=56 * 1024 * 1024),
        name="s5_core",
    )(proj, win, wout, k_f, k_b, lam_t)


def _s5_branch(proj, p, nctx):
    win, wout, k_f, k_b, lam_t = _s5_weights(p['a_re'], p['a_im'], p['log_dt'], p['b_re'], p['b_im'],
                                             p['c_re'], p['c_im'])
    return _s5_core_call(proj, win, wout, k_f, k_b, lam_t, nctx)


def _rope_tables(n_lat, nctx):
    rows = n_lat // GRID_W
    row = jnp.broadcast_to(jnp.arange(rows, dtype=F32)[:, None], (rows, GRID_W)).reshape(-1)
    col = jnp.broadcast_to(jnp.arange(GRID_W, dtype=F32)[None, :], (rows, GRID_W)).reshape(-1)
    n_freq = HD // 4
    inv = ROPE_THETA ** (-jnp.arange(n_freq, dtype=F32) / n_freq)
    ang = jnp.concatenate([row[:, None] * inv, col[:, None] * inv], axis=-1)
    cs, sn = jnp.cos(ang), jnp.sin(ang)
    cos2 = jnp.concatenate([cs, cs], axis=-1)
    sin2 = jnp.concatenate([-sn, sn], axis=-1)
    cos_all = jnp.concatenate([jnp.ones((nctx, HD), F32), cos2], axis=0)
    sin_all = jnp.concatenate([jnp.zeros((nctx, HD), F32), sin2], axis=0)
    return cos_all, sin_all, cos2, sin2


def _norm_rope(x, w, cs, sn):
    y = _rms(x, w)
    return y * cs + pltpu.roll(y, HD // 2, 1) * sn


def _attn_kv_prep_kernel(k_ref, v_ref, cos_ref, sin_ref, kw_ref, ko_ref, vo_ref):
    cs = cos_ref[...]
    sn = sin_ref[...]
    k = k_ref[0]
    for h in range(AKV):
        ko_ref[0, :, h * HD:(h + 1) * HD] = _norm_rope(k[:, h * HD:(h + 1) * HD], kw_ref[...], cs, sn).astype(BF16)
    vo_ref[0] = v_ref[0].astype(BF16)


def _attn_kv_prep_call(proj, cos2, sin2, kw):
    bsz, s, _ = proj.shape
    tr = _pick_tile(s, 3 * ROW_TILE, ROW_TILE)
    return pl.pallas_call(
        _attn_kv_prep_kernel,
        grid=(bsz, s // tr),
        in_specs=[
            pl.BlockSpec((1, tr, KVW), lambda b, i: (b, i, C_AK // KVW)),
            pl.BlockSpec((1, tr, KVW), lambda b, i: (b, i, C_AV // KVW)),
            pl.BlockSpec((tr, HD), lambda b, i: (i, 0)),
            pl.BlockSpec((tr, HD), lambda b, i: (i, 0)),
            pl.BlockSpec((1, HD), lambda b, i: (0, 0)),
        ],
        out_specs=[
            pl.BlockSpec((1, tr, KVW), lambda b, i: (b, i, 0)),
            pl.BlockSpec((1, tr, KVW), lambda b, i: (b, i, 0)),
        ],
        out_shape=[
            jax.ShapeDtypeStruct((bsz, s, KVW), BF16),
            jax.ShapeDtypeStruct((bsz, s, KVW), BF16),
        ],
        compiler_params=_cparams("parallel", "parallel"),
        name="attn_kv_prep",
    )(proj, proj, cos2, sin2, kw)


Q_SCALE = HD ** -0.5 * math.log2(math.e)


def _attn_finish(acc, l, z_refs, o_ref, tq):
    o = acc / l
    ntile = len(z_refs)
    for h in range(AGRP):
        for t, z_ref in enumerate(z_refs):
            r0 = (h * ntile + t) * tq
            z = z_ref[0][:, h * HD:(h + 1) * HD]
            o_ref[0, t * tq:(t + 1) * tq, h * HD:(h + 1) * HD] = (o[r0:r0 + tq] * _silu(z)).astype(BF16)


def _attn_ctx_kernel(q_ref, z_ref, qw_ref, k_ref, v_ref, o_ref, *, tq):
    q = q_ref[0]
    q4 = jnp.concatenate([(_rms(q[:, h * HD:(h + 1) * HD], qw_ref[...]) * Q_SCALE).astype(BF16)
                          for h in range(AGRP)], axis=0)
    s = _dot_nt(q4, k_ref[0])
    m = jnp.max(s, axis=-1, keepdims=True)
    p = jnp.exp2(s - m)
    l = jnp.sum(p, axis=-1, keepdims=True)
    _attn_finish(_dot(p.astype(BF16), v_ref[0]), l, [z_ref], o_ref, tq)


def _attn_ctx_call(proj, ks, vs, qw, nctx):
    bsz = proj.shape[0]
    tq = ROW_TILE
    zw = AGRP * HD
    return pl.pallas_call(
        functools.partial(_attn_ctx_kernel, tq=tq),
        grid=(bsz, AKV, nctx // tq),
        in_specs=[
            pl.BlockSpec((1, tq, zw), lambda b, g, i: (b, i, C_AQ // zw + g)),
            pl.BlockSpec((1, tq, zw), lambda b, g, i: (b, i, C_AZ // zw + g)),
            pl.BlockSpec((1, HD), lambda b, g, i: (0, 0)),
            pl.BlockSpec((1, nctx, HD), lambda b, g, i: (b, 0, g)),
            pl.BlockSpec((1, nctx, HD), lambda b, g, i: (b, 0, g)),
        ],
        out_specs=pl.BlockSpec((1, tq, zw), lambda b, g, i: (b, i, g)),
        out_shape=jax.ShapeDtypeStruct((bsz, nctx, AW), BF16),
        compiler_params=_cparams("parallel", "parallel", "parallel"),
        name="attention_ctx",
    )(proj, proj, qw, ks, vs)


def _attn_lat_kernel(q_ref, qn_ref, z_ref, cos_ref, sin_ref, cosn_ref, sinn_ref, qw_ref, k_ref, v_ref,
                     o_ref, q_sc, s_sc, *, tq, ck, nck):
    i = pl.program_id(2)

    def prep(qr, cr, sr):
        q = qr[0]
        return jnp.concatenate(
            [(_norm_rope(q[:, h * HD:(h + 1) * HD], qw_ref[...], cr[...], sr[...]) * Q_SCALE).astype(BF16)
             for h in range(AGRP)], axis=0)

    base = (i * nck) % 2 if nck % 2 else 0

    def slot(j):
        return (base + j) % 2

    @pl.when(i == 0)
    def _():
        q_sc[...] = prep(q_ref, cos_ref, sin_ref)
        s_sc[slot(0)] = _dot_nt(q_sc[...], k_ref[0, 0:ck, :])

    m = l = acc = None
    for j in range(nck):
        if j + 1 < nck:
            s_sc[slot(j + 1)] = _dot_nt(q_sc[...], k_ref[0, (j + 1) * ck:(j + 2) * ck, :])
        else:
            q_sc[...] = prep(qn_ref, cosn_ref, sinn_ref)
            s_sc[slot(j + 1)] = _dot_nt(q_sc[...], k_ref[0, 0:ck, :])
        s = s_sc[slot(j)]
        s_max = jnp.max(s, axis=-1, keepdims=True)
        m_new = s_max if j == 0 else jnp.maximum(m, s_max)
        p = jnp.exp2(s - m_new)
        pv = _dot(p.astype(BF16), v_ref[0, j * ck:(j + 1) * ck, :])
        p_sum = jnp.sum(p, axis=-1, keepdims=True)
        if j == 0:
            l, acc = p_sum, pv
        else:
            alpha = jnp.exp2(m - m_new)
            l = alpha * l + p_sum
            acc = alpha * acc + pv
        m = m_new
    _attn_finish(acc, l, [z_ref], o_ref, tq)


def _attn_lat_call(proj, ks, vs, cos_lat, sin_lat, qw, nctx):
    bsz, s, _ = proj.shape
    n_lat = s - nctx
    tq = ROW_TILE
    nct = nctx // tq
    ck = _pick_tile(s, 1536, 128)
    nck = s // ck
    zw = AGRP * HD
    cq = C_AQ // zw
    cz = C_AZ // zw
    nlt = n_lat // tq

    def nxt(i):
        return jnp.minimum(i + 1, nlt - 1)

    return pl.pallas_call(
        functools.partial(_attn_lat_kernel, tq=tq, ck=ck, nck=nck),
        grid=(bsz, AKV, nlt),
        in_specs=[
            pl.BlockSpec((1, tq, zw), lambda b, g, i: (b, nct + i, cq + g)),
            pl.BlockSpec((1, tq, zw), lambda b, g, i: (b, nct + nxt(i), cq + g)),
            pl.BlockSpec((1, tq, zw), lambda b, g, i: (b, nct + i, cz + g)),
            pl.BlockSpec((tq, HD), lambda b, g, i: (i, 0)),
            pl.BlockSpec((tq, HD), lambda b, g, i: (i, 0)),
            pl.BlockSpec((tq, HD), lambda b, g, i: (nxt(i), 0)),
            pl.BlockSpec((tq, HD), lambda b, g, i: (nxt(i), 0)),
            pl.BlockSpec((1, HD), lambda b, g, i: (0, 0)),
            pl.BlockSpec((1, s, HD), lambda b, g, i: (b, 0, g)),
            pl.BlockSpec((1, s, HD), lambda b, g, i: (b, 0, g)),
        ],
        out_specs=pl.BlockSpec((1, tq, zw), lambda b, g, i: (b, i, g)),
        out_shape=jax.ShapeDtypeStruct((bsz, n_lat, AW), BF16),
        scratch_shapes=[pltpu.VMEM((AGRP * tq, HD), BF16), pltpu.VMEM((2, AGRP * tq, ck), F32)],
        compiler_params=_cparams("parallel", "parallel", "arbitrary"),
        name="attention",
    )(proj, proj, proj, cos_lat, sin_lat, cos_lat, sin_lat, qw, ks, vs)


def _mlstm_prep_kernel(q_ref, k_ref, qp_ref, kp_ref, qn_ref, kn_ref, w_ref, b_ref, qo_ref, ko_ref,
                       *, tr, nctx, s):
    rid = lax.broadcasted_iota(jnp.int32, (tr, MW), 0)
    row = pl.program_id(1) * tr + rid
    starts = jnp.logical_or(row == 0, row == nctx)
    ends = jnp.logical_or(row == nctx - 1, row == s - 1)
    w = w_ref[...]
    bias = b_ref[...]

    def conv(x, prev_row, next_row, off):
        xp = jnp.where(rid == 0, prev_row, pltpu.roll(x, 1, 0))
        xn = jnp.where(rid == tr - 1, next_row, pltpu.roll(x, tr - 1, 0))
        xp = jnp.where(starts, 0.0, xp)
        xn = jnp.where(ends, 0.0, xn)
        y = (w[0:1, off:off + MW] * xp + w[1:2, off:off + MW] * x + w[2:3, off:off + MW] * xn
             + bias[:, off:off + MW])
        return _silu(y)

    qo_ref[0] = conv(q_ref[0], qp_ref[0, 7:8, :], qn_ref[0, 0:1, :], 0).astype(BF16)
    ko_ref[0] = (conv(k_ref[0], kp_ref[0, 7:8, :], kn_ref[0, 0:1, :], MW) * (MD ** -0.5)).astype(BF16)


def _mlstm_prep_call(proj, conv_w, conv_b, nctx):
    bsz, s, _ = proj.shape
    tr = _pick_tile(s, 3 * ROW_TILE, ROW_TILE)
    nt = s // tr
    r8 = tr // 8
    n8 = s // 8
    cq = C_MQ // MW
    ck = C_MK // MW

    def prev_map(c):
        return lambda b, i: (b, jnp.maximum(i * r8 - 1, 0), c)

    def next_map(c):
        return lambda b, i: (b, jnp.minimum((i + 1) * r8, n8 - 1), c)

    return pl.pallas_call(
        functools.partial(_mlstm_prep_kernel, tr=tr, nctx=nctx, s=s),
        grid=(bsz, nt),
        in_specs=[
            pl.BlockSpec((1, tr, MW), lambda b, i: (b, i, cq)),
            pl.BlockSpec((1, tr, MW), lambda b, i: (b, i, ck)),
            pl.BlockSpec((1, 8, MW), prev_map(cq)),
            pl.BlockSpec((1, 8, MW), prev_map(ck)),
            pl.BlockSpec((1, 8, MW), next_map(cq)),
            pl.BlockSpec((1, 8, MW), next_map(ck)),
            pl.BlockSpec((8, 2 * MW), lambda b, i: (0, 0)),
            pl.BlockSpec((1, 2 * MW), lambda b, i: (0, 0)),
        ],
        out_specs=[
            pl.BlockSpec((1, tr, MW), lambda b, i: (b, i, 0)),
            pl.BlockSpec((1, tr, MW), lambda b, i: (b, i, 0)),
        ],
        out_shape=[
            jax.ShapeDtypeStruct((bsz, s, MW), BF16),
            jax.ShapeDtypeStruct((bsz, s, MW), BF16),
        ],
        compiler_params=_cparams("parallel", "parallel"),
        name="mlstm_prep",
    )(proj, proj, proj, proj, proj, proj, conv_w, conv_b)


def _mlstm_bwd_chunk(step, nctx_ch, nch):
    return jnp.where(step < nctx_ch, nctx_ch - 1 - step, nch - 1 - (step - nctx_ch))


def _hi_mid(x):
    hi = x.astype(BF16)
    mid = (x - hi.astype(F32)).astype(BF16)
    return jnp.concatenate([hi, mid], axis=1)


def _mlstm_chunk(dirn, q, k, v, gr, o_ref, b, ct_sc, n_sc, m_sc, base):
    row = _iota((MCH, MCH), 0)
    col = _iota((MCH, MCH), 1)
    allowed = (col <= row) if dirn == 0 else (col >= row)
    eye = col == row
    tri = jnp.where(allowed, 1.0, 0.0).astype(BF16)
    lf_r = _log_sigmoid(gr)
    r3 = _split3(lf_r)
    b_r = _dot_nt(r3[0], tri) + _dot_nt(r3[1], tri) + _dot_nt(r3[2], tri)

    a8 = jnp.concatenate([gr[:MH] - b_r[MH:], gr[:MH] - b_r[MH:]], axis=0)
    lane = _iota((8, MCH), 1)
    cm8 = a8
    sh = 1
    while sh < MCH:
        if dirn == 0:
            cm8 = jnp.maximum(cm8, jnp.where(lane >= sh, pltpu.roll(cm8, sh, 1), -jnp.inf))
        else:
            cm8 = jnp.maximum(cm8, jnp.where(lane < MCH - sh, pltpu.roll(cm8, MCH - sh, 1), -jnp.inf))
        sh *= 2

    ones2 = jnp.ones((2 * MCH, MCH), BF16)
    lhs = []
    for h in range(MH):
        lhs += [_hi_mid(jnp.where(allowed, lf_r[MH + h:MH + h + 1, :], 0.0)),
                _hi_mid(jnp.where(eye, cm8[h:h + 1, :], 0.0)),
                _hi_mid(jnp.where(eye, gr[h:h + 1, :], 0.0))]
    rep = _dot(jnp.concatenate(lhs, axis=0), ones2)

    heads = []
    lhs2 = []
    for h in range(MH):
        st = base + h
        bc = rep[(3 * h) * MCH:(3 * h + 1) * MCH]
        cm = rep[(3 * h + 1) * MCH:(3 * h + 2) * MCH]
        lic = rep[(3 * h + 2) * MCH:(3 * h + 3) * MCH]
        br = b_r[MH + h:MH + h + 1, :]
        li_r = gr[h:h + 1, :]
        m_row = m_sc[st][:1, :]
        qh = q[:, h * MD:(h + 1) * MD]
        kh = k[:, h * MD:(h + 1) * MD]

        log_d = jnp.where(allowed, bc - br + li_r, -jnp.inf)
        m_t = bc + jnp.maximum(m_row, cm)
        s = _dot_nt(qh, kh) * jnp.exp(log_d - m_t)
        s_bf = s.astype(BF16)
        lhs2 += [s_bf, (qh.astype(F32) * n_sc[st][:1, :]).astype(BF16)]
        heads.append((st, bc, lic, m_row, m_t, qh, kh, s_bf))

    rep2 = _dot(jnp.concatenate(lhs2, axis=0), jnp.ones((MCH, MCH), BF16))

    for h, (st, bc, lic, m_row, m_t, qh, kh, s_bf) in enumerate(heads):
        vh = v[:, h * MD:(h + 1) * MD]
        s_sum = rep2[(2 * h) * MCH:(2 * h + 1) * MCH]
        qn = rep2[(2 * h + 1) * MCH:(2 * h + 2) * MCH]
        w_inter = jnp.exp(bc + m_row - m_t)
        num = _dot(s_bf, vh) + w_inter * _dot(qh, ct_sc[st].astype(BF16))
        den = s_sum + w_inter * qn
        o_ref[b, :, h * MD:(h + 1) * MD] = num / jnp.maximum(jnp.abs(den), jnp.exp(-m_t))

        b_last = bc[MCH - 1:MCH, :] if dirn == 0 else bc[0:1, :]
        log_w = b_last - bc + lic
        m_new = jnp.maximum(b_last + m_row, jnp.max(log_w, axis=0, keepdims=True))
        decay = jnp.exp(b_last + m_row - m_new)
        kw = kh.astype(F32) * jnp.exp(log_w - m_new)
        ct_sc[st] = decay * ct_sc[st] + _dot_tn(kw.astype(BF16), vh)
        n_sc[st] = jnp.broadcast_to(decay * n_sc[st][:1, :] + jnp.sum(kw, axis=0, keepdims=True), (8, MD))
        m_sc[st] = jnp.broadcast_to(m_new, (8, 128))


def _mlstm_kernel(qf_ref, kf_ref, vf_ref, grf_ref, qb_ref, kb_ref, vb_ref, grb_ref,
                  of_ref, ob_ref, ct_sc, n_sc, m_sc, *, bsz):
    @pl.when(pl.program_id(0) == 0)
    def _():
        ct_sc[...] = jnp.zeros_like(ct_sc)
        n_sc[...] = jnp.zeros_like(n_sc)
        m_sc[...] = jnp.zeros_like(m_sc)

    for b in range(bsz):
        _mlstm_chunk(0, qf_ref[b], kf_ref[b], vf_ref[b].astype(BF16), grf_ref[b, 0],
                     of_ref, b, ct_sc, n_sc, m_sc, b * MH)
        _mlstm_chunk(1, qb_ref[b], kb_ref[b], vb_ref[b].astype(BF16), grb_ref[b, 0],
                     ob_ref, b, ct_sc, n_sc, m_sc, (bsz + b) * MH)


def _mlstm_call(qc, kc, proj, g_row, nctx):
    bsz, s, _ = qc.shape
    nch = s // MCH
    nctx_ch = nctx // MCH
    cv = C_MV // MW

    def cb(st):
        return _mlstm_bwd_chunk(st, nctx_ch, nch)

    nst = 2 * bsz * MH
    return pl.pallas_call(
        functools.partial(_mlstm_kernel, bsz=bsz),
        grid=(nch,),
        in_specs=[
            pl.BlockSpec((bsz, MCH, MW), lambda st: (0, st, 0)),
            pl.BlockSpec((bsz, MCH, MW), lambda st: (0, st, 0)),
            pl.BlockSpec((bsz, MCH, MW), lambda st: (0, st, cv)),
            pl.BlockSpec((bsz, 1, 8, MCH), lambda st: (0, 0, 0, st)),
            pl.BlockSpec((bsz, MCH, MW), lambda st: (0, cb(st), 0)),
            pl.BlockSpec((bsz, MCH, MW), lambda st: (0, cb(st), 0)),
            pl.BlockSpec((bsz, MCH, MW), lambda st: (0, cb(st), cv)),
            pl.BlockSpec((bsz, 1, 8, MCH), lambda st: (0, 1, 0, cb(st))),
        ],
        out_specs=[
            pl.BlockSpec((bsz, MCH, MW), lambda st: (0, st, 0)),
            pl.BlockSpec((bsz, MCH, MW), lambda st: (0, cb(st), 0)),
        ],
        out_shape=[
            jax.ShapeDtypeStruct((bsz, s, MW), F32),
            jax.ShapeDtypeStruct((bsz, s, MW), F32),
        ],
        scratch_shapes=[
            pltpu.VMEM((nst, MD, MD), F32),
            pltpu.VMEM((nst, 8, MD), F32),
            pltpu.VMEM((nst, 8, 128), F32),
        ],
        compiler_params=_cparams("arbitrary"),
        name="mlstm_scan",
    )(qc, kc, proj, g_row, qc, kc, proj, g_row)


def _mixer_specs(tr, row_of):
    def tok(col_block):
        return pl.BlockSpec((1, tr, SSM_W), lambda b, i: (b, row_of(i), col_block))
    const = lambda shape: pl.BlockSpec(shape, lambda b, i: (0,) * len(shape))
    return [tok(0), tok(C_U // SSM_W), tok(C_Z // SSM_W), const((1, SSM_W)), const((SSM_W, SSM_W)),
            tok(0), tok(0), tok(C_MO // MW), tok(C_MZ // MW), const((1, MW))]


def _mixer_outputs(y_ref, u_ref, sz_ref, d_ref, wg_ref, hf_ref, hb_ref, mo_ref, mz_ref, mn_ref):
    g = jax.nn.gelu(y_ref[0] + d_ref[...] * u_ref[0], approximate=True)
    ya = (g * jax.nn.sigmoid(_dot(g.astype(BF16), wg_ref[...])) * _silu(sz_ref[0])).astype(BF16)
    hs = hf_ref[0] + hb_ref[0]
    og, zg, nw = mo_ref[0], mz_ref[0], mn_ref[...]
    yc = []
    for h in range(MH):
        sl = slice(h * MD, (h + 1) * MD)
        yc.append((jax.nn.sigmoid(og[:, sl]) * _rms(hs[:, sl], nw[:, sl]) * _silu(zg[:, sl])).astype(BF16))
    return ya, jnp.concatenate(yc, axis=1)


def _mix_dot(ya, yb, yc, w_ref):
    return (_dot(ya, w_ref[:SSM_W, :]) + _dot(yb, w_ref[SSM_W:SSM_W + AW, :])
            + _dot(yc, w_ref[SSM_W + AW:, :]))


def _outproj_mid_kernel(*refs, nct):
    mixer, (ybc_ref, ybl_ref, w_ref, x_ref, ctx_ref, g_ref, nw_ref, sh_ref, sc_ref,
            wgr_ref, bgr_ref, h_ref, xn_ref, gr_ref) = refs[:10], refs[10:]
    is_ctx = pl.program_id(1) < nct
    ya, yc = _mixer_outputs(*mixer)
    yb = jnp.where(is_ctx, ybc_ref[0], ybl_ref[0])
    mix = _mix_dot(ya, yb, yc, w_ref)
    h = jnp.where(is_ctx, ctx_ref[0], x_ref[0]) + g_ref[0] * mix
    h_ref[0] = h
    xn = (_rms(h, nw_ref[0]) * (1.0 + sc_ref[0]) + sh_ref[0]).astype(BF16)
    xn_ref[0] = xn
    for dirn in range(2):
        gr_ref[0, dirn] = _dot_nt(wgr_ref[dirn], xn) + bgr_ref[dirn][:, :1]


def _outproj_mid_call(yssm, h_f, h_b, yb_ctx, yb_lat, proj, d_row, w_glu, mnorm, w_out, x, ctx, mod, nw,
                      w_gr, b_gr, layer):
    bsz, s, _ = yssm.shape
    d = x.shape[-1]
    nctx = ctx.shape[1]
    tr = ROW_TILE
    nct = nctx // tr
    nxt = layer + 1
    lat = lambda b, i: (b, jnp.maximum(i - nct, 0), 0)
    cxt = lambda b, i: (b, jnp.minimum(i, nct - 1), 0)
    return pl.pallas_call(
        functools.partial(_outproj_mid_kernel, nct=nct),
        grid=(bsz, s // tr),
        in_specs=_mixer_specs(tr, lambda i: i) + [
            pl.BlockSpec((1, tr, AW), cxt),
            pl.BlockSpec((1, tr, AW), lat),
            pl.BlockSpec((MIX_W, d), lambda b, i: (0, 0)),
            pl.BlockSpec((1, tr, d), lat),
            pl.BlockSpec((1, tr, d), cxt),
            _mod_spec(d, layer, 2, nct, bsz),
            pl.BlockSpec((1, 1, d), lambda b, i: (nxt, 0, 0)),
            _mod_spec(d, nxt, 0, nct, bsz),
            _mod_spec(d, nxt, 1, nct, bsz),
            pl.BlockSpec((2, 8, d), lambda b, i: (0, 0, 0)),
            pl.BlockSpec((2, 8, 128), lambda b, i: (0, 0, 0)),
        ],
        out_specs=[
            pl.BlockSpec((1, tr, d), lambda b, i: (b, i, 0)),
            pl.BlockSpec((1, tr, d), lambda b, i: (b, i, 0)),
            pl.BlockSpec((1, 2, 8, tr), lambda b, i: (b, 0, 0, i)),
        ],
        out_shape=[
            jax.ShapeDtypeStruct((bsz, s, d), F32),
            jax.ShapeDtypeStruct((bsz, s, d), BF16),
            jax.ShapeDtypeStruct((bsz, 2, 8, s), F32),
        ],
        compiler_params=_cparams("parallel", "parallel"),
        name="out_proj_mid",
    )(yssm, proj, proj, d_row, w_glu, h_f, h_b, proj, proj, mnorm,
      yb_ctx, yb_lat, w_out, x, ctx, mod, nw, mod, mod, w_gr, b_gr)


def _outproj_last_kernel(*refs):
    mixer, (ybl_ref, w_ref, h_ref, g_ref, nw_ref, o_ref) = refs[:10], refs[10:]
    ya, yc = _mixer_outputs(*mixer)
    mix = _mix_dot(ya, ybl_ref[0], yc, w_ref)
    o_ref[0] = _rms(h_ref[0] + g_ref[0] * mix, nw_ref[...])


def _outproj_last_call(yssm, h_f, h_b, yb_lat, proj, d_row, w_glu, mnorm, w_out, h_prev, mod, final_w,
                       layer, nctx):
    bsz, s, d = h_prev.shape
    tr = ROW_TILE
    nct = nctx // tr
    n_lat = s - nctx
    return pl.pallas_call(
        _outproj_last_kernel,
        grid=(bsz, n_lat // tr),
        in_specs=_mixer_specs(tr, lambda i: i + nct) + [
            pl.BlockSpec((1, tr, AW), lambda b, i: (b, i, 0)),
            pl.BlockSpec((MIX_W, d), lambda b, i: (0, 0)),
            pl.BlockSpec((1, tr, d), lambda b, i: (b, i + nct, 0)),
            pl.BlockSpec((1, 1, d), lambda b, i: (layer * 8 + b, 0, 2)),
            pl.BlockSpec((1, d), lambda b, i: (0, 0)),
        ],
        out_specs=pl.BlockSpec((1, tr, d), lambda b, i: (b, i, 0)),
        out_shape=jax.ShapeDtypeStruct((bsz, n_lat, d), F32),
        compiler_params=_cparams("parallel", "parallel"),
        name="out_proj_last",
    )(yssm, proj, proj, d_row, w_glu, h_f, h_b, proj, proj, mnorm,
      yb_lat, w_out, h_prev, mod, final_w)


def _reorder_w_in(w):
    g0 = C_MO + MW
    return jnp.concatenate([w[:, :g0], w[:, g0 + 4 * MH:]], axis=1), w[:, g0:g0 + 4 * MH]


def kernel(x, c, ctx, c_ctx, norm_w, ada_w, ada_b, w_in, mlstm_gate_b, ssm_a_re, ssm_a_im, ssm_log_dt,
           ssm_b_re, ssm_b_im, ssm_c_re, ssm_c_im, ssm_d, ssm_w_glu, attn_q_norm, attn_k_norm,
           mlstm_conv_w, mlstm_conv_b, mlstm_norm_w, w_out, final_norm_w):
    bsz, n_lat, d = x.shape
    nctx = ctx.shape[1]
    s = nctx + n_lat
    depth = norm_w.shape[0]
    assert bsz < 8 and nctx % ROW_TILE == 0 and n_lat % ROW_TILE == 0 and depth == 2

    c_rows = jnp.zeros((8, d), F32).at[:bsz].set(c).at[bsz].set(c_ctx)
    mod = _ada_call(c_rows, ada_w, ada_b).reshape(depth * 8, 1, 3 * d)
    cos_all, sin_all, cos_lat, sin_lat = _rope_tables(n_lat, nctx)

    w_main, w_gr, b_gr = [], [], []
    for layer in range(depth):
        wm, wg = _reorder_w_in(w_in[layer])
        w_main.append(wm.astype(BF16))
        w_gr.append(wg.reshape(d, 2, 2 * MH).transpose(1, 2, 0).astype(BF16))
        gb = mlstm_gate_b[layer].astype(F32).reshape(2, 2 * MH)
        b_gr.append(jnp.broadcast_to(gb[:, :, None], (2, 2 * MH, 128)))

    norm_w3 = norm_w.astype(F32).reshape(depth, 1, d)
    xn, g_row = _prenorm_call(x, ctx, norm_w3, mod, w_gr[0], b_gr[0], 0)
    h_prev = None
    out = None
    for layer in range(depth):
        last = layer == depth - 1
        proj = _inproj_call(xn.reshape(bsz * s, d), w_main[layer]).reshape(bsz, s, PROJ_W)

        yssm = _s5_branch(proj, dict(a_re=ssm_a_re[layer], a_im=ssm_a_im[layer], log_dt=ssm_log_dt[layer],
                                     b_re=ssm_b_re[layer], b_im=ssm_b_im[layer], c_re=ssm_c_re[layer],
                                     c_im=ssm_c_im[layer]), nctx)

        qw = attn_q_norm[layer].reshape(1, HD).astype(F32)
        ks, vs = _attn_kv_prep_call(proj, cos_all, sin_all, attn_k_norm[layer].reshape(1, HD).astype(F32))
        yb_lat = _attn_lat_call(proj, ks, vs, cos_lat, sin_lat, qw, nctx)

        conv_w = jnp.zeros((8, 2 * MW), F32).at[:3].set(mlstm_conv_w[layer].astype(F32))
        qc, kc = _mlstm_prep_call(proj, conv_w, mlstm_conv_b[layer].astype(F32).reshape(1, 2 * MW), nctx)
        h_f, h_b = _mlstm_call(qc, kc, proj, g_row, nctx)

        d_row = ssm_d[layer].astype(F32).reshape(1, SSM_W)
        w_glu = ssm_w_glu[layer].astype(BF16)
        mnorm = mlstm_norm_w[layer].astype(F32).reshape(1, MW)
        w_o = w_out[layer].astype(BF16)
        if not last:
            yb_ctx = _attn_ctx_call(proj, ks, vs, qw, nctx)
            h_prev, xn, g_row = _outproj_mid_call(yssm, h_f, h_b, yb_ctx, yb_lat, proj, d_row, w_glu, mnorm,
                                                  w_o, x, ctx, mod, norm_w3, w_gr[layer + 1], b_gr[layer + 1],
                                                  layer)
        else:
            out = _outproj_last_call(yssm, h_f, h_b, yb_lat, proj, d_row, w_glu, mnorm, w_o, h_prev, mod,
                                     final_norm_w.reshape(1, d), layer, nctx)
    return out
```
